```python
import math
import jax, jax.numpy as jnp
from jax import lax
import numpy as np

D_MODEL = 2048
BATCH = 1
SEQ = 8192
DEPTH = 2

CONV_CH = D_MODEL // 4
NSA_HEAD_DIM = 128
NSA_HEADS = D_MODEL // 256
NSA_KV_HEADS = NSA_HEADS // 4
NSA_WIDTH = NSA_HEADS * NSA_HEAD_DIM
NSA_KV = NSA_KV_HEADS * NSA_HEAD_DIM
RET_HEADS = D_MODEL // 512
RET_KEY_DIM = 128
RET_VAL_DIM = 128
RET_QK = RET_HEADS * RET_KEY_DIM
RET_WIDTH = RET_HEADS * RET_VAL_DIM
MIX_WIDTH = CONV_CH + NSA_WIDTH + RET_WIDTH
IN_SPLITS = (CONV_CH, CONV_CH, NSA_WIDTH, NSA_KV, NSA_KV, NSA_KV, NSA_KV, NSA_KV, NSA_KV, 3 * NSA_HEADS,
             RET_QK, RET_QK, RET_WIDTH, RET_WIDTH)
IN_WIDTH = 2 * CONV_CH + NSA_WIDTH + 6 * NSA_KV + 3 * NSA_HEADS + 2 * RET_QK + 2 * RET_WIDTH

CONV_WIDTH = 31
CMP_BLOCK = 32
CMP_STRIDE = 16
SEL_BLOCK = 64
SEL_TOPN = 16
WINDOW = 512
Q_BLOCK = 128
ROPE_THETA = 500000.0
ROPE_DIM = NSA_HEAD_DIM // 4
RET_CHUNK = 128
RET_THETA = 10000.0
N_EXPERTS = 32
TOP_K = 4
D_EXPERT = D_MODEL
SWIGLU_LIMIT = 7.0
SWIGLU_ALPHA = 1.702
MOE_ROWS = 256

NORM_EPS = 1e-6
NEG = -1e30
BIG = 1e30
F32 = jnp.float32

kernel_name = 'hybrid_conv_nsa_retention_moe_adaln'


def rms_norm(x, g):
    xf = x.astype(F32)
    y = xf * lax.rsqrt(jnp.mean(xf * xf, axis=-1, keepdims=True) + NORM_EPS)
    return (y * g.astype(F32)).astype(x.dtype)


def layer_norm(x, g, b):
    xf = x.astype(F32)
    mu = jnp.mean(xf, axis=-1, keepdims=True)
    var = jnp.mean(jnp.square(xf - mu), axis=-1, keepdims=True)
    y = (xf - mu) * lax.rsqrt(var + NORM_EPS) * g.astype(F32) + b.astype(F32)
    return y.astype(x.dtype)


def rotary(x, pos, rot_dim, theta):
    half = rot_dim // 2
    inv = theta ** (-jnp.arange(half, dtype=F32) * 2.0 / rot_dim)
    ang = pos.astype(F32)[:, None] * inv[None, :]
    cos, sin = jnp.cos(ang), jnp.sin(ang)
    xf = x.astype(F32)
    x1, x2 = xf[..., :half], xf[..., half:rot_dim]
    out = jnp.concatenate([x1 * cos - x2 * sin, x2 * cos + x1 * sin, xf[..., rot_dim:]], axis=-1)
    return out.astype(x.dtype)


def split_heads(t, n):
    b, s, _ = t.shape
    return t.reshape(b, s, n, -1).transpose(0, 2, 1, 3)


def split_cols(t, sizes):
    out, off = [], 0
    for s in sizes:
        out.append(t[..., off:off + s])
        off += s
    return out


def masked_softmax(s, mask):
    return jax.nn.softmax(jnp.where(mask, s, NEG), axis=-1)


def conv_module(val, gate, dw_w, dw_b, ln_g, ln_b):
    h = val * jax.nn.sigmoid(gate)
    h = lax.conv_general_dilated(h, dw_w[:, None, :].astype(h.dtype), window_strides=(1,),
                                 padding=[(CONV_WIDTH - 1, 0)],
                                 dimension_numbers=('NWC', 'WIO', 'NWC'),
                                 feature_group_count=CONV_CH) + dw_b
    h = layer_norm(h, ln_g, ln_b)
    return jax.nn.silu(h)


def nsa_attention(q, k_cmp, v_cmp, k_sel, v_sel, k_win, v_win, gates,
                  pe_k, pe_v, cmp_k_w1, cmp_k_w2, cmp_v_w1, cmp_v_w2):
    B, S, _ = q.shape
    hd, hkv = NSA_HEAD_DIM, NSA_KV_HEADS
    grp = NSA_HEADS // NSA_KV_HEADS
    scale = hd ** -0.5
    pos = jnp.arange(S)
    q = rotary(split_heads(q, NSA_HEADS), pos, ROPE_DIM, ROPE_THETA)
    k_cmp = rotary(split_heads(k_cmp, hkv), pos, ROPE_DIM, ROPE_THETA)
    k_sel = rotary(split_heads(k_sel, hkv), pos, ROPE_DIM, ROPE_THETA)
    k_win = rotary(split_heads(k_win, hkv), pos, ROPE_DIM, ROPE_THETA)
    v_cmp, v_sel, v_win = split_heads(v_cmp, hkv), split_heads(v_sel, hkv), split_heads(v_win, hkv)

    n_cmp = (S - CMP_BLOCK) // CMP_STRIDE + 1
    cmp_idx = np.arange(n_cmp)[:, None] * CMP_STRIDE + np.arange(CMP_BLOCK)[None, :]
    cmp_end = jnp.asarray(cmp_idx[:, -1])

    def compress(t, pe, w1, w2):
        blocks = t[:, :, cmp_idx, :] + pe
        flat = blocks.reshape(B, hkv, n_cmp, CMP_BLOCK * hd)
        return jax.nn.silu(flat @ w1) @ w2

    kc = compress(k_cmp, pe_k, cmp_k_w1, cmp_k_w2)
    vc = compress(v_cmp, pe_v, cmp_v_w1, cmp_v_w2)

    n_sel = S // SEL_BLOCK
    top_n = min(SEL_TOPN, n_sel)
    sel_start = np.arange(n_sel) * SEL_BLOCK
    cover = jnp.asarray(((cmp_idx[:, :1] <= sel_start[None, :] + SEL_BLOCK - 1)
                         & (cmp_idx[:, -1:] >= sel_start[None, :])).astype(np.float32))
    k_sel_blk = k_sel.reshape(B, hkv, n_sel, SEL_BLOCK, hd)
    v_sel_blk = v_sel.reshape(B, hkv, n_sel, SEL_BLOCK, hd)
    b_ix = jnp.arange(B)[:, None, None, None]
    h_ix = jnp.arange(hkv)[None, :, None, None]

    pad = ((0, 0), (0, 0), (WINDOW, 0), (0, 0))
    k_win_p, v_win_p = jnp.pad(k_win, pad), jnp.pad(v_win, pad)

    def query_block(qb):
        q0 = qb * Q_BLOCK
        qg = lax.dynamic_slice_in_dim(q, q0, Q_BLOCK, axis=2).reshape(B, hkv, grp, Q_BLOCK, hd)
        tpos = q0 + jnp.arange(Q_BLOCK)
        s = jnp.einsum('bkgqd,bknd->bkgqn', qg, kc).astype(F32) * scale
        mc = cmp_end[None, :] <= tpos[:, None]
        p = masked_softmax(s, mc) * jnp.any(mc, axis=-1)[:, None].astype(F32)
        o_c = jnp.einsum('bkgqn,bknd->bkgqd', p.astype(vc.dtype), vc)
        imp = jnp.einsum('bkgqn,nj->bkqj', p, cover)
        blk = jnp.arange(n_sel)
        imp = jnp.where(blk[None, :] * SEL_BLOCK <= tpos[:, None], imp, NEG)
        forced = (blk[None, :] == 0) | (blk[None, :] == (tpos // SEL_BLOCK)[:, None])
        imp = jnp.where(forced, BIG, imp)
        top_v, top_i = lax.top_k(imp, top_n)
        kg = k_sel_blk[b_ix, h_ix, top_i].reshape(B, hkv, Q_BLOCK, top_n * SEL_BLOCK, hd)
        vg = v_sel_blk[b_ix, h_ix, top_i].reshape(B, hkv, Q_BLOCK, top_n * SEL_BLOCK, hd)
        kpos = top_i[..., None] * SEL_BLOCK + jnp.arange(SEL_BLOCK)
        ms = ((top_v > 0.5 * NEG)[..., None] & (kpos <= tpos[:, None, None]))
        ms = ms.reshape(B, hkv, Q_BLOCK, top_n * SEL_BLOCK)
        s = jnp.einsum('bkgqd,bkqmd->bkgqm', qg, kg).astype(F32) * scale
        p = masked_softmax(s, ms[:, :, None])
        o_s = jnp.einsum('bkgqm,bkqmd->bkgqd', p.astype(vg.dtype), vg)
        kw = lax.dynamic_slice_in_dim(k_win_p, q0, WINDOW + Q_BLOCK, axis=2)
        vw = lax.dynamic_slice_in_dim(v_win_p, q0, WINDOW + Q_BLOCK, axis=2)
        kpw = q0 - WINDOW + jnp.arange(WINDOW + Q_BLOCK)
        mw = ((kpw[None, :] <= tpos[:, None]) & (kpw[None, :] > tpos[:, None] - WINDOW)
              & (kpw[None, :] >= 0))
        s = jnp.einsum('bkgqd,bkmd->bkgqm', qg, kw).astype(F32) * scale
        p = masked_softmax(s, mw)
        o_w = jnp.einsum('bkgqm,bkmd->bkgqd', p.astype(vw.dtype), vw)
        return o_c, o_s, o_w

    o_c, o_s, o_w = lax.map(query_block, jnp.arange(S // Q_BLOCK))

    def merge(o):
        return o.transpose(1, 0, 4, 2, 3, 5).reshape(B, S, NSA_HEADS, hd)

    g = jax.nn.sigmoid(gates.astype(F32)).reshape(B, S, 3, NSA_HEADS, 1).astype(q.dtype)
    o = g[:, :, 0] * merge(o_c) + g[:, :, 1] * merge(o_s) + g[:, :, 2] * merge(o_w)
    return o.reshape(B, S, NSA_WIDTH)


def retention(q, k, v, g, gn_g, gn_b):
    B, S, _ = q.shape
    H, dk, dv, C = RET_HEADS, RET_KEY_DIM, RET_VAL_DIM, RET_CHUNK
    n_ch = S // C
    pos = jnp.arange(S)
    q = rotary(split_heads(q, H), pos, dk, RET_THETA).astype(F32)
    k = rotary(split_heads(k, H), pos, dk, RET_THETA).astype(F32) * (dk ** -0.5)
    v = split_heads(v, H).astype(F32)
    qc = q.reshape(B, H, n_ch, C, dk)
    kc = k.reshape(B, H, n_ch, C, dk)
    vc = v.reshape(B, H, n_ch, C, dv)
    log_g = jnp.log(1.0 - 2.0 ** (-5.0 - jnp.arange(H, dtype=F32)))
    i = jnp.arange(C, dtype=F32)
    diff = i[:, None] - i[None, :]
    dmat = jnp.where(diff >= 0, jnp.exp(log_g[:, None, None] * jnp.maximum(diff, 0.0)), 0.0)
    inner = jnp.einsum('bhncd,bhnmd->bhncm', qc, kc) * dmat[:, None]
    inner = jnp.einsum('bhncm,bhnme->bhnce', inner, vc)
    zeta = jnp.exp(log_g[:, None] * (C - 1.0 - i))
    kv = jnp.einsum('bhnmd,bhnme->bhnde', kc * zeta[:, None, :, None], vc)
    decay_c = jnp.exp(log_g * C)[:, None, None]

    def step(state, kv_n):
        return decay_c * state + kv_n, state

    _, s_prev = lax.scan(step, jnp.zeros((B, H, dk, dv), F32), jnp.moveaxis(kv, 2, 0))
    s_prev = jnp.moveaxis(s_prev, 0, 2)
    xi = jnp.exp(log_g[:, None] * (i + 1.0))
    cross = jnp.einsum('bhncd,bhnde->bhnce', qc, s_prev) * xi[:, None, :, None]
    o = (inner + cross).reshape(B, H, S, dv)
    mu = jnp.mean(o, axis=-1, keepdims=True)
    var = jnp.mean(jnp.square(o - mu), axis=-1, keepdims=True)
    o = ((o - mu) * lax.rsqrt(var + NORM_EPS)).transpose(0, 2, 1, 3).reshape(B, S, RET_WIDTH)
    o = o * gn_g.astype(F32) + gn_b.astype(F32)
    return (o * jax.nn.silu(g.astype(F32))).astype(g.dtype)


def hybrid_mixer(h, w_in, w_out, conv_dw_w, conv_dw_b, conv_ln_g, conv_ln_b,
                 pe_k, pe_v, cmp_k_w1, cmp_k_w2, cmp_v_w1, cmp_v_w2, ret_gn_g, ret_gn_b):
    proj = h @ w_in
    (cv, cg, nq, nkc, nvc, nks, nvs, nkw, nvw, ngt, rq, rk, rv, rg) = split_cols(proj, IN_SPLITS)
    y_conv = conv_module(cv, cg, conv_dw_w, conv_dw_b, conv_ln_g, conv_ln_b)
    y_nsa = nsa_attention(nq, nkc, nvc, nks, nvs, nkw, nvw, ngt,
                          pe_k, pe_v, cmp_k_w1, cmp_k_w2, cmp_v_w1, cmp_v_w2)
    y_ret = retention(rq, rk, rv, rg, ret_gn_g, ret_gn_b)
    y = jnp.concatenate([y_conv, y_nsa, y_ret], axis=-1)
    return y @ w_out


def moe_ffn(h, router_w, router_b, w_gu, b_gu, w_dn, b_dn):
    B, S, D = h.shape
    T = B * S
    xf = h.reshape(T, D)
    logits = (xf @ router_w + router_b).astype(F32)
    top_val, top_exp = lax.top_k(logits, TOP_K)
    gate = jax.nn.softmax(top_val, axis=-1)
    n_assign = T * TOP_K
    flat_e = top_exp.reshape(-1)
    flat_t = jnp.repeat(jnp.arange(T, dtype=jnp.int32), TOP_K)
    flat_w = gate.reshape(-1)
    order = jnp.argsort(flat_e)
    se, st, sw = flat_e[order], flat_t[order], flat_w[order]
    counts = jnp.bincount(flat_e, length=N_EXPERTS)
    starts = jnp.cumsum(counts) - counts
    padded = (counts + MOE_ROWS - 1) // MOE_ROWS * MOE_ROWS
    pend = jnp.cumsum(padded)
    pstart = pend - padded
    dest = pstart[se] + jnp.arange(n_assign) - starts[se]
    n_slots = (n_assign + MOE_ROWS - 1) // MOE_ROWS * MOE_ROWS + N_EXPERTS * MOE_ROWS
    n_blocks = n_slots // MOE_ROWS
    slot_tok = jnp.zeros((n_slots,), jnp.int32).at[dest].set(st)
    slot_w = jnp.zeros((n_slots,), F32).at[dest].set(sw)
    blk_exp = jnp.minimum(jnp.searchsorted(pend, jnp.arange(n_blocks) * MOE_ROWS, side='right'),
                          N_EXPERTS - 1)

    def expert_block(args):
        tok, wt, e = args
        xb = xf[tok]
        gu = xb @ w_gu[e] + b_gu[e]
        glu, lin = gu[:, :D_EXPERT], gu[:, D_EXPERT:]
        glu = jnp.minimum(glu, SWIGLU_LIMIT)
        lin = jnp.clip(lin, -SWIGLU_LIMIT, SWIGLU_LIMIT)
        act = glu * jax.nn.sigmoid(SWIGLU_ALPHA * glu) * (lin + 1)
        return (act @ w_dn[e] + b_dn[e]) * wt[:, None].astype(xf.dtype)

    ys = lax.map(expert_block, (slot_tok.reshape(n_blocks, MOE_ROWS),
                                slot_w.reshape(n_blocks, MOE_ROWS), blk_exp))
    out = jax.ops.segment_sum(ys.reshape(n_slots, D), slot_tok, num_segments=T)
    return out.reshape(B, S, D)


def _uinit(key, shape, std):
    a = std * math.sqrt(3.0)
    return jax.random.uniform(key, shape, F32, -a, a)


def setup_inputs(seed: int = 0) -> dict:
    key = jax.random.key(seed)
    ks = jax.random.split(key, 32)
    hd = NSA_HEAD_DIM
    nrm = jax.random.normal
    return {
        'x': nrm(ks[0], (BATCH, SEQ, D_MODEL), F32),
        'c': nrm(ks[1], (BATCH, D_MODEL), F32),
        'ada_w': _uinit(ks[2], (DEPTH, D_MODEL, 6 * D_MODEL), 0.5 * D_MODEL ** -0.5),
        'ada_b': 0.01 * nrm(ks[3], (DEPTH, 6 * D_MODEL), F32),
        'norm_mix_g': 1.0 + 0.01 * nrm(ks[4], (DEPTH, D_MODEL), F32),
        'w_in': _uinit(ks[5], (DEPTH, D_MODEL, IN_WIDTH), D_MODEL ** -0.5),
        'conv_dw_w': _uinit(ks[6], (DEPTH, CONV_WIDTH, CONV_CH), CONV_WIDTH ** -0.5),
        'conv_dw_b': 0.01 * nrm(ks[7], (DEPTH, CONV_CH), F32),
        'conv_ln_g': 1.0 + 0.01 * nrm(ks[8], (DEPTH, CONV_CH), F32),
        'conv_ln_b': 0.01 * nrm(ks[9], (DEPTH, CONV_CH), F32),
        'nsa_pe_k': 0.02 * nrm(ks[10], (DEPTH, CMP_BLOCK, hd), F32),
        'nsa_pe_v': 0.02 * nrm(ks[11], (DEPTH, CMP_BLOCK, hd), F32),
        'nsa_cmp_k_w1': _uinit(ks[12], (DEPTH, CMP_BLOCK * hd, hd), (CMP_BLOCK * hd) ** -0.5),
        'nsa_cmp_k_w2': _uinit(ks[13], (DEPTH, hd, hd), hd ** -0.5),
        'nsa_cmp_v_w1': _uinit(ks[14], (DEPTH, CMP_BLOCK * hd, hd), (CMP_BLOCK * hd) ** -0.5),
        'nsa_cmp_v_w2': _uinit(ks[15], (DEPTH, hd, hd), hd ** -0.5),
        'ret_gn_g': 1.0 + 0.01 * nrm(ks[16], (DEPTH, RET_WIDTH), F32),
        'ret_gn_b': 0.01 * nrm(ks[17], (DEPTH, RET_WIDTH), F32),
        'w_out': _uinit(ks[18], (DEPTH, MIX_WIDTH, D_MODEL), MIX_WIDTH ** -0.5),
        'norm_ffn_g': 1.0 + 0.01 * nrm(ks[19], (DEPTH, D_MODEL), F32),
        'router_w': _uinit(ks[20], (DEPTH, D_MODEL, N_EXPERTS), D_MODEL ** -0.5),
        'router_b': 0.01 * nrm(ks[21], (DEPTH, N_EXPERTS), F32),
        'moe_w_gate_up': _uinit(ks[22], (DEPTH, N_EXPERTS, D_MODEL, 2 * D_EXPERT), D_MODEL ** -0.5),
        'moe_b_gate_up': 0.01 * nrm(ks[23], (DEPTH, N_EXPERTS, 2 * D_EXPERT), F32),
        'moe_w_down': _uinit(ks[24], (DEPTH, N_EXPERTS, D_EXPERT, D_MODEL), D_EXPERT ** -0.5),
        'moe_b_down': 0.01 * nrm(ks[25], (DEPTH, N_EXPERTS, D_MODEL), F32),
        'final_norm_g': 1.0 + 0.01 * nrm(ks[26], (D_MODEL,), F32),
    }


def reference(x, c, ada_w, ada_b, norm_mix_g, w_in, conv_dw_w, conv_dw_b, conv_ln_g, conv_ln_b,
              nsa_pe_k, nsa_pe_v, nsa_cmp_k_w1, nsa_cmp_k_w2, nsa_cmp_v_w1, nsa_cmp_v_w2,
              ret_gn_g, ret_gn_b, w_out, norm_ffn_g, router_w, router_b,
              moe_w_gate_up, moe_b_gate_up, moe_w_down, moe_b_down, final_norm_g):
    b = c.shape[0]
    for l in range(DEPTH):
        mod = (jax.nn.silu(c) @ ada_w[l] + ada_b[l]).reshape(b, 6, D_MODEL)[:, :, None, :]
        shift_a, scale_a, gate_a, shift_f, scale_f, gate_f = [mod[:, i] for i in range(6)]
        h = rms_norm(x, norm_mix_g[l]) * (1 + scale_a) + shift_a
        y = hybrid_mixer(h, w_in[l], w_out[l], conv_dw_w[l], conv_dw_b[l], conv_ln_g[l], conv_ln_b[l],
                         nsa_pe_k[l], nsa_pe_v[l], nsa_cmp_k_w1[l], nsa_cmp_k_w2[l],
                         nsa_cmp_v_w1[l], nsa_cmp_v_w2[l], ret_gn_g[l], ret_gn_b[l])
        x = x + gate_a * y
        h = rms_norm(x, norm_ffn_g[l]) * (1 + scale_f) + shift_f
        x = x + gate_f * moe_ffn(h, router_w[l], router_b[l], moe_w_gate_up[l], moe_b_gate_up[l],
                                 moe_w_down[l], moe_b_down[l])
    return rms_norm(x, final_norm_g)
```

```python
import functools
import math

import numpy as np
import jax
import jax.numpy as jnp
from jax import lax
from jax.experimental import pallas as pl
from jax.experimental.pallas import tpu as pltpu

F32 = jnp.float32
BF16 = jnp.bfloat16
HI = lax.Precision.HIGHEST

D_MODEL = 2048
DEPTH = 2
CONV_CH = 512
CONV_WIDTH = 31
HD = 128
NSA_HEADS = 8
NSA_KV_HEADS = 2
NSA_GROUP = 4
CMP_BLOCK = 32
CMP_STRIDE = 16
SEL_BLOCK = 64
SEL_TOPN = 16
WINDOW = 512
ROPE_THETA = 500000.0
ROPE_DIM = 32
RET_HEADS = 4
RET_CHUNK = 128
RET_THETA = 10000.0
N_EXPERTS = 32
TOP_K = 4
D_EXPERT = 2048
SWIGLU_LIMIT = 7.0
SWIGLU_ALPHA = 1.702
NORM_EPS = 1e-6
NEG = -1e30
BIG = 1e30

LANES = 128
VMEM_LIMIT = 56 * 1024 * 1024

PROJ_W = 45 * LANES
CB_CV, CB_CG, CB_Q, CB_KC, CB_VC, CB_KS, CB_VS, CB_KW, CB_VW, CB_GT = 0, 4, 8, 16, 18, 20, 22, 24, 26, 28
CB_RQ, CB_RK, CB_RV, CB_RG = 29, 33, 37, 41
GATE_OFF = 2 * CONV_CH + NSA_HEADS * HD + 6 * NSA_KV_HEADS * HD
RET_OFF = GATE_OFF + 3 * NSA_HEADS

MOE_SUB = 256
MOE_SUBMAX = 5
MOE_TE = 512


def _sigmoid(x):
    return 1.0 / (1.0 + jnp.exp(-x))


def _cparams(sem, vmem=VMEM_LIMIT):
    return pltpu.CompilerParams(dimension_semantics=sem, vmem_limit_bytes=vmem)


def _ada_kernel(c_ref, w_ref, b_ref, o_ref, sc_ref):
    @pl.when(pl.program_id(1) == 0)
    def _():
        cv = c_ref[...]
        sc_ref[...] = cv * _sigmoid(cv)

    tn = o_ref.shape[-1]

    def body(i, acc):
        k0 = pl.multiple_of(i * 64, 64)
        p = w_ref[0, pl.ds(k0, 64), :] * sc_ref[pl.ds(k0, 64), :]
        return acc + p.reshape(8, 8, tn).sum(axis=0)

    acc = lax.fori_loop(0, D_MODEL // 64, body, jnp.zeros((8, tn), F32))
    o_ref[0] = jnp.sum(acc, axis=0, keepdims=True) + b_ref[0]


def _ada_mod(c, ada_w, ada_b):
    tn = 1024
    n = 6 * D_MODEL
    return pl.pallas_call(
        _ada_kernel,
        out_shape=jax.ShapeDtypeStruct((DEPTH, 1, n), F32),
        grid=(DEPTH, n // tn),
        in_specs=[pl.BlockSpec((D_MODEL, 1), lambda l, j: (0, 0)),
                  pl.BlockSpec((1, D_MODEL, tn), lambda l, j: (l, 0, j)),
                  pl.BlockSpec((1, 1, tn), lambda l, j: (l, 0, j))],
        out_specs=pl.BlockSpec((1, 1, tn), lambda l, j: (l, 0, j)),
        scratch_shapes=[pltpu.VMEM((D_MODEL, 1), F32)],
        compiler_params=_cparams(("arbitrary", "arbitrary")),
        name="ada_mod",
    )(c.reshape(D_MODEL, 1), ada_w, ada_b.reshape(DEPTH, 1, n))


def _modulated_norm(x, g, scale, shift):
    ms = jnp.mean(x * x, axis=-1, keepdims=True)
    return x * lax.rsqrt(ms + NORM_EPS) * g * (1.0 + scale) + shift


def _inproj_kernel(x_ref, g_ref, sc_ref, sh_ref, w_ref, cn_ref, s1_ref, s2_ref, cr_ref, sr_ref,
                   o_ref, h_ref):
    j = pl.program_id(1)

    @pl.when(j == 0)
    def _():
        h_ref[...] = _modulated_norm(x_ref[...], g_ref[...], sc_ref[...], sh_ref[...]).astype(BF16)

    acc = jnp.dot(h_ref[...], w_ref[...], preferred_element_type=F32)
    nsub = o_ref.shape[-1] // LANES
    for c in range(nsub):
        cb = j * nsub + c
        sub = acc[:, c * LANES:(c + 1) * LANES]
        is_nsa = (((cb >= CB_Q) & (cb < CB_VC)) | ((cb >= CB_KS) & (cb < CB_VS))
                  | ((cb >= CB_KW) & (cb < CB_VW)))
        is_ret = (cb >= CB_RQ) & (cb < CB_RV)

        @pl.when(is_nsa)
        def _():
            o_ref[:, c * LANES:(c + 1) * LANES] = (
                sub * cn_ref[...] + pltpu.roll(sub, LANES - ROPE_DIM // 2, 1) * s1_ref[...]
                + pltpu.roll(sub, ROPE_DIM // 2, 1) * s2_ref[...])

        @pl.when(is_ret)
        def _():
            o_ref[:, c * LANES:(c + 1) * LANES] = sub * cr_ref[...] + pltpu.roll(sub, HD // 2, 1) * sr_ref[...]

        @pl.when(jnp.logical_not(is_nsa | is_ret))
        def _():
            o_ref[:, c * LANES:(c + 1) * LANES] = sub


def _in_proj(x2, g, scale, shift, w_bf, tabs):
    S = x2.shape[0]
    tm, tn = 512, 640
    row = pl.BlockSpec((1, D_MODEL), lambda i, j: (0, 0))
    tab = pl.BlockSpec((tm, LANES), lambda i, j: (i, 0))
    return pl.pallas_call(
        _inproj_kernel,
        out_shape=jax.ShapeDtypeStruct((S, PROJ_W), F32),
        grid=(S // tm, PROJ_W // tn),
        in_specs=[pl.BlockSpec((tm, D_MODEL), lambda i, j: (i, 0)), row, row, row,
                  pl.BlockSpec((D_MODEL, tn), lambda i, j: (0, j)), tab, tab, tab, tab, tab],
        out_specs=pl.BlockSpec((tm, tn), lambda i, j: (i, j)),
        scratch_shapes=[pltpu.VMEM((tm, D_MODEL), BF16)],
        compiler_params=_cparams(("arbitrary", "arbitrary")),
        name="in_proj",
    )(x2, g, scale, shift, w_bf, *tabs)


def _rotary_tables(S):
    pos = jnp.arange(S, dtype=F32)[:, None]
    lane = np.arange(LANES)
    half = ROPE_DIM // 2
    inv = ROPE_THETA ** (-jnp.arange(half, dtype=F32) * 2.0 / ROPE_DIM)
    ang = pos * inv[None, :]
    cos, sin = jnp.cos(ang), jnp.sin(ang)
    ones = jnp.ones((S, LANES - ROPE_DIM), F32)
    zeros_r = jnp.zeros((S, LANES - ROPE_DIM), F32)
    zeros_h = jnp.zeros((S, half), F32)
    cn = jnp.concatenate([cos, cos, ones], axis=1)
    s1 = jnp.concatenate([-sin, zeros_h, zeros_r], axis=1)
    s2 = jnp.concatenate([zeros_h, sin, zeros_r], axis=1)
    halfr = HD // 2
    invr = RET_THETA ** (-jnp.arange(halfr, dtype=F32) * 2.0 / HD)
    angr = pos * invr[None, :]
    cosr, sinr = jnp.cos(angr), jnp.sin(angr)
    cr = jnp.concatenate([cosr, cosr], axis=1)
    sr = jnp.concatenate([-sinr, sinr], axis=1)
    del lane
    return cn, s1, s2, cr, sr


def _conv_kernel(cv_ref, cg_ref, dw_ref, db_ref, lg_ref, lb_ref, o_ref, hb_ref):
    i = pl.program_id(0)
    ts = o_ref.shape[0]
    halo = 32

    @pl.when(i == 0)
    def _():
        hb_ref[0:halo, :] = jnp.zeros((halo, CONV_CH), F32)

    @pl.when(i > 0)
    def _():
        hb_ref[0:halo, :] = hb_ref[ts:ts + halo, :]

    hb_ref[halo:halo + ts, :] = cv_ref[...] * _sigmoid(cg_ref[...])
    acc = jnp.zeros((ts, CONV_CH), F32) + db_ref[...]
    for w in range(CONV_WIDTH):
        acc = acc + hb_ref[pl.ds(halo - (CONV_WIDTH - 1) + w, ts), :] * dw_ref[w:w + 1, :]
    mu = jnp.mean(acc, axis=-1, keepdims=True)
    var = jnp.mean(jnp.square(acc - mu), axis=-1, keepdims=True)
    y = (acc - mu) * lax.rsqrt(var + NORM_EPS) * lg_ref[...] + lb_ref[...]
    o_ref[...] = (y * _sigmoid(y)).astype(o_ref.dtype)


def _conv_group(proj, dw_w, dw_b, ln_g, ln_b):
    S = proj.shape[0]
    ts = 256
    vec = pl.BlockSpec((1, CONV_CH), lambda i: (0, 0))
    return pl.pallas_call(
        _conv_kernel,
        out_shape=jax.ShapeDtypeStruct((S, CONV_CH), BF16),
        grid=(S // ts,),
        in_specs=[pl.BlockSpec((ts, CONV_CH), lambda i: (i, 0)),
                  pl.BlockSpec((ts, CONV_CH), lambda i: (i, 1)),
                  pl.BlockSpec((CONV_WIDTH, CONV_CH), lambda i: (0, 0)), vec, vec, vec],
        out_specs=pl.BlockSpec((ts, CONV_CH), lambda i: (i, 0)),
        scratch_shapes=[pltpu.VMEM((ts + 32, CONV_CH), F32)],
        compiler_params=_cparams(("arbitrary",)),
        name="conv_group",
    )(proj, proj, dw_w, dw_b.reshape(1, -1), ln_g.reshape(1, -1), ln_b.reshape(1, -1))


def _compress_kernel(x_ref, pe_ref, w1_ref, w2_ref, o_ref):
    X = x_ref[0]
    nrow = X.shape[0]
    half = CMP_STRIDE * HD
    A = jnp.dot(X, w1_ref[0:half, :], precision=HI, preferred_element_type=F32)
    B = jnp.dot(X, w1_ref[half:2 * half, :], precision=HI, preferred_element_type=F32)
    pe8 = jnp.broadcast_to(pe_ref[...], (8, 2 * half))
    cst = jnp.dot(pe8, w1_ref[...], precision=HI, preferred_element_type=F32)[0:1]
    pre = A + pltpu.roll(B, nrow - 1, 0) + cst
    act = pre * _sigmoid(pre)
    out = jnp.dot(act, w2_ref[...], precision=HI, preferred_element_type=F32)
    rows = lax.broadcasted_iota(jnp.int32, out.shape, 0)
    o_ref[0] = jnp.where(rows < nrow - 1, out, 0.0)


def _compress(strips, pe, w1, w2):
    nh, nrow, width = strips.shape
    return pl.pallas_call(
        _compress_kernel,
        out_shape=jax.ShapeDtypeStruct((nh, nrow, HD), F32),
        grid=(nh,),
        in_specs=[pl.BlockSpec((1, nrow, width), lambda h: (h, 0, 0)),
                  pl.BlockSpec((1, width * 2), lambda h: (0, 0)),
                  pl.BlockSpec((width * 2, HD), lambda h: (0, 0)),
                  pl.BlockSpec((HD, HD), lambda h: (0, 0))],
        out_specs=pl.BlockSpec((1, nrow, HD), lambda h: (h, 0, 0)),
        compiler_params=_cparams(("arbitrary",)),
        name="nsa_compress",
    )(strips, pe.reshape(1, -1), w1, w2)


def _lane_pick(vals, lane, idx):
    return jnp.sum(jnp.where(lane == idx, vals, 0.0), axis=-1, keepdims=True)


def _cmp_kernel(q_ref, gt_ref, kc_ref, vc_ref, cov_ref, oc_ref, sb_ref, *, n_sel):
    k = pl.program_id(0)
    i = pl.program_id(1)
    tq = q_ref.shape[0]
    ncp = kc_ref.shape[1]
    scale = HD ** -0.5
    tpos = i * tq + lax.broadcasted_iota(jnp.int32, (tq, 1), 0)
    ncol = lax.broadcasted_iota(jnp.int32, (1, ncp), 1)
    mc = (ncol * CMP_STRIDE + (CMP_BLOCK - 1)) <= tpos
    anyv = tpos >= (CMP_BLOCK - 1)
    kc = kc_ref[0]
    vc = vc_ref[0].astype(BF16)
    lane = lax.broadcasted_iota(jnp.int32, (tq, LANES), 1)
    sig = _sigmoid(gt_ref[...])
    psum = jnp.zeros((tq, ncp), F32)
    for g in range(NSA_GROUP):
        qg = q_ref[:, g * HD:(g + 1) * HD] * scale
        s = lax.dot_general(qg, kc, (((1,), (1,)), ((), ())), precision=HI, preferred_element_type=F32)
        s = jnp.where(mc, s, NEG)
        m = jnp.max(s, axis=-1, keepdims=True)
        e = jnp.where(mc, jnp.exp(s - m), 0.0)
        l = jnp.sum(e, axis=-1, keepdims=True)
        p = e / jnp.where(anyv, l, 1.0)
        o = jnp.dot(p.astype(BF16), vc, preferred_element_type=F32)
        gate = _lane_pick(sig, lane, k * NSA_GROUP + g)
        oc_ref[:, g * HD:(g + 1) * HD] = gate * o
        psum = psum + p
    imp = jnp.dot(psum, cov_ref[...], precision=HI, preferred_element_type=F32)
    valid = (lane * SEL_BLOCK <= tpos) & (lane < n_sel)
    forced = (lane == 0) | (lane == tpos // SEL_BLOCK)
    work = jnp.where(forced, BIG, jnp.where(valid, imp, NEG))
    lane_f = lane.astype(F32)
    chosen = jnp.zeros((tq, LANES), F32)
    for _ in range(min(SEL_TOPN, n_sel)):
        m = jnp.max(work, axis=-1, keepdims=True)
        first = jnp.min(jnp.where(work == m, lane_f, float(LANES)), axis=-1, keepdims=True)
        hit = lane_f == first
        chosen = jnp.where(hit, 1.0, chosen)
        work = jnp.where(hit, -jnp.inf, work)
    keep = (chosen > 0.5) & valid
    sb_ref[0] = jnp.where(keep, 0.0, NEG).astype(BF16)


def _cmp_attention(proj, kc, vc, cover):
    S = proj.shape[0]
    tq = 256
    ncp = kc.shape[1]
    return pl.pallas_call(
        functools.partial(_cmp_kernel, n_sel=S // SEL_BLOCK),
        out_shape=(jax.ShapeDtypeStruct((S, NSA_HEADS * HD), F32),
                   jax.ShapeDtypeStruct((NSA_KV_HEADS, S, LANES), BF16)),
        grid=(NSA_KV_HEADS, S // tq),
        in_specs=[pl.BlockSpec((tq, NSA_GROUP * HD), lambda k, i: (i, CB_Q // NSA_GROUP + k)),
                  pl.BlockSpec((tq, LANES), lambda k, i: (i, CB_GT)),
                  pl.BlockSpec((1, ncp, HD), lambda k, i: (k, 0, 0)),
                  pl.BlockSpec((1, ncp, HD), lambda k, i: (k, 0, 0)),
                  pl.BlockSpec((ncp, LANES), lambda k, i: (0, 0))],
        out_specs=(pl.BlockSpec((tq, NSA_GROUP * HD), lambda k, i: (i, k)),
                   pl.BlockSpec((1, tq, LANES), lambda k, i: (k, i, 0))),
        compiler_params=_cparams(("arbitrary", "arbitrary")),
        name="nsa_cmp_select",
    )(proj, proj, kc, vc, cover)


def _kvprep_kernel(ks_ref, vs_ref, kw_ref, vw_ref, ksa_ref, vsb_ref, kwb_ref, vwb_ref):
    i = pl.program_id(1)
    ts = ks_ref.shape[0]
    rows = i * ts + lax.broadcasted_iota(jnp.int32, (ts, LANES), 0)
    lane = lax.broadcasted_iota(jnp.int32, (ts, LANES), 1)
    ksa_ref[0, :, 0:HD] = ks_ref[...].astype(BF16)
    ksa_ref[0, :, HD:2 * HD] = jnp.where(lane == rows // SEL_BLOCK, 1.0, 0.0).astype(BF16)
    vsb_ref[0] = vs_ref[...].astype(BF16)
    kwb_ref[0] = kw_ref[...].astype(BF16)
    vwb_ref[0] = vw_ref[...].astype(BF16)


def _kv_prep(proj):
    S = proj.shape[0]
    ts = 512

    def col(cb):
        return pl.BlockSpec((ts, HD), lambda k, i: (i, cb + k))

    out = pl.BlockSpec((1, ts, HD), lambda k, i: (k, i, 0))
    return pl.pallas_call(
        _kvprep_kernel,
        out_shape=(jax.ShapeDtypeStruct((NSA_KV_HEADS, S, 2 * HD), BF16),
                   jax.ShapeDtypeStruct((NSA_KV_HEADS, S, HD), BF16),
                   jax.ShapeDtypeStruct((NSA_KV_HEADS, S, HD), BF16),
                   jax.ShapeDtypeStruct((NSA_KV_HEADS, S, HD), BF16)),
        grid=(NSA_KV_HEADS, S // ts),
        in_specs=[col(CB_KS), col(CB_VS), col(CB_KW), col(CB_VW)],
        out_specs=(pl.BlockSpec((1, ts, 2 * HD), lambda k, i: (k, i, 0)), out, out, out),
        compiler_params=_cparams(("arbitrary", "arbitrary")),
        name="nsa_kv_prep",
    )(proj, proj, proj, proj)


def _online_step(s, v, m, l, acc):
    m_new = jnp.maximum(m, jnp.max(s, axis=-1, keepdims=True))
    alpha = jnp.exp(m - m_new)
    p = jnp.exp(s - m_new)
    l = alpha * l + jnp.sum(p, axis=-1, keepdims=True)
    acc = alpha * acc + jnp.dot(p.astype(BF16), v, preferred_element_type=F32)
    return m_new, l, acc


def _nsa_kernel(q_ref, sb_ref, oc_ref, gt_ref, ks_ref, vs_ref, kw_ref, vw_ref, y_ref):
    k = pl.program_id(0)
    i = pl.program_id(1)
    tq = q_ref.shape[0]
    rows = NSA_GROUP * tq
    tks = 512
    scale = HD ** -0.5
    q0 = i * tq
    nt = (((1,), (1,)), ((), ()))
    qs = jnp.concatenate([q_ref[:, g * HD:(g + 1) * HD] for g in range(NSA_GROUP)], axis=0)
    qs = (qs * scale).astype(BF16)
    sb = sb_ref[0]
    qaug = jnp.concatenate([qs, jnp.concatenate([sb] * NSA_GROUP, axis=0)], axis=1)
    tpos = q0 + lax.rem(lax.broadcasted_iota(jnp.int32, (rows, 1), 0), tq)
    init = (jnp.full((rows, 1), NEG, F32), jnp.zeros((rows, 1), F32), jnp.zeros((rows, HD), F32))

    def sel_body(j, carry):
        k0 = pl.multiple_of(j * tks, tks)
        s = lax.dot_general(qaug, ks_ref[0, pl.ds(k0, tks), :], nt, preferred_element_type=F32)
        kpos = k0 + lax.broadcasted_iota(jnp.int32, (1, tks), 1)
        s = jnp.where(kpos <= tpos, s, NEG)
        return _online_step(s, vs_ref[0, pl.ds(k0, tks), :], *carry)

    _, l_s, acc_s = lax.fori_loop(0, (q0 + tq - 1) // tks + 1, sel_body, init)

    def win_tile(t, carry):
        k0 = pl.multiple_of(t * tq, tq)
        s = lax.dot_general(qs, kw_ref[0, pl.ds(k0, tq), :], nt, preferred_element_type=F32)
        kpos = k0 + lax.broadcasted_iota(jnp.int32, (1, tq), 1)
        s = jnp.where((kpos <= tpos) & (kpos > tpos - WINDOW), s, NEG)
        return _online_step(s, vw_ref[0, pl.ds(k0, tq), :], *carry)

    carry = win_tile(i, init)
    _, l_w, acc_w = lax.fori_loop(jnp.maximum(i - WINDOW // tq, 0), i, win_tile, carry)

    o_s = acc_s / l_s
    o_w = acc_w / l_w
    lane = lax.broadcasted_iota(jnp.int32, (tq, LANES), 1)
    sig = _sigmoid(gt_ref[...])
    for g in range(NSA_GROUP):
        head = k * NSA_GROUP + g
        g_s = _lane_pick(sig, lane, NSA_HEADS + head)
        g_w = _lane_pick(sig, lane, 2 * NSA_HEADS + head)
        y = (oc_ref[:, g * HD:(g + 1) * HD] + g_s * o_s[g * tq:(g + 1) * tq]
             + g_w * o_w[g * tq:(g + 1) * tq])
        y_ref[:, g * HD:(g + 1) * HD] = y.astype(y_ref.dtype)


def _nsa_attention(proj, selbias, oc, ksa, vsb, kwb, vwb):
    S = proj.shape[0]
    tq = 128

    def full(w):
        return pl.BlockSpec((1, S, w), lambda k, i: (k, 0, 0))

    return pl.pallas_call(
        _nsa_kernel,
        out_shape=jax.ShapeDtypeStruct((S, NSA_HEADS * HD), BF16),
        grid=(NSA_KV_HEADS, S // tq),
        in_specs=[pl.BlockSpec((tq, NSA_GROUP * HD), lambda k, i: (i, CB_Q // NSA_GROUP + k)),
                  pl.BlockSpec((1, tq, LANES), lambda k, i: (k, i, 0)),
                  pl.BlockSpec((tq, NSA_GROUP * HD), lambda k, i: (i, k)),
                  pl.BlockSpec((tq, LANES), lambda k, i: (i, CB_GT)),
                  full(2 * HD), full(HD), full(HD), full(HD)],
        out_specs=pl.BlockSpec((tq, NSA_GROUP * HD), lambda k, i: (i, k)),
        compiler_params=_cparams(("arbitrary", "arbitrary")),
        name="nsa_sel_win",
    )(proj, selbias, oc, proj, ksa, vsb, kwb, vwb)


def _ret_kernel(q_ref, k_ref, v_ref, g_ref, dm_ref, ze_ref, xi_ref, dc_ref, gg_ref, gb_ref, o_ref, st_ref):
    n = pl.program_id(1)
    C = RET_CHUNK
    nt = (((1,), (1,)), ((), ()))
    tn = (((0,), (0,)), ((), ()))

    @pl.when(n == 0)
    def _():
        st_ref[...] = jnp.zeros_like(st_ref)

    dmat = dm_ref[0]
    zeta = ze_ref[0]
    xi = xi_ref[0]
    decay = dc_ref[0]
    for c in range(q_ref.shape[0] // C):
        sl = slice(c * C, (c + 1) * C)
        q = q_ref[sl, :]
        kk = k_ref[sl, :] * (HD ** -0.5)
        v = v_ref[sl, :]
        qb, kb, vb = q.astype(BF16), kk.astype(BF16), v.astype(BF16)
        inner = lax.dot_general(qb, kb, nt, preferred_element_type=F32) * dmat
        o = jnp.dot(inner.astype(BF16), vb, preferred_element_type=F32)
        state = st_ref[...]
        o = o + jnp.dot(qb, state.astype(BF16), preferred_element_type=F32) * xi
        kv = lax.dot_general((kk * zeta).astype(BF16), vb, tn, preferred_element_type=F32)
        st_ref[...] = decay * state + kv
        mu = jnp.mean(o, axis=-1, keepdims=True)
        var = jnp.mean(jnp.square(o - mu), axis=-1, keepdims=True)
        y = (o - mu) * lax.rsqrt(var + NORM_EPS) * gg_ref[...] + gb_ref[...]
        gt = g_ref[sl, :]
        o_ref[sl, :] = (y * (gt * _sigmoid(gt))).astype(o_ref.dtype)


def _retention(proj, gn_g, gn_b):
    S = proj.shape[0]
    C = RET_CHUNK
    ts = 1024 if S % 1024 == 0 else C
    H = RET_HEADS
    log_g = jnp.log(1.0 - 2.0 ** (-5.0 - jnp.arange(H, dtype=F32)))
    i = jnp.arange(C, dtype=F32)
    diff = i[:, None] - i[None, :]
    dmat = jnp.where(diff >= 0, jnp.exp(log_g[:, None, None] * jnp.maximum(diff, 0.0)), 0.0)
    zeta = jnp.exp(log_g[:, None] * (C - 1.0 - i))[:, :, None]
    xi = jnp.exp(log_g[:, None] * (i + 1.0))[:, :, None]
    decay = jnp.broadcast_to(jnp.exp(log_g * C)[:, None, None], (H, 1, HD))

    def col(cb):
        return pl.BlockSpec((ts, HD), lambda h, n: (n, cb + h))

    return pl.pallas_call(
        _ret_kernel,
        out_shape=jax.ShapeDtypeStruct((S, H * HD), BF16),
        grid=(H, S // ts),
        in_specs=[col(CB_RQ), col(CB_RK), col(CB_RV), col(CB_RG),
                  pl.BlockSpec((1, C, C), lambda h, n: (h, 0, 0)),
                  pl.BlockSpec((1, C, 1), lambda h, n: (h, 0, 0)),
                  pl.BlockSpec((1, C, 1), lambda h, n: (h, 0, 0)),
                  pl.BlockSpec((1, 1, HD), lambda h, n: (h, 0, 0)),
                  pl.BlockSpec((1, HD), lambda h, n: (0, h)),
                  pl.BlockSpec((1, HD), lambda h, n: (0, h))],
        out_specs=pl.BlockSpec((ts, HD), lambda h, n: (n, h)),
        scratch_shapes=[pltpu.VMEM((HD, HD), F32)],
        compiler_params=_cparams(("arbitrary", "arbitrary")),
        name="retention",
    )(proj, proj, proj, proj, dmat, zeta, xi, decay, gn_g.reshape(1, -1), gn_b.reshape(1, -1))


def _outproj_kernel(x_ref, yc_ref, yn_ref, yr_ref, wc_ref, wn_ref, wr_ref, ga_ref, o_ref):
    y = jnp.dot(yc_ref[...], wc_ref[...], preferred_element_type=F32)
    y = y + jnp.dot(yn_ref[...], wn_ref[...], preferred_element_type=F32)
    y = y + jnp.dot(yr_ref[...], wr_ref[...], preferred_element_type=F32)
    o_ref[...] = x_ref[...] + ga_ref[...] * y


def _out_proj(x2, y_conv, y_nsa, y_ret, w_out_bf, gate_a):
    S = x2.shape[0]
    tm = 512
    wc, wn, wr = w_out_bf[:CONV_CH], w_out_bf[CONV_CH:CONV_CH + NSA_HEADS * HD], w_out_bf[CONV_CH + NSA_HEADS * HD:]

    def rows(w):
        return pl.BlockSpec((tm, w), lambda i: (i, 0))

    def whole(a):
        return pl.BlockSpec(a.shape, lambda i: (0, 0))

    return pl.pallas_call(
        _outproj_kernel,
        out_shape=jax.ShapeDtypeStruct((S, D_MODEL), F32),
        grid=(S // tm,),
        in_specs=[rows(D_MODEL), rows(y_conv.shape[1]), rows(y_nsa.shape[1]), rows(y_ret.shape[1]),
                  whole(wc), whole(wn), whole(wr), pl.BlockSpec((1, D_MODEL), lambda i: (0, 0))],
        out_specs=rows(D_MODEL),
        compiler_params=_cparams(("arbitrary",)),
        name="out_proj",
    )(x2, y_conv, y_nsa, y_ret, wc, wn, wr, gate_a)


def _router_kernel(x_ref, g_ref, sc_ref, sh_ref, rw_ref, rb_ref, h_ref, rt_ref, cnt_ref, carry_ref):
    i = pl.program_id(0)
    tm = x_ref.shape[0]

    @pl.when(i == 0)
    def _():
        carry_ref[...] = jnp.zeros_like(carry_ref)

    h = _modulated_norm(x_ref[...], g_ref[...], sc_ref[...], sh_ref[...])
    h_ref[...] = h
    logits = jnp.dot(h, rw_ref[...], precision=HI, preferred_element_type=F32) + rb_ref[...]
    lane = lax.broadcasted_iota(jnp.int32, (tm, LANES), 1)
    lane_f = lane.astype(F32)
    work = jnp.where(lane < N_EXPERTS, logits, -jnp.inf)
    onehot = jnp.zeros((tm, LANES), F32)
    vals, idxs = [], []
    for _ in range(TOP_K):
        m = jnp.max(work, axis=-1, keepdims=True)
        first = jnp.min(jnp.where(work == m, lane_f, float(LANES)), axis=-1, keepdims=True)
        hit = lane_f == first
        onehot = jnp.where(hit, 1.0, onehot)
        work = jnp.where(hit, -jnp.inf, work)
        vals.append(m)
        idxs.append(first)
    ex = [jnp.exp(v - vals[0]) for v in vals]
    den = ex[0] + ex[1] + ex[2] + ex[3]
    r = lax.broadcasted_iota(jnp.int32, (tm, tm), 0)
    c = lax.broadcasted_iota(jnp.int32, (tm, tm), 1)
    tri = jnp.where(c < r, 1.0, 0.0).astype(BF16)
    cum = jnp.dot(tri, onehot.astype(BF16), preferred_element_type=F32) + carry_ref[...]
    out = jnp.zeros((tm, LANES), F32)
    for kk in range(TOP_K):
        rank = jnp.sum(jnp.where(lane_f == idxs[kk], cum, 0.0), axis=-1, keepdims=True)
        out = jnp.where(lane == kk, idxs[kk], out)
        out = jnp.where(lane == TOP_K + kk, ex[kk] / den, out)
        out = jnp.where(lane == 2 * TOP_K + kk, rank, out)
    rt_ref[...] = out
    carry_ref[...] = carry_ref[...] + jnp.sum(onehot, axis=0, keepdims=True)
    cnt_ref[...] = carry_ref[...]


def _router(x2, g, scale, shift, router_w, router_b):
    T = x2.shape[0]
    tm = 256
    rw = jnp.pad(router_w, ((0, 0), (0, LANES - N_EXPERTS)))
    rb = jnp.pad(router_b, (0, LANES - N_EXPERTS)).reshape(1, LANES)
    row = pl.BlockSpec((1, D_MODEL), lambda i: (0, 0))
    return pl.pallas_call(
        _router_kernel,
        out_shape=(jax.ShapeDtypeStruct((T, D_MODEL), F32),
                   jax.ShapeDtypeStruct((T, LANES), F32),
                   jax.ShapeDtypeStruct((1, LANES), F32)),
        grid=(T // tm,),
        in_specs=[pl.BlockSpec((tm, D_MODEL), lambda i: (i, 0)), row, row, row,
                  pl.BlockSpec((D_MODEL, LANES), lambda i: (0, 0)),
                  pl.BlockSpec((1, LANES), lambda i: (0, 0))],
        out_specs=(pl.BlockSpec((tm, D_MODEL), lambda i: (i, 0)),
                   pl.BlockSpec((tm, LANES), lambda i: (i, 0)),
                   pl.BlockSpec((1, LANES), lambda i: (0, 0))),
        scratch_shapes=[pltpu.VMEM((1, LANES), F32)],
        compiler_params=_cparams(("arbitrary",)),
        name="moe_router",
    )(x2, g, scale, shift, rw, rb)


def _dispatch_kernel(tok_ref, nu_ref, h_hbm, o_ref, buf, sem):
    b = pl.program_id(0)
    nrow = o_ref.shape[0]

    @pl.when(b < nu_ref[0])
    def _():
        def issue(r, carry):
            t = tok_ref[b * nrow + r]
            pltpu.make_async_copy(h_hbm.at[pl.ds(t, 1)], buf.at[pl.ds(r, 1)], sem).start()
            return carry

        lax.fori_loop(0, nrow, issue, 0)

        def drain(r, carry):
            pltpu.make_async_copy(h_hbm.at[pl.ds(0, 1)], buf.at[pl.ds(r, 1)], sem).wait()
            return carry

        lax.fori_loop(0, nrow, drain, 0)
        o_ref[...] = buf[...].astype(o_ref.dtype)

    @pl.when(b >= nu_ref[0])
    def _():
        o_ref[...] = jnp.zeros_like(o_ref)


def _dispatch(h2, slot_tok, n_used):
    n_slots = slot_tok.shape[0]
    return pl.pallas_call(
        _dispatch_kernel,
        out_shape=jax.ShapeDtypeStruct((n_slots, D_MODEL), BF16),
        grid_spec=pltpu.PrefetchScalarGridSpec(
            num_scalar_prefetch=2,
            grid=(n_slots // MOE_SUB,),
            in_specs=[pl.BlockSpec(memory_space=pl.ANY)],
            out_specs=pl.BlockSpec((MOE_SUB, D_MODEL), lambda b, tok, nu: (b, 0)),
            scratch_shapes=[pltpu.VMEM((MOE_SUB, D_MODEL), F32), pltpu.SemaphoreType.DMA(())]),
        compiler_params=_cparams(("arbitrary",)),
        name="moe_dispatch",
    )(slot_tok, n_used, h2)


def _expert_kernel(ie_ref, ir_ref, ins_ref, xs_hbm, wgl_ref, wli_ref, bgl_ref, bli_ref, wdn_ref, bdn_ref,
                   ys_hbm, xbuf, acc, wg_s, wl_s, wd_s, sem_in, sem_out):
    w = pl.program_id(0)
    ct = pl.program_id(1)
    n_ct = pl.num_programs(1)
    nsub = ins_ref[w]
    row0 = ir_ref[w]

    def x_copy(sb):
        return pltpu.make_async_copy(xs_hbm.at[pl.ds(pl.multiple_of(row0 + sb * MOE_SUB, MOE_SUB), MOE_SUB)],
                                     xbuf.at[pl.ds(sb * MOE_SUB, MOE_SUB)], sem_in.at[sb])

    def y_copy(sb):
        return pltpu.make_async_copy(acc.at[pl.ds(sb * MOE_SUB, MOE_SUB)],
                                     ys_hbm.at[pl.ds(pl.multiple_of(row0 + sb * MOE_SUB, MOE_SUB), MOE_SUB)], sem_out.at[sb])

    @pl.when(nsub > 0)
    def _():
        @pl.when(ct == 0)
        def _():
            for sb in range(MOE_SUBMAX):
                @pl.when(sb < nsub)
                def _():
                    x_copy(sb).start()
            for sb in range(MOE_SUBMAX):
                @pl.when(sb < nsub)
                def _():
                    x_copy(sb).wait()

        wg_s[...] = wgl_ref[0].astype(BF16)
        wl_s[...] = wli_ref[0].astype(BF16)
        wd_s[...] = wdn_ref[0].astype(BF16)

        def sub_body(sb, carry):
            r0 = pl.multiple_of(sb * MOE_SUB, MOE_SUB)
            x = xbuf[pl.ds(r0, MOE_SUB), :]
            glu = jnp.dot(x, wg_s[...], preferred_element_type=F32) + bgl_ref[0]
            lin = jnp.dot(x, wl_s[...], preferred_element_type=F32) + bli_ref[0]
            glu = jnp.minimum(glu, SWIGLU_LIMIT)
            lin = jnp.clip(lin, -SWIGLU_LIMIT, SWIGLU_LIMIT)
            act = glu * _sigmoid(SWIGLU_ALPHA * glu) * (lin + 1.0)
            y = jnp.dot(act.astype(BF16), wd_s[...], preferred_element_type=F32)

            @pl.when(ct == 0)
            def _():
                acc[pl.ds(r0, MOE_SUB), :] = y + bdn_ref[0]

            @pl.when(ct > 0)
            def _():
                acc[pl.ds(r0, MOE_SUB), :] += y

            return carry

        lax.fori_loop(0, nsub, sub_body, 0)

        @pl.when(ct == n_ct - 1)
        def _():
            for sb in range(MOE_SUBMAX):
                @pl.when(sb < nsub)
                def _():
                    y_copy(sb).start()
            for sb in range(MOE_SUBMAX):
                @pl.when(sb < nsub)
                def _():
                    y_copy(sb).wait()


def _experts(xs, item_e, item_row0, item_nsub, w_gu, b_gu, w_dn, b_dn):
    n_slots = xs.shape[0]
    W = item_e.shape[0]
    te = MOE_TE
    n_ct = D_EXPERT // te
    rows = MOE_SUBMAX * MOE_SUB

    def ct_eff(w, ct, ins):
        return jnp.where(ins[w] > 0, ct, n_ct - 1)

    return pl.pallas_call(
        _expert_kernel,
        out_shape=jax.ShapeDtypeStruct((n_slots, D_MODEL), F32),
        grid_spec=pltpu.PrefetchScalarGridSpec(
            num_scalar_prefetch=3,
            grid=(W, n_ct),
            in_specs=[pl.BlockSpec(memory_space=pl.ANY),
                      pl.BlockSpec((1, D_MODEL, te), lambda w, ct, ie, ir, ins: (ie[w], 0, ct_eff(w, ct, ins))),
                      pl.BlockSpec((1, D_MODEL, te), lambda w, ct, ie, ir, ins: (ie[w], 0, n_ct + ct_eff(w, ct, ins))),
                      pl.BlockSpec((1, 1, te), lambda w, ct, ie, ir, ins: (ie[w], 0, ct_eff(w, ct, ins))),
                      pl.BlockSpec((1, 1, te), lambda w, ct, ie, ir, ins: (ie[w], 0, n_ct + ct_eff(w, ct, ins))),
                      pl.BlockSpec((1, te, D_MODEL), lambda w, ct, ie, ir, ins: (ie[w], ct_eff(w, ct, ins), 0)),
                      pl.BlockSpec((1, 1, D_MODEL), lambda w, ct, ie, ir, ins: (ie[w], 0, 0))],
            out_specs=pl.BlockSpec(memory_space=pl.ANY),
            scratch_shapes=[pltpu.VMEM((rows, D_MODEL), BF16), pltpu.VMEM((rows, D_MODEL), F32),
                            pltpu.VMEM((D_MODEL, te), BF16), pltpu.VMEM((D_MODEL, te), BF16),
                            pltpu.VMEM((te, D_MODEL), BF16),
                            pltpu.SemaphoreType.DMA((MOE_SUBMAX,)), pltpu.SemaphoreType.DMA((MOE_SUBMAX,))]),
        compiler_params=_cparams(("arbitrary", "arbitrary")),
        name="moe_experts",
    )(item_e, item_row0, item_nsub, xs, w_gu, w_gu, b_gu.reshape(N_EXPERTS, 1, -1), b_gu.reshape(N_EXPERTS, 1, -1),
      w_dn, b_dn.reshape(N_EXPERTS, 1, -1))


def _combine_kernel(dest_ref, x_ref, rt_ref, gf_ref, fg_ref, ys_hbm, o_ref, buf, sem, *, final_norm):
    i = pl.program_id(0)
    tm = x_ref.shape[0]

    def issue(r, carry):
        for kk in range(TOP_K):
            d = dest_ref[(i * tm + r) * TOP_K + kk]
            pltpu.make_async_copy(ys_hbm.at[pl.ds(d, 1)], buf.at[kk, pl.ds(r, 1)], sem).start()
        return carry

    lax.fori_loop(0, tm, issue, 0)

    def drain(r, carry):
        for kk in range(TOP_K):
            pltpu.make_async_copy(ys_hbm.at[pl.ds(0, 1)], buf.at[kk, pl.ds(r, 1)], sem).wait()
        return carry

    lax.fori_loop(0, tm, drain, 0)
    rt = rt_ref[...]
    lane = lax.broadcasted_iota(jnp.int32, rt.shape, 1)
    moe = jnp.zeros((tm, D_MODEL), F32)
    for kk in range(TOP_K):
        moe = moe + buf[kk] * _lane_pick(rt, lane, TOP_K + kk)
    out = x_ref[...] + gf_ref[...] * moe
    if final_norm:
        ms = jnp.mean(out * out, axis=-1, keepdims=True)
        out = out * lax.rsqrt(ms + NORM_EPS) * fg_ref[...]
    o_ref[...] = out


def _combine(x2, route, gate_f, final_g, ys, dest_flat, final_norm):
    T = x2.shape[0]
    tm = 128
    row = pl.BlockSpec((1, D_MODEL), lambda i, d: (0, 0))
    return pl.pallas_call(
        functools.partial(_combine_kernel, final_norm=final_norm),
        out_shape=jax.ShapeDtypeStruct((T, D_MODEL), F32),
        grid_spec=pltpu.PrefetchScalarGridSpec(
            num_scalar_prefetch=1,
            grid=(T // tm,),
            in_specs=[pl.BlockSpec((tm, D_MODEL), lambda i, d: (i, 0)),
                      pl.BlockSpec((tm, LANES), lambda i, d: (i, 0)), row, row,
                      pl.BlockSpec(memory_space=pl.ANY)],
            out_specs=pl.BlockSpec((tm, D_MODEL), lambda i, d: (i, 0)),
            scratch_shapes=[pltpu.VMEM((TOP_K, tm, D_MODEL), F32), pltpu.SemaphoreType.DMA(())]),
        compiler_params=_cparams(("arbitrary",)),
        name="moe_combine",
    )(dest_flat, x2, route, gate_f, final_g, ys)


def _moe_plan(route, counts_f):
    T = route.shape[0]
    e = route[:, 0:TOP_K].astype(jnp.int32)
    rank = route[:, 2 * TOP_K:3 * TOP_K].astype(jnp.int32)
    counts = counts_f[0, :N_EXPERTS].astype(jnp.int32)
    nsub = (counts + MOE_SUB - 1) // MOE_SUB
    padded = nsub * MOE_SUB
    pend = jnp.cumsum(padded)
    pstart = pend - padded
    dest = pstart[e] + rank
    n_slots = (T * TOP_K + MOE_SUB - 1) // MOE_SUB * MOE_SUB + N_EXPERTS * MOE_SUB
    tok = jnp.repeat(jnp.arange(T, dtype=jnp.int32), TOP_K)
    slot_tok = jnp.zeros((n_slots,), jnp.int32).at[dest.reshape(-1)].set(tok)
    n_used = (pend[-1] // MOE_SUB).astype(jnp.int32).reshape(1)
    n_items_max = N_EXPERTS + (n_slots // MOE_SUB) // MOE_SUBMAX
    per_e = (nsub + MOE_SUBMAX - 1) // MOE_SUBMAX
    iend = jnp.cumsum(per_e)
    w = jnp.arange(n_items_max, dtype=jnp.int32)
    ew = jnp.minimum(jnp.searchsorted(iend, w, side='right'), N_EXPERTS - 1).astype(jnp.int32)
    local = w - (iend[ew] - per_e[ew])
    live = w < iend[-1]
    item_nsub = jnp.where(live, jnp.clip(nsub[ew] - local * MOE_SUBMAX, 0, MOE_SUBMAX), 0).astype(jnp.int32)
    item_row0 = jnp.where(live, pstart[ew] + local * (MOE_SUBMAX * MOE_SUB), 0).astype(jnp.int32)
    last_e = ew[jnp.maximum(iend[-1] - 1, 0)]
    item_e = jnp.where(live, ew, last_e).astype(jnp.int32)
    return dest.reshape(-1).astype(jnp.int32), slot_tok, n_used, item_e, item_row0, item_nsub


def _repack_w_in(w_in_l):
    gates = jnp.pad(w_in_l[:, GATE_OFF:RET_OFF], ((0, 0), (0, LANES - 3 * NSA_HEADS)))
    return jnp.concatenate([w_in_l[:, :GATE_OFF], gates, w_in_l[:, RET_OFF:]], axis=1).astype(BF16)


def _cover_matrix(S):
    n_strip = S // CMP_STRIDE
    n_cmp = (S - CMP_BLOCK) // CMP_STRIDE + 1
    n = np.arange(n_strip)[:, None]
    j = np.arange(LANES)[None, :]
    start, end = n * CMP_STRIDE, n * CMP_STRIDE + CMP_BLOCK - 1
    cov = (start <= j * SEL_BLOCK + SEL_BLOCK - 1) & (end >= j * SEL_BLOCK) & (n < n_cmp) & (j < S // SEL_BLOCK)
    return jnp.asarray(cov.astype(np.float32))


def _strips(proj, cb):
    S = proj.shape[0]
    t = proj[:, cb * LANES:(cb + NSA_KV_HEADS) * LANES].reshape(S, NSA_KV_HEADS, HD)
    return t.transpose(1, 0, 2).reshape(NSA_KV_HEADS, S // CMP_STRIDE, CMP_STRIDE * HD)


def kernel(x, c, ada_w, ada_b, norm_mix_g, w_in, conv_dw_w, conv_dw_b, conv_ln_g, conv_ln_b, nsa_pe_k, nsa_pe_v, nsa_cmp_k_w1, nsa_cmp_k_w2, nsa_cmp_v_w1, nsa_cmp_v_w2, ret_gn_g, ret_gn_b, w_out, norm_ffn_g, router_w, router_b, moe_w_gate_up, moe_b_gate_up, moe_w_down, moe_b_down, final_norm_g):
    B, S, _ = x.shape
    assert B == 1 and c.shape[0] == 1
    x2 = x.reshape(S, D_MODEL)
    mod = _ada_mod(c, ada_w, ada_b).reshape(DEPTH, 6, 1, D_MODEL)
    tabs = _rotary_tables(S)
    cover = _cover_matrix(S)
    final_g = final_norm_g.reshape(1, D_MODEL)
    for l in range(DEPTH):
        shift_a, scale_a, gate_a, shift_f, scale_f, gate_f = [mod[l, i] for i in range(6)]
        proj = _in_proj(x2, norm_mix_g[l].reshape(1, -1), scale_a, shift_a, _repack_w_in(w_in[l]), tabs)
        y_conv = _conv_group(proj, conv_dw_w[l], conv_dw_b[l], conv_ln_g[l], conv_ln_b[l])
        kc = _compress(_strips(proj, CB_KC), nsa_pe_k[l], nsa_cmp_k_w1[l], nsa_cmp_k_w2[l])
        vc = _compress(_strips(proj, CB_VC), nsa_pe_v[l], nsa_cmp_v_w1[l], nsa_cmp_v_w2[l])
        oc, selbias = _cmp_attention(proj, kc, vc, cover)
        ksa, vsb, kwb, vwb = _kv_prep(proj)
        y_nsa = _nsa_attention(proj, selbias, oc, ksa, vsb, kwb, vwb)
        y_ret = _retention(proj, ret_gn_g[l], ret_gn_b[l])
        x2 = _out_proj(x2, y_conv, y_nsa, y_ret, w_out[l].astype(BF16), gate_a)
        h2, route, counts = _router(x2, norm_ffn_g[l].reshape(1, -1), scale_f, shift_f, router_w[l], router_b[l])
        dest, slot_tok, n_used, item_e, item_row0, item_nsub = _moe_plan(route, counts)
        xs = _dispatch(h2, slot_tok, n_used)
        ys = _experts(xs, item_e, item_row0, item_nsub, moe_w_gate_up[l], moe_b_gate_up[l], moe_w_down[l], moe_b_down[l])
        x2 = _combine(x2, route, gate_f, final_g, ys, dest, final_norm=(l == DEPTH - 1))
    return x2.reshape(B, S, D_MODEL)
```

```python
import functools
import math

import numpy as np
import jax
import jax.numpy as jnp
from jax import lax
from jax.experimental import pallas as pl
from jax.experimental.pallas import tpu as pltpu

F32 = jnp.float32
BF16 = jnp.bfloat16
HI = lax.Precision.HIGHEST

D_MODEL = 2048
DEPTH = 2
CONV_CH = 512
CONV_WIDTH = 31
HD = 128
NSA_HEADS = 8
NSA_KV_HEADS = 2
NSA_GROUP = 4
CMP_BLOCK = 32
CMP_STRIDE = 16
SEL_BLOCK = 64
SEL_TOPN = 16
WINDOW = 512
ROPE_THETA = 500000.0
ROPE_DIM = 32
RET_HEADS = 4
RET_CHUNK = 128
RET_THETA = 10000.0
N_EXPERTS = 32
TOP_K = 4
D_EXPERT = 2048
SWIGLU_LIMIT = 7.0
SWIGLU_ALPHA = 1.702
NORM_EPS = 1e-6
NEG = -1e30
BIG = 1e30
LOG2E = math.log2(math.e)

LANES = 128
VMEM_LIMIT = 56 * 1024 * 1024

CB_Q, CB_KC, CB_KS, CB_KW = 0, 8, 10, 12
CB_RQ, CB_RK = 14, 18
CB_VC, CB_CV, CB_CG, CB_VS, CB_VW, CB_GT, CB_RV, CB_RG = 22, 24, 28, 32, 34, 36, 37, 41
PROJ_BLOCKS = 46
PROJ_W = PROJ_BLOCKS * LANES
PROJ_TN = 256
NSA_ROT_TILES = CB_RQ * LANES // PROJ_TN
RET_ROT_TILES = CB_VC * LANES // PROJ_TN
_SRC = {'cv': (0, 512), 'cg': (512, 512), 'q': (1024, 1024), 'kc': (2048, 256), 'vc': (2304, 256),
        'ks': (2560, 256), 'vs': (2816, 256), 'kw': (3072, 256), 'vw': (3328, 256), 'gt': (3584, 24),
        'rq': (3608, 512), 'rk': (4120, 512), 'rv': (4632, 512), 'rg': (5144, 512)}
_ORDER = ['q', 'kc', 'ks', 'kw', 'rq', 'rk', 'vc', 'cv', 'cg', 'vs', 'vw', 'gt', 'rv', 'rg']

MOE_SUB = 256
MOE_SUBMAX = 5
MOE_TE = 512


def _sigmoid(x):
    return 1.0 / (1.0 + jnp.exp(-x))


def _cparams(sem, vmem=VMEM_LIMIT):
    return pltpu.CompilerParams(dimension_semantics=sem, vmem_limit_bytes=vmem)


def _ada_kernel(c_ref, w_ref, b_ref, o_ref, sc_ref):
    @pl.when(pl.program_id(1) == 0)
    def _():
        cv = c_ref[...]
        sc_ref[...] = cv * _sigmoid(cv)

    tn = o_ref.shape[-1]

    def body(i, acc):
        k0 = pl.multiple_of(i * 64, 64)
        p = w_ref[0, pl.ds(k0, 64), :] * sc_ref[pl.ds(k0, 64), :]
        return acc + p.reshape(8, 8, tn).sum(axis=0)

    acc = lax.fori_loop(0, D_MODEL // 64, body, jnp.zeros((8, tn), F32))
    o_ref[0] = jnp.sum(acc, axis=0, keepdims=True) + b_ref[0]


def _ada_mod(c, ada_w, ada_b):
    tn = 1024
    n = 6 * D_MODEL
    return pl.pallas_call(
        _ada_kernel,
        out_shape=jax.ShapeDtypeStruct((DEPTH, 1, n), F32),
        grid=(DEPTH, n // tn),
        in_specs=[pl.BlockSpec((D_MODEL, 1), lambda l, j: (0, 0)),
                  pl.BlockSpec((1, D_MODEL, tn), lambda l, j: (l, 0, j)),
                  pl.BlockSpec((1, 1, tn), lambda l, j: (l, 0, j))],
        out_specs=pl.BlockSpec((1, 1, tn), lambda l, j: (l, 0, j)),
        scratch_shapes=[pltpu.VMEM((D_MODEL, 1), F32)],
        compiler_params=_cparams(("arbitrary", "arbitrary")),
        name="ada_mod",
    )(c.reshape(D_MODEL, 1), ada_w, ada_b.reshape(DEPTH, 1, n))


def _modulated_norm(x, g, scale, shift):
    ms = jnp.mean(x * x, axis=-1, keepdims=True)
    return x * lax.rsqrt(ms + NORM_EPS) * g * (1.0 + scale) + shift


def _inproj_kernel(x_ref, g_ref, sc_ref, sh_ref, w_ref, cn_ref, s1_ref, s2_ref, cr_ref, sr_ref,
                   o_ref, h_ref):
    j = pl.program_id(1)
    nsub = o_ref.shape[-1] // LANES

    @pl.when(j == 0)
    def _():
        h_ref[...] = _modulated_norm(x_ref[...], g_ref[...], sc_ref[...], sh_ref[...]).astype(BF16)

    def matmul():
        return jnp.dot(h_ref[...], w_ref[...], preferred_element_type=F32)

    @pl.when(j < NSA_ROT_TILES)
    def _():
        acc = matmul()
        for c in range(nsub):
            sub = acc[:, c * LANES:(c + 1) * LANES]
            o_ref[:, c * LANES:(c + 1) * LANES] = (
                sub * cn_ref[...] + pltpu.roll(sub, LANES - ROPE_DIM // 2, 1) * s1_ref[...]
                + pltpu.roll(sub, ROPE_DIM // 2, 1) * s2_ref[...])

    @pl.when((j >= NSA_ROT_TILES) & (j < RET_ROT_TILES))
    def _():
        acc = matmul()
        for c in range(nsub):
            sub = acc[:, c * LANES:(c + 1) * LANES]
            o_ref[:, c * LANES:(c + 1) * LANES] = sub * cr_ref[...] + pltpu.roll(sub, HD // 2, 1) * sr_ref[...]

    @pl.when(j >= RET_ROT_TILES)
    def _():
        o_ref[...] = matmul()


def _in_proj(x2, g, scale, shift, w_bf, tabs):
    S = x2.shape[0]
    tm, tn = 1024, PROJ_TN
    row = pl.BlockSpec((1, D_MODEL), lambda i, j: (0, 0))
    tab = pl.BlockSpec((tm, LANES), lambda i, j: (i, 0))
    return pl.pallas_call(
        _inproj_kernel,
        out_shape=jax.ShapeDtypeStruct((S, PROJ_W), F32),
        grid=(S // tm, PROJ_W // tn),
        in_specs=[pl.BlockSpec((tm, D_MODEL), lambda i, j: (i, 0)), row, row, row,
                  pl.BlockSpec((D_MODEL, tn), lambda i, j: (0, j)), tab, tab, tab, tab, tab],
        out_specs=pl.BlockSpec((tm, tn), lambda i, j: (i, j)),
        scratch_shapes=[pltpu.VMEM((tm, D_MODEL), BF16)],
        compiler_params=_cparams(("arbitrary", "arbitrary")),
        name="in_proj",
    )(x2, g, scale, shift, w_bf, *tabs)


def _rotary_tables(S):
    pos = jnp.arange(S, dtype=F32)[:, None]
    half = ROPE_DIM // 2
    inv = ROPE_THETA ** (-jnp.arange(half, dtype=F32) * 2.0 / ROPE_DIM)
    ang = pos * inv[None, :]
    cos, sin = jnp.cos(ang), jnp.sin(ang)
    ones = jnp.ones((S, LANES - ROPE_DIM), F32)
    zeros_r = jnp.zeros((S, LANES - ROPE_DIM), F32)
    zeros_h = jnp.zeros((S, half), F32)
    cn = jnp.concatenate([cos, cos, ones], axis=1)
    s1 = jnp.concatenate([-sin, zeros_h, zeros_r], axis=1)
    s2 = jnp.concatenate([zeros_h, sin, zeros_r], axis=1)
    halfr = HD // 2
    invr = RET_THETA ** (-jnp.arange(halfr, dtype=F32) * 2.0 / HD)
    angr = pos * invr[None, :]
    cosr, sinr = jnp.cos(angr), jnp.sin(angr)
    cr = jnp.concatenate([cosr, cosr], axis=1)
    sr = jnp.concatenate([-sinr, sinr], axis=1)
    return cn, s1, s2, cr, sr


def _conv_kernel(cv_ref, cg_ref, dw_ref, db_ref, lg_ref, lb_ref, o_ref, hb_ref):
    i = pl.program_id(0)
    ts = o_ref.shape[0]
    halo = 32

    @pl.when(i == 0)
    def _():
        hb_ref[0:halo, :] = jnp.zeros((halo, CONV_CH), F32)

    @pl.when(i > 0)
    def _():
        hb_ref[0:halo, :] = hb_ref[ts:ts + halo, :]

    hb_ref[halo:halo + ts, :] = cv_ref[...] * _sigmoid(cg_ref[...])
    acc = jnp.zeros((ts, CONV_CH), F32) + db_ref[...]
    for w in range(CONV_WIDTH):
        acc = acc + hb_ref[pl.ds(halo - (CONV_WIDTH - 1) + w, ts), :] * dw_ref[w:w + 1, :]
    mu = jnp.mean(acc, axis=-1, keepdims=True)
    var = jnp.mean(jnp.square(acc - mu), axis=-1, keepdims=True)
    y = (acc - mu) * lax.rsqrt(var + NORM_EPS) * lg_ref[...] + lb_ref[...]
    o_ref[...] = (y * _sigmoid(y)).astype(o_ref.dtype)


def _conv_group(proj, dw_w, dw_b, ln_g, ln_b):
    S = proj.shape[0]
    ts = 256
    vec = pl.BlockSpec((1, CONV_CH), lambda i: (0, 0))
    return pl.pallas_call(
        _conv_kernel,
        out_shape=jax.ShapeDtypeStruct((S, CONV_CH), BF16),
        grid=(S // ts,),
        in_specs=[pl.BlockSpec((ts, CONV_CH), lambda i: (i, CB_CV * LANES // CONV_CH)),
                  pl.BlockSpec((ts, CONV_CH), lambda i: (i, CB_CG * LANES // CONV_CH)),
                  pl.BlockSpec((CONV_WIDTH, CONV_CH), lambda i: (0, 0)), vec, vec, vec],
        out_specs=pl.BlockSpec((ts, CONV_CH), lambda i: (i, 0)),
        scratch_shapes=[pltpu.VMEM((ts + 32, CONV_CH), F32)],
        compiler_params=_cparams(("arbitrary",)),
        name="conv_group",
    )(proj, proj, dw_w, dw_b.reshape(1, -1), ln_g.reshape(1, -1), ln_b.reshape(1, -1))


def _compress_kernel(x_ref, pe_ref, w1_ref, w2_ref, o_ref):
    X = x_ref[0]
    nrow = X.shape[0]
    half = CMP_STRIDE * HD
    A = jnp.dot(X, w1_ref[0:half, :], precision=HI, preferred_element_type=F32)
    B = jnp.dot(X, w1_ref[half:2 * half, :], precision=HI, preferred_element_type=F32)
    pe8 = jnp.broadcast_to(pe_ref[...], (8, 2 * half))
    cst = jnp.dot(pe8, w1_ref[...], precision=HI, preferred_element_type=F32)[0:1]
    pre = A + pltpu.roll(B, nrow - 1, 0) + cst
    act = pre * _sigmoid(pre)
    out = jnp.dot(act, w2_ref[...], precision=HI, preferred_element_type=F32)
    rows = lax.broadcasted_iota(jnp.int32, out.shape, 0)
    o_ref[0] = jnp.where(rows < nrow - 1, out, 0.0)


def _compress(strips, pe, w1, w2):
    nh, nrow, width = strips.shape
    return pl.pallas_call(
        _compress_kernel,
        out_shape=jax.ShapeDtypeStruct((nh, nrow, HD), F32),
        grid=(nh,),
        in_specs=[pl.BlockSpec((1, nrow, width), lambda h: (h, 0, 0)),
                  pl.BlockSpec((1, width * 2), lambda h: (0, 0)),
                  pl.BlockSpec((width * 2, HD), lambda h: (0, 0)),
                  pl.BlockSpec((HD, HD), lambda h: (0, 0))],
        out_specs=pl.BlockSpec((1, nrow, HD), lambda h: (h, 0, 0)),
        compiler_params=_cparams(("arbitrary",)),
        name="nsa_compress",
    )(strips, pe.reshape(1, -1), w1, w2)


def _lane_pick(vals, lane, idx):
    return jnp.sum(jnp.where(lane == idx, vals, 0.0), axis=-1, keepdims=True)


def _cmp_kernel(q_ref, gt_ref, kc_ref, vc_ref, cov_ref, oc_ref, sb_ref, *, n_sel):
    k = pl.program_id(0)
    i = pl.program_id(1)
    tq = q_ref.shape[0]
    ncp = kc_ref.shape[1]
    scale = HD ** -0.5
    tpos = i * tq + lax.broadcasted_iota(jnp.int32, (tq, 1), 0)
    ncol = lax.broadcasted_iota(jnp.int32, (1, ncp), 1)
    mc = (ncol * CMP_STRIDE + (CMP_BLOCK - 1)) <= tpos
    anyv = tpos >= (CMP_BLOCK - 1)
    kc = kc_ref[0]
    vc = vc_ref[0].astype(BF16)
    lane = lax.broadcasted_iota(jnp.int32, (tq, LANES), 1)
    sig = _sigmoid(gt_ref[...])
    psum = jnp.zeros((tq, ncp), F32)
    for g in range(NSA_GROUP):
        qg = q_ref[:, g * HD:(g + 1) * HD] * scale
        s = lax.dot_general(qg, kc, (((1,), (1,)), ((), ())), precision=HI, preferred_element_type=F32)
        s = jnp.where(mc, s, NEG)
        m = jnp.max(s, axis=-1, keepdims=True)
        e = jnp.where(mc, jnp.exp(s - m), 0.0)
        l = jnp.sum(e, axis=-1, keepdims=True)
        p = e / jnp.where(anyv, l, 1.0)
        o = jnp.dot(p.astype(BF16), vc, preferred_element_type=F32)
        gate = _lane_pick(sig, lane, k * NSA_GROUP + g)
        oc_ref[:, g * HD:(g + 1) * HD] = gate * o
        psum = psum + p
    imp = jnp.dot(psum, cov_ref[...], precision=HI, preferred_element_type=F32)
    valid = (lane * SEL_BLOCK <= tpos) & (lane < n_sel)
    forced = (lane == 0) | (lane == tpos // SEL_BLOCK)
    work = jnp.where(forced, BIG, jnp.where(valid, imp, NEG))
    lane_f = lane.astype(F32)
    chosen = jnp.zeros((tq, LANES), F32)
    for _ in range(min(SEL_TOPN, n_sel)):
        m = jnp.max(work, axis=-1, keepdims=True)
        first = jnp.min(jnp.where(work == m, lane_f, float(LANES)), axis=-1, keepdims=True)
        hit = lane_f == first
        chosen = jnp.where(hit, 1.0, chosen)
        work = jnp.where(hit, -jnp.inf, work)
    keep = (chosen > 0.5) & valid
    sb_ref[0] = jnp.where(keep, 0.0, NEG).astype(BF16)


def _cmp_attention(proj, kc, vc, cover):
    S = proj.shape[0]
    tq = 256
    ncp = kc.shape[1]
    return pl.pallas_call(
        functools.partial(_cmp_kernel, n_sel=S // SEL_BLOCK),
        out_shape=(jax.ShapeDtypeStruct((S, NSA_HEADS * HD), F32),
                   jax.ShapeDtypeStruct((NSA_KV_HEADS, S, LANES), BF16)),
        grid=(NSA_KV_HEADS, S // tq),
        in_specs=[pl.BlockSpec((tq, NSA_GROUP * HD), lambda k, i: (i, CB_Q // NSA_GROUP + k)),
                  pl.BlockSpec((tq, LANES), lambda k, i: (i, CB_GT)),
                  pl.BlockSpec((1, ncp, HD), lambda k, i: (k, 0, 0)),
                  pl.BlockSpec((1, ncp, HD), lambda k, i: (k, 0, 0)),
                  pl.BlockSpec((ncp, LANES), lambda k, i: (0, 0))],
        out_specs=(pl.BlockSpec((tq, NSA_GROUP * HD), lambda k, i: (i, k)),
                   pl.BlockSpec((1, tq, LANES), lambda k, i: (k, i, 0))),
        compiler_params=_cparams(("arbitrary", "arbitrary")),
        name="nsa_cmp_select",
    )(proj, proj, kc, vc, cover)


def _kvprep_kernel(ks_ref, vs_ref, kw_ref, vw_ref, ksa_ref, vsb_ref, kwb_ref, vwb_ref):
    i = pl.program_id(1)
    ts = ks_ref.shape[0]
    rows = i * ts + lax.broadcasted_iota(jnp.int32, (ts, LANES), 0)
    lane = lax.broadcasted_iota(jnp.int32, (ts, LANES), 1)
    ksa_ref[0, :, 0:HD] = ks_ref[...].astype(BF16)
    ksa_ref[0, :, HD:2 * HD] = jnp.where(lane == rows // SEL_BLOCK, 1.0, 0.0).astype(BF16)
    vsb_ref[0, 0] = vs_ref[...].T.astype(BF16)
    kwb_ref[0] = kw_ref[...].astype(BF16)
    for t in range(ts // NSA_TQ):
        vwb_ref[0, t] = vw_ref[t * NSA_TQ:(t + 1) * NSA_TQ, :].T.astype(BF16)


def _kv_prep(proj):
    S = proj.shape[0]
    ts = NSA_TK
    nw = ts // NSA_TQ

    def col(cb):
        return pl.BlockSpec((ts, HD), lambda k, i: (i, cb + k))

    return pl.pallas_call(
        _kvprep_kernel,
        out_shape=(jax.ShapeDtypeStruct((NSA_KV_HEADS, S, 2 * HD), BF16),
                   jax.ShapeDtypeStruct((NSA_KV_HEADS, S // ts, HD, ts), BF16),
                   jax.ShapeDtypeStruct((NSA_KV_HEADS, S, HD), BF16),
                   jax.ShapeDtypeStruct((NSA_KV_HEADS, S // NSA_TQ, HD, NSA_TQ), BF16)),
        grid=(NSA_KV_HEADS, S // ts),
        in_specs=[col(CB_KS), col(CB_VS), col(CB_KW), col(CB_VW)],
        out_specs=(pl.BlockSpec((1, ts, 2 * HD), lambda k, i: (k, i, 0)),
                   pl.BlockSpec((1, 1, HD, ts), lambda k, i: (k, i, 0, 0)),
                   pl.BlockSpec((1, ts, HD), lambda k, i: (k, i, 0)),
                   pl.BlockSpec((1, nw, HD, NSA_TQ), lambda k, i: (k, i, 0, 0))),
        compiler_params=_cparams(("arbitrary", "arbitrary")),
        name="nsa_kv_prep",
    )(proj, proj, proj, proj)


NSA_TQ = 128
NSA_TK = 512
NSA_WK = WINDOW + NSA_TQ
NSA_CH = 32


def _nsa_kernel(q_ref, sb_ref, oc_ref, gt_ref, ks_ref, vst_ref, kw_ref, vwt_ref, y_ref, s_ref, p_ref, acc_ref):
    k = pl.program_id(0)
    i = pl.program_id(1)
    tq = NSA_TQ
    cols = NSA_GROUP * tq
    q0 = i * tq
    nt = (((1,), (1,)), ((), ()))
    qs = jnp.concatenate([q_ref[:, g * HD:(g + 1) * HD] for g in range(NSA_GROUP)], axis=0)
    qs = (qs * (HD ** -0.5 * LOG2E)).astype(BF16)
    sb = sb_ref[0]
    qaug = jnp.concatenate([qs, jnp.concatenate([sb] * NSA_GROUP, axis=0)], axis=1)
    qpos = q0 + lax.broadcasted_iota(jnp.int32, (1, LANES), 1)

    def softmax_tile(n_keys, k0, m_old, mask_fn, c_list):
        def scores(r0, c):
            s = s_ref[r0:r0 + NSA_CH, c * LANES:(c + 1) * LANES]
            if mask_fn is None:
                return s
            kpos = k0 + r0 + lax.broadcasted_iota(jnp.int32, (NSA_CH, 1), 0)
            return jnp.where(mask_fn(kpos), s, NEG)

        def fold(x):
            return x.reshape(NSA_CH // 8, 8, LANES)

        m_new, sums = [], []
        for c in c_list:
            mx8 = jnp.max(fold(scores(0, c)), axis=0)
            for r0 in range(NSA_CH, n_keys, NSA_CH):
                mx8 = jnp.maximum(mx8, jnp.max(fold(scores(r0, c)), axis=0))
            mx = jnp.max(mx8, axis=0, keepdims=True)
            mc = mx if m_old is None else jnp.maximum(m_old[:, c * LANES:(c + 1) * LANES], mx)
            tot8 = jnp.zeros((8, LANES), F32)
            for r0 in range(0, n_keys, NSA_CH):
                p = jnp.exp2(scores(r0, c) - mc)
                tot8 = tot8 + jnp.sum(fold(p), axis=0)
                p_ref[r0:r0 + NSA_CH, c * LANES:(c + 1) * LANES] = p.astype(BF16)
            m_new.append(mc)
            sums.append(jnp.sum(tot8, axis=0, keepdims=True))
        return m_new, sums

    acc_ref[...] = jnp.zeros((HD, cols), F32)

    def sel_tile(j, carry, causal):
        m_old, l_old = carry
        k0 = pl.multiple_of(j * NSA_TK, NSA_TK)
        kt = ks_ref[0, pl.ds(k0, NSA_TK), :]
        vt = vst_ref[0, j]
        mask_fn = (lambda kpos: kpos <= qpos) if causal else None
        halves = [slice(h * 2 * LANES, (h + 1) * 2 * LANES) for h in range(2)]
        for hs in halves:
            s_ref[0:NSA_TK, hs] = lax.dot_general(kt, qaug[hs], nt, preferred_element_type=F32)
        m_out, l_out = [], []
        for h, hs in enumerate(halves):
            m_new, sums = softmax_tile(NSA_TK, k0, m_old, mask_fn, [2 * h, 2 * h + 1])
            m_new, sums = jnp.concatenate(m_new, axis=1), jnp.concatenate(sums, axis=1)
            alpha = jnp.exp2(m_old[:, hs] - m_new)
            acc_ref[:, hs] = acc_ref[:, hs] * alpha + jnp.dot(vt, p_ref[0:NSA_TK, hs],
                                                              preferred_element_type=F32)
            m_out.append(m_new)
            l_out.append(alpha * l_old[:, hs] + sums)
        return jnp.concatenate(m_out, axis=1), jnp.concatenate(l_out, axis=1)

    n_full = (q0 + tq - 1) // NSA_TK
    init = (jnp.full((1, cols), NEG, F32), jnp.zeros((1, cols), F32))
    carry = lax.fori_loop(0, n_full, lambda j, c: sel_tile(j, c, False), init)
    _, l_s = sel_tile(n_full, carry, True)
    ot_s = acc_ref[...] / l_s

    w0 = pl.multiple_of(jnp.maximum(q0 - WINDOW, 0), tq)
    s_ref[...] = lax.dot_general(kw_ref[0, pl.ds(w0, NSA_WK), :], qs, nt, preferred_element_type=F32)
    _, l_w = softmax_tile(NSA_WK, w0, None, lambda kpos: (kpos <= qpos) & (kpos > qpos - WINDOW),
                          list(range(cols // LANES)))
    l_w = jnp.concatenate(l_w, axis=1)
    ot_w = jnp.zeros((HD, cols), F32)
    for t in range(NSA_WK // tq):
        ot_w = ot_w + jnp.dot(vwt_ref[0, w0 // tq + t], p_ref[t * tq:(t + 1) * tq, :],
                              preferred_element_type=F32)
    ot_w = ot_w / l_w

    lane = lax.broadcasted_iota(jnp.int32, (tq, LANES), 1)
    sig = _sigmoid(gt_ref[...])
    for g in range(NSA_GROUP):
        head = k * NSA_GROUP + g
        g_s = _lane_pick(sig, lane, NSA_HEADS + head)
        g_w = _lane_pick(sig, lane, 2 * NSA_HEADS + head)
        y = (oc_ref[:, g * HD:(g + 1) * HD] + g_s * ot_s[:, g * tq:(g + 1) * tq].T
             + g_w * ot_w[:, g * tq:(g + 1) * tq].T)
        y_ref[:, g * HD:(g + 1) * HD] = y.astype(y_ref.dtype)


def _nsa_attention(proj, selbias, oc, ksa, vst, kwb, vwt):
    S = proj.shape[0]
    tq = NSA_TQ
    cols = NSA_GROUP * tq
    assert S >= NSA_WK and tq == LANES

    def full(a):
        return pl.BlockSpec((1,) + a.shape[1:], lambda k, i: (k,) + (0,) * (a.ndim - 1))

    return pl.pallas_call(
        _nsa_kernel,
        out_shape=jax.ShapeDtypeStruct((S, NSA_HEADS * HD), BF16),
        grid=(NSA_KV_HEADS, S // tq),
        in_specs=[pl.BlockSpec((tq, NSA_GROUP * HD), lambda k, i: (i, CB_Q // NSA_GROUP + k)),
                  pl.BlockSpec((1, tq, LANES), lambda k, i: (k, i, 0)),
                  pl.BlockSpec((tq, NSA_GROUP * HD), lambda k, i: (i, k)),
                  pl.BlockSpec((tq, LANES), lambda k, i: (i, CB_GT)),
                  full(ksa), full(vst), full(kwb), full(vwt)],
        out_specs=pl.BlockSpec((tq, NSA_GROUP * HD), lambda k, i: (i, k)),
        scratch_shapes=[pltpu.VMEM((NSA_WK, cols), F32), pltpu.VMEM((NSA_WK, cols), BF16),
                        pltpu.VMEM((HD, cols), F32)],
        compiler_params=_cparams(("arbitrary", "arbitrary")),
        name="nsa_sel_win",
    )(proj, selbias, oc, proj, ksa, vst, kwb, vwt)


def _ret_kernel(q_ref, k_ref, v_ref, g_ref, dm_ref, ze_ref, xi_ref, dc_ref, gg_ref, gb_ref, o_ref, st_ref):
    n = pl.program_id(1)
    C = RET_CHUNK
    nt = (((1,), (1,)), ((), ()))
    tn = (((0,), (0,)), ((), ()))

    @pl.when(n == 0)
    def _():
        st_ref[...] = jnp.zeros_like(st_ref)

    dmat = dm_ref[0]
    zeta = ze_ref[0]
    xi = xi_ref[0]
    decay = dc_ref[0]
    for c in range(q_ref.shape[0] // C):
        sl = slice(c * C, (c + 1) * C)
        q = q_ref[sl, :]
        kk = k_ref[sl, :] * (HD ** -0.5)
        v = v_ref[sl, :]
        qb, kb, vb = q.astype(BF16), kk.astype(BF16), v.astype(BF16)
        inner = lax.dot_general(qb, kb, nt, preferred_element_type=F32) * dmat
        o = jnp.dot(inner.astype(BF16), vb, preferred_element_type=F32)
        state = st_ref[...]
        o = o + jnp.dot(qb, state.astype(BF16), preferred_element_type=F32) * xi
        kv = lax.dot_general((kk * zeta).astype(BF16), vb, tn, preferred_element_type=F32)
        st_ref[...] = decay * state + kv
        mu = jnp.mean(o, axis=-1, keepdims=True)
        var = jnp.mean(jnp.square(o - mu), axis=-1, keepdims=True)
        y = (o - mu) * lax.rsqrt(var + NORM_EPS) * gg_ref[...] + gb_ref[...]
        gt = g_ref[sl, :]
        o_ref[sl, :] = (y * (gt * _sigmoid(gt))).astype(o_ref.dtype)


def _retention(proj, gn_g, gn_b):
    S = proj.shape[0]
    C = RET_CHUNK
    ts = 1024 if S % 1024 == 0 else C
    H = RET_HEADS
    log_g = jnp.log(1.0 - 2.0 ** (-5.0 - jnp.arange(H, dtype=F32)))
    i = jnp.arange(C, dtype=F32)
    diff = i[:, None] - i[None, :]
    dmat = jnp.where(diff >= 0, jnp.exp(log_g[:, None, None] * jnp.maximum(diff, 0.0)), 0.0)
    zeta = jnp.exp(log_g[:, None] * (C - 1.0 - i))[:, :, None]
    xi = jnp.exp(log_g[:, None] * (i + 1.0))[:, :, None]
    decay = jnp.broadcast_to(jnp.exp(log_g * C)[:, None, None], (H, 1, HD))

    def col(cb):
        return pl.BlockSpec((ts, HD), lambda h, n: (n, cb + h))

    return pl.pallas_call(
        _ret_kernel,
        out_shape=jax.ShapeDtypeStruct((S, H * HD), BF16),
        grid=(H, S // ts),
        in_specs=[col(CB_RQ), col(CB_RK), col(CB_RV), col(CB_RG),
                  pl.BlockSpec((1, C, C), lambda h, n: (h, 0, 0)),
                  pl.BlockSpec((1, C, 1), lambda h, n: (h, 0, 0)),
                  pl.BlockSpec((1, C, 1), lambda h, n: (h, 0, 0)),
                  pl.BlockSpec((1, 1, HD), lambda h, n: (h, 0, 0)),
                  pl.BlockSpec((1, HD), lambda h, n: (0, h)),
                  pl.BlockSpec((1, HD), lambda h, n: (0, h))],
        out_specs=pl.BlockSpec((ts, HD), lambda h, n: (n, h)),
        scratch_shapes=[pltpu.VMEM((HD, HD), F32)],
        compiler_params=_cparams(("arbitrary", "arbitrary")),
        name="retention",
    )(proj, proj, proj, proj, dmat, zeta, xi, decay, gn_g.reshape(1, -1), gn_b.reshape(1, -1))


def _outproj_kernel(x_ref, yc_ref, yn_ref, yr_ref, wc_ref, wn_ref, wr_ref, ga_ref, o_ref):
    y = jnp.dot(yc_ref[...], wc_ref[...], preferred_element_type=F32)
    y = y + jnp.dot(yn_ref[...], wn_ref[...], preferred_element_type=F32)
    y = y + jnp.dot(yr_ref[...], wr_ref[...], preferred_element_type=F32)
    o_ref[...] = x_ref[...] + ga_ref[...] * y


def _out_proj(x2, y_conv, y_nsa, y_ret, w_out_bf, gate_a):
    S = x2.shape[0]
    tm = 512
    wc, wn, wr = w_out_bf[:CONV_CH], w_out_bf[CONV_CH:CONV_CH + NSA_HEADS * HD], w_out_bf[CONV_CH + NSA_HEADS * HD:]

    def rows(w):
        return pl.BlockSpec((tm, w), lambda i: (i, 0))

    def whole(a):
        return pl.BlockSpec(a.shape, lambda i: (0, 0))

    return pl.pallas_call(
        _outproj_kernel,
        out_shape=jax.ShapeDtypeStruct((S, D_MODEL), F32),
        grid=(S // tm,),
        in_specs=[rows(D_MODEL), rows(y_conv.shape[1]), rows(y_nsa.shape[1]), rows(y_ret.shape[1]),
                  whole(wc), whole(wn), whole(wr), pl.BlockSpec((1, D_MODEL), lambda i: (0, 0))],
        out_specs=rows(D_MODEL),
        compiler_params=_cparams(("arbitrary",)),
        name="out_proj",
    )(x2, y_conv, y_nsa, y_ret, wc, wn, wr, gate_a)


def _router_kernel(x_ref, g_ref, sc_ref, sh_ref, rw_ref, rb_ref, h_ref, rt_ref, cnt_ref, carry_ref):
    i = pl.program_id(0)
    tm = x_ref.shape[0]

    @pl.when(i == 0)
    def _():
        carry_ref[...] = jnp.zeros_like(carry_ref)

    h = _modulated_norm(x_ref[...], g_ref[...], sc_ref[...], sh_ref[...])
    h_ref[...] = h
    logits = jnp.dot(h, rw_ref[...], precision=HI, preferred_element_type=F32) + rb_ref[...]
    lane = lax.broadcasted_iota(jnp.int32, (tm, LANES), 1)
    lane_f = lane.astype(F32)
    work = jnp.where(lane < N_EXPERTS, logits, -jnp.inf)
    onehot = jnp.zeros((tm, LANES), F32)
    vals, idxs = [], []
    for _ in range(TOP_K):
        m = jnp.max(work, axis=-1, keepdims=True)
        first = jnp.min(jnp.where(work == m, lane_f, float(LANES)), axis=-1, keepdims=True)
        hit = lane_f == first
        onehot = jnp.where(hit, 1.0, onehot)
        work = jnp.where(hit, -jnp.inf, work)
        vals.append(m)
        idxs.append(first)
    ex = [jnp.exp(v - vals[0]) for v in vals]
    den = ex[0] + ex[1] + ex[2] + ex[3]
    r = lax.broadcasted_iota(jnp.int32, (tm, tm), 0)
    c = lax.broadcasted_iota(jnp.int32, (tm, tm), 1)
    tri = jnp.where(c < r, 1.0, 0.0).astype(BF16)
    cum = jnp.dot(tri, onehot.astype(BF16), preferred_element_type=F32) + carry_ref[...]
    out = jnp.zeros((tm, LANES), F32)
    for kk in range(TOP_K):
        rank = jnp.sum(jnp.where(lane_f == idxs[kk], cum, 0.0), axis=-1, keepdims=True)
        out = jnp.where(lane == kk, idxs[kk], out)
        out = jnp.where(lane == TOP_K + kk, ex[kk] / den, out)
        out = jnp.where(lane == 2 * TOP_K + kk, rank, out)
    rt_ref[...] = out
    carry_ref[...] = carry_ref[...] + jnp.sum(onehot, axis=0, keepdims=True)
    cnt_ref[...] = carry_ref[...]


def _router(x2, g, scale, shift, router_w, router_b):
    T = x2.shape[0]
    tm = 256
    rw = jnp.pad(router_w, ((0, 0), (0, LANES - N_EXPERTS)))
    rb = jnp.pad(router_b, (0, LANES - N_EXPERTS)).reshape(1, LANES)
    row = pl.BlockSpec((1, D_MODEL), lambda i: (0, 0))
    return pl.pallas_call(
        _router_kernel,
        out_shape=(jax.ShapeDtypeStruct((T, D_MODEL), F32),
                   jax.ShapeDtypeStruct((T, LANES), F32),
                   jax.ShapeDtypeStruct((1, LANES), F32)),
        grid=(T // tm,),
        in_specs=[pl.BlockSpec((tm, D_MODEL), lambda i: (i, 0)), row, row, row,
                  pl.BlockSpec((D_MODEL, LANES), lambda i: (0, 0)),
                  pl.BlockSpec((1, LANES), lambda i: (0, 0))],
        out_specs=(pl.BlockSpec((tm, D_MODEL), lambda i: (i, 0)),
                   pl.BlockSpec((tm, LANES), lambda i: (i, 0)),
                   pl.BlockSpec((1, LANES), lambda i: (0, 0))),
        scratch_shapes=[pltpu.VMEM((1, LANES), F32)],
        compiler_params=_cparams(("arbitrary",)),
        name="moe_router",
    )(x2, g, scale, shift, rw, rb)


GATHER_UNROLL = 8


def _dispatch_kernel(tok_ref, nu_ref, h_hbm, o_ref, buf, sem):
    b = pl.program_id(0)
    nrow = o_ref.shape[0]
    n_used = nu_ref[0]

    def issue(blk):
        slot = blk % 2

        def body(c, carry):
            for u in range(GATHER_UNROLL):
                r = c * GATHER_UNROLL + u
                t = tok_ref[blk * nrow + r]
                pltpu.make_async_copy(h_hbm.at[pl.ds(t, 1)], buf.at[slot, pl.ds(r, 1)], sem.at[slot]).start()
            return carry

        lax.fori_loop(0, nrow // GATHER_UNROLL, body, 0)

    @pl.when((b == 0) & (n_used > 0))
    def _():
        issue(b)

    @pl.when(b + 1 < n_used)
    def _():
        issue(b + 1)

    @pl.when(b < n_used)
    def _():
        slot = b % 2
        pltpu.make_async_copy(h_hbm.at[pl.ds(0, nrow)], buf.at[slot], sem.at[slot]).wait()
        o_ref[...] = buf[slot].astype(o_ref.dtype)

    @pl.when(b >= n_used)
    def _():
        o_ref[...] = jnp.zeros_like(o_ref)


def _dispatch(h2, slot_tok, n_used):
    n_slots = slot_tok.shape[0]
    return pl.pallas_call(
        _dispatch_kernel,
        out_shape=jax.ShapeDtypeStruct((n_slots, D_MODEL), BF16),
        grid_spec=pltpu.PrefetchScalarGridSpec(
            num_scalar_prefetch=2,
            grid=(n_slots // MOE_SUB,),
            in_specs=[pl.BlockSpec(memory_space=pl.ANY)],
            out_specs=pl.BlockSpec((MOE_SUB, D_MODEL), lambda b, tok, nu: (b, 0)),
            scratch_shapes=[pltpu.VMEM((2, MOE_SUB, D_MODEL), F32), pltpu.SemaphoreType.DMA((2,))]),
        compiler_params=_cparams(("arbitrary",)),
        name="moe_dispatch",
    )(slot_tok, n_used, h2)


def _expert_kernel(ie_ref, ir_ref, ins_ref, xs_hbm, wgl_ref, wli_ref, bgl_ref, bli_ref, wdn_ref, bdn_ref,
                   ys_hbm, xbuf, acc, wg_s, wl_s, wd_s, sem_in, sem_out):
    w = pl.program_id(0)
    ct = pl.program_id(1)
    n_w = pl.num_programs(0)
    n_ct = pl.num_programs(1)
    nsub = ins_ref[w]

    def x_copy(item, sb):
        row = pl.multiple_of(ir_ref[item] + sb * MOE_SUB, MOE_SUB)
        slot = item % 2
        return pltpu.make_async_copy(xs_hbm.at[pl.ds(row, MOE_SUB)],
                                     xbuf.at[slot, pl.ds(sb * MOE_SUB, MOE_SUB)], sem_in.at[slot, sb])

    def y_copy(item, sb):
        row = pl.multiple_of(ir_ref[item] + sb * MOE_SUB, MOE_SUB)
        return pltpu.make_async_copy(acc.at[pl.ds(sb * MOE_SUB, MOE_SUB)],
                                     ys_hbm.at[pl.ds(row, MOE_SUB)], sem_out.at[sb])

    def for_subs(item, fn):
        n = ins_ref[item]
        for sb in range(MOE_SUBMAX):
            @pl.when(sb < n)
            def _():
                fn(item, sb)

    @pl.when(nsub > 0)
    def _():
        @pl.when((ct == 0) & (w == 0))
        def _():
            for_subs(w, lambda it, sb: x_copy(it, sb).start())

        @pl.when((ct == 1) & (w + 1 < n_w))
        def _():
            for_subs(jnp.minimum(w + 1, n_w - 1), lambda it, sb: x_copy(it, sb).start())

        @pl.when(ct == 0)
        def _():
            for_subs(w, lambda it, sb: x_copy(it, sb).wait())

        wg_s[...] = wgl_ref[0, 0].astype(BF16)
        wl_s[...] = wli_ref[0, 0].astype(BF16)
        wd_s[...] = wdn_ref[0, 0].astype(BF16)
        slot = w % 2

        @pl.when((ct == 0) & (w > 0))
        def _():
            for_subs(jnp.maximum(w - 1, 0), lambda it, sb: y_copy(it, sb).wait())

        def ffn_rows(r0, m_rows):
            x = xbuf[slot, pl.ds(r0, m_rows), :]
            glu = jnp.dot(x, wg_s[...], preferred_element_type=F32) + bgl_ref[0]
            lin = jnp.dot(x, wl_s[...], preferred_element_type=F32) + bli_ref[0]
            glu = jnp.minimum(glu, SWIGLU_LIMIT)
            lin = jnp.clip(lin, -SWIGLU_LIMIT, SWIGLU_LIMIT)
            act = glu * _sigmoid(SWIGLU_ALPHA * glu) * (lin + 1.0)
            y = jnp.dot(act.astype(BF16), wd_s[...], preferred_element_type=F32)

            @pl.when(ct == 0)
            def _():
                acc[pl.ds(r0, m_rows), :] = y + bdn_ref[0]

            @pl.when(ct > 0)
            def _():
                acc[pl.ds(r0, m_rows), :] += y

        def pair_body(pr, carry):
            ffn_rows(pl.multiple_of(pr * (2 * MOE_SUB), 2 * MOE_SUB), 2 * MOE_SUB)
            return carry

        lax.fori_loop(0, nsub // 2, pair_body, 0)

        @pl.when(nsub % 2 == 1)
        def _():
            ffn_rows(pl.multiple_of((nsub - 1) * MOE_SUB, MOE_SUB), MOE_SUB)

        @pl.when(ct == n_ct - 1)
        def _():
            for_subs(w, lambda it, sb: y_copy(it, sb).start())
            last = (w == n_w - 1) | (ins_ref[jnp.minimum(w + 1, n_w - 1)] == 0)

            @pl.when(last)
            def _():
                for_subs(w, lambda it, sb: y_copy(it, sb).wait())


def _experts(xs, item_e, item_row0, item_nsub, w_gu, b_gu, w_dn, b_dn, layer):
    n_slots = xs.shape[0]
    W = item_e.shape[0]
    te = MOE_TE
    n_ct = D_EXPERT // te
    rows = MOE_SUBMAX * MOE_SUB

    def ct_eff(w, ct, ins):
        return jnp.where(ins[w] > 0, ct, n_ct - 1)

    return pl.pallas_call(
        _expert_kernel,
        out_shape=jax.ShapeDtypeStruct((n_slots, D_MODEL), F32),
        grid_spec=pltpu.PrefetchScalarGridSpec(
            num_scalar_prefetch=3,
            grid=(W, n_ct),
            in_specs=[pl.BlockSpec(memory_space=pl.ANY),
                      pl.BlockSpec((1, 1, D_MODEL, te),
                                   lambda w, ct, ie, ir, ins: (layer, ie[w], 0, ct_eff(w, ct, ins))),
                      pl.BlockSpec((1, 1, D_MODEL, te),
                                   lambda w, ct, ie, ir, ins: (layer, ie[w], 0, n_ct + ct_eff(w, ct, ins))),
                      pl.BlockSpec((1, 1, te), lambda w, ct, ie, ir, ins: (ie[w], 0, ct_eff(w, ct, ins))),
                      pl.BlockSpec((1, 1, te), lambda w, ct, ie, ir, ins: (ie[w], 0, n_ct + ct_eff(w, ct, ins))),
                      pl.BlockSpec((1, 1, te, D_MODEL),
                                   lambda w, ct, ie, ir, ins: (layer, ie[w], ct_eff(w, ct, ins), 0)),
                      pl.BlockSpec((1, 1, D_MODEL), lambda w, ct, ie, ir, ins: (ie[w], 0, 0))],
            out_specs=pl.BlockSpec(memory_space=pl.ANY),
            scratch_shapes=[pltpu.VMEM((2, rows, D_MODEL), BF16), pltpu.VMEM((rows, D_MODEL), F32),
                            pltpu.VMEM((D_MODEL, te), BF16), pltpu.VMEM((D_MODEL, te), BF16),
                            pltpu.VMEM((te, D_MODEL), BF16),
                            pltpu.SemaphoreType.DMA((2, MOE_SUBMAX)), pltpu.SemaphoreType.DMA((MOE_SUBMAX,))]),
        compiler_params=_cparams(("arbitrary", "arbitrary"), vmem=60 * 1024 * 1024),
        name="moe_experts",
    )(item_e, item_row0, item_nsub, xs, w_gu, w_gu, b_gu.reshape(N_EXPERTS, 1, -1), b_gu.reshape(N_EXPERTS, 1, -1),
      w_dn, b_dn.reshape(N_EXPERTS, 1, -1))


def _combine_kernel(dest_ref, x_ref, rt_ref, gf_ref, fg_ref, ys_hbm, o_ref, buf, sem, *, final_norm):
    i = pl.program_id(0)
    n = pl.num_programs(0)
    tm = x_ref.shape[0]

    def issue(tile):
        slot = tile % 2

        def body(c, carry):
            for u in range(GATHER_UNROLL // TOP_K):
                r = c * (GATHER_UNROLL // TOP_K) + u
                for kk in range(TOP_K):
                    d = dest_ref[(tile * tm + r) * TOP_K + kk]
                    pltpu.make_async_copy(ys_hbm.at[pl.ds(d, 1)], buf.at[slot, kk, pl.ds(r, 1)],
                                          sem.at[slot]).start()
            return carry

        lax.fori_loop(0, tm * TOP_K // GATHER_UNROLL, body, 0)

    @pl.when(i == 0)
    def _():
        issue(i)

    @pl.when(i + 1 < n)
    def _():
        issue(i + 1)

    slot = i % 2
    for kk in range(TOP_K):
        pltpu.make_async_copy(ys_hbm.at[pl.ds(0, tm)], buf.at[slot, kk], sem.at[slot]).wait()
    rt = rt_ref[...]
    lane = lax.broadcasted_iota(jnp.int32, rt.shape, 1)
    moe = jnp.zeros((tm, D_MODEL), F32)
    for kk in range(TOP_K):
        moe = moe + buf[slot, kk] * _lane_pick(rt, lane, TOP_K + kk)
    out = x_ref[...] + gf_ref[...] * moe
    if final_norm:
        ms = jnp.mean(out * out, axis=-1, keepdims=True)
        out = out * lax.rsqrt(ms + NORM_EPS) * fg_ref[...]
    o_ref[...] = out


def _combine(x2, route, gate_f, final_g, ys, dest_flat, final_norm):
    T = x2.shape[0]
    tm = 128
    row = pl.BlockSpec((1, D_MODEL), lambda i, d: (0, 0))
    return pl.pallas_call(
        functools.partial(_combine_kernel, final_norm=final_norm),
        out_shape=jax.ShapeDtypeStruct((T, D_MODEL), F32),
        grid_spec=pltpu.PrefetchScalarGridSpec(
            num_scalar_prefetch=1,
            grid=(T // tm,),
            in_specs=[pl.BlockSpec((tm, D_MODEL), lambda i, d: (i, 0)),
                      pl.BlockSpec((tm, LANES), lambda i, d: (i, 0)), row, row,
                      pl.BlockSpec(memory_space=pl.ANY)],
            out_specs=pl.BlockSpec((tm, D_MODEL), lambda i, d: (i, 0)),
            scratch_shapes=[pltpu.VMEM((2, TOP_K, tm, D_MODEL), F32), pltpu.SemaphoreType.DMA((2,))]),
        compiler_params=_cparams(("arbitrary",)),
        name="moe_combine",
    )(dest_flat, x2, route, gate_f, final_g, ys)


def _moe_plan(route, counts_f):
    T = route.shape[0]
    e = route[:, 0:TOP_K].astype(jnp.int32)
    rank = route[:, 2 * TOP_K:3 * TOP_K].astype(jnp.int32)
    counts = counts_f[0, :N_EXPERTS].astype(jnp.int32)
    nsub = (counts + MOE_SUB - 1) // MOE_SUB
    padded = nsub * MOE_SUB
    pend = jnp.cumsum(padded)
    pstart = pend - padded
    dest = pstart[e] + rank
    n_slots = (T * TOP_K + MOE_SUB - 1) // MOE_SUB * MOE_SUB + N_EXPERTS * MOE_SUB
    tok = jnp.repeat(jnp.arange(T, dtype=jnp.int32), TOP_K)
    slot_tok = jnp.zeros((n_slots,), jnp.int32).at[dest.reshape(-1)].set(tok)
    n_used = (pend[-1] // MOE_SUB).astype(jnp.int32).reshape(1)
    n_items_max = N_EXPERTS + (n_slots // MOE_SUB) // MOE_SUBMAX
    per_e = (nsub + MOE_SUBMAX - 1) // MOE_SUBMAX
    iend = jnp.cumsum(per_e)
    w = jnp.arange(n_items_max, dtype=jnp.int32)
    ew = jnp.minimum(jnp.searchsorted(iend, w, side='right'), N_EXPERTS - 1).astype(jnp.int32)
    local = w - (iend[ew] - per_e[ew])
    live = w < iend[-1]
    item_nsub = jnp.where(live, jnp.clip(nsub[ew] - local * MOE_SUBMAX, 0, MOE_SUBMAX), 0).astype(jnp.int32)
    item_row0 = jnp.where(live, pstart[ew] + local * (MOE_SUBMAX * MOE_SUB), 0).astype(jnp.int32)
    last_e = ew[jnp.maximum(iend[-1] - 1, 0)]
    item_e = jnp.where(live, ew, last_e).astype(jnp.int32)
    return dest.reshape(-1).astype(jnp.int32), slot_tok, n_used, item_e, item_row0, item_nsub


def _repack_w_in(w_in_l):
    parts, width = [], 0
    for name in _ORDER:
        off, size = _SRC[name]
        parts.append(w_in_l[:, off:off + size])
        width += size
        if size % LANES:
            parts.append(jnp.zeros((D_MODEL, LANES - size % LANES), w_in_l.dtype))
            width += LANES - size % LANES
    parts.append(jnp.zeros((D_MODEL, PROJ_W - width), w_in_l.dtype))
    return jnp.concatenate(parts, axis=1).astype(BF16)


def _cover_matrix(S):
    n_strip = S // CMP_STRIDE
    n_cmp = (S - CMP_BLOCK) // CMP_STRIDE + 1
    n = np.arange(n_strip)[:, None]
    j = np.arange(LANES)[None, :]
    start, end = n * CMP_STRIDE, n * CMP_STRIDE + CMP_BLOCK - 1
    cov = (start <= j * SEL_BLOCK + SEL_BLOCK - 1) & (end >= j * SEL_BLOCK) & (n < n_cmp) & (j < S // SEL_BLOCK)
    return jnp.asarray(cov.astype(np.float32))


def _strips(proj, cb):
    S = proj.shape[0]
    t = proj[:, cb * LANES:(cb + NSA_KV_HEADS) * LANES].reshape(S, NSA_KV_HEADS, HD)
    return t.transpose(1, 0, 2).reshape(NSA_KV_HEADS, S // CMP_STRIDE, CMP_STRIDE * HD)


def kernel(x, c, ada_w, ada_b, norm_mix_g, w_in, conv_dw_w, conv_dw_b, conv_ln_g, conv_ln_b, nsa_pe_k, nsa_pe_v, nsa_cmp_k_w1, nsa_cmp_k_w2, nsa_cmp_v_w1, nsa_cmp_v_w2, ret_gn_g, ret_gn_b, w_out, norm_ffn_g, router_w, router_b, moe_w_gate_up, moe_b_gate_up, moe_w_down, moe_b_down, final_norm_g):
    B, S, _ = x.shape
    assert B == 1 and c.shape[0] == 1
    x2 = x.reshape(S, D_MODEL)
    mod = _ada_mod(c, ada_w, ada_b).reshape(DEPTH, 6, 1, D_MODEL)
    tabs = _rotary_tables(S)
    cover = _cover_matrix(S)
    final_g = final_norm_g.reshape(1, D_MODEL)
    for l in range(DEPTH):
        shift_a, scale_a, gate_a, shift_f, scale_f, gate_f = [mod[l, i] for i in range(6)]
        proj = _in_proj(x2, norm_mix_g[l].reshape(1, -1), scale_a, shift_a, _repack_w_in(w_in[l]), tabs)
        y_conv = _conv_group(proj, conv_dw_w[l], conv_dw_b[l], conv_ln_g[l], conv_ln_b[l])
        kc = _compress(_strips(proj, CB_KC), nsa_pe_k[l], nsa_cmp_k_w1[l], nsa_cmp_k_w2[l])
        vc = _compress(_strips(proj, CB_VC), nsa_pe_v[l], nsa_cmp_v_w1[l], nsa_cmp_v_w2[l])
        oc, selbias = _cmp_attention(proj, kc, vc, cover)
        ksa, vsb, kwb, vwb = _kv_prep(proj)
        y_nsa = _nsa_attention(proj, selbias, oc, ksa, vsb, kwb, vwb)
        y_ret = _retention(proj, ret_gn_g[l], ret_gn_b[l])
        x2 = _out_proj(x2, y_conv, y_nsa, y_ret, w_out[l].astype(BF16), gate_a)
        h2, route, counts = _router(x2, norm_ffn_g[l].reshape(1, -1), scale_f, shift_f, router_w[l], router_b[l])
        dest, slot_tok, n_used, item_e, item_row0, item_nsub = _moe_plan(route, counts)
        xs = _dispatch(h2, slot_tok, n_used)
        ys = _experts(xs, item_e, item_row0, item_nsub, moe_w_gate_up, moe_b_gate_up[l], moe_w_down, moe_b_down[l], l)
        x2 = _combine(x2, route, gate_f, final_g, ys, dest, final_norm=(l == DEPTH - 1))
    return x2.reshape(B, S, D_MODEL)
```

```python
import functools
import math

import numpy as np
import jax
import jax.numpy as jnp
from jax import lax
from jax.experimental import pallas as pl
from jax.experimental.pallas import tpu as pltpu

F32 = jnp.float32
BF16 = jnp.bfloat16
HI = lax.Precision.HIGHEST

D_MODEL = 2048
DEPTH = 2
CONV_CH = 512
CONV_WIDTH = 31
HD = 128
NSA_HEADS = 8
NSA_KV_HEADS = 2
NSA_GROUP = 4
CMP_BLOCK = 32
CMP_STRIDE = 16
SEL_BLOCK = 64
SEL_TOPN = 16
WINDOW = 512
ROPE_THETA = 500000.0
ROPE_DIM = 32
RET_HEADS = 4
RET_CHUNK = 128
RET_THETA = 10000.0
N_EXPERTS = 32
TOP_K = 4
D_EXPERT = 2048
SWIGLU_LIMIT = 7.0
SWIGLU_ALPHA = 1.702
NORM_EPS = 1e-6
NEG = -1e30
BIG = 1e30
LOG2E = math.log2(math.e)

LANES = 128
VMEM_LIMIT = 56 * 1024 * 1024

CB_Q, CB_KC, CB_KS, CB_KW = 0, 8, 10, 12
CB_RQ, CB_RK = 14, 18
CB_VC, CB_CV, CB_CG, CB_VS, CB_VW, CB_GT, CB_RV, CB_RG = 22, 24, 28, 32, 34, 36, 37, 41
PROJ_BLOCKS = 46
PROJ_W = PROJ_BLOCKS * LANES
PROJ_TN = 256
NSA_ROT_TILES = CB_RQ * LANES // PROJ_TN
RET_ROT_TILES = CB_VC * LANES // PROJ_TN
_SRC = {'cv': (0, 512), 'cg': (512, 512), 'q': (1024, 1024), 'kc': (2048, 256), 'vc': (2304, 256),
        'ks': (2560, 256), 'vs': (2816, 256), 'kw': (3072, 256), 'vw': (3328, 256), 'gt': (3584, 24),
        'rq': (3608, 512), 'rk': (4120, 512), 'rv': (4632, 512), 'rg': (5144, 512)}
_ORDER = ['q', 'kc', 'ks', 'kw', 'rq', 'rk', 'vc', 'cv', 'cg', 'vs', 'vw', 'gt', 'rv', 'rg']

MOE_SUB = 256
MOE_SUBMAX = 5
MOE_TE = 512


def _sigmoid(x):
    return 1.0 / (1.0 + jnp.exp(-x))


def _cparams(sem, vmem=VMEM_LIMIT):
    return pltpu.CompilerParams(dimension_semantics=sem, vmem_limit_bytes=vmem)


def _ada_kernel(c_ref, w_ref, b_ref, o_ref, sc_ref):
    @pl.when(pl.program_id(1) == 0)
    def _():
        cv = c_ref[...]
        sc_ref[...] = cv * _sigmoid(cv)

    tn = o_ref.shape[-1]

    def body(i, acc):
        k0 = pl.multiple_of(i * 64, 64)
        p = w_ref[0, pl.ds(k0, 64), :] * sc_ref[pl.ds(k0, 64), :]
        return acc + p.reshape(8, 8, tn).sum(axis=0)

    acc = lax.fori_loop(0, D_MODEL // 64, body, jnp.zeros((8, tn), F32))
    o_ref[0] = jnp.sum(acc, axis=0, keepdims=True) + b_ref[0]


def _ada_mod(c, ada_w, ada_b):
    tn = 1024
    n = 6 * D_MODEL
    return pl.pallas_call(
        _ada_kernel,
        out_shape=jax.ShapeDtypeStruct((DEPTH, 1, n), F32),
        grid=(DEPTH, n // tn),
        in_specs=[pl.BlockSpec((D_MODEL, 1), lambda l, j: (0, 0)),
                  pl.BlockSpec((1, D_MODEL, tn), lambda l, j: (l, 0, j)),
                  pl.BlockSpec((1, 1, tn), lambda l, j: (l, 0, j))],
        out_specs=pl.BlockSpec((1, 1, tn), lambda l, j: (l, 0, j)),
        scratch_shapes=[pltpu.VMEM((D_MODEL, 1), F32)],
        compiler_params=_cparams(("arbitrary", "arbitrary")),
        name="ada_mod",
    )(c.reshape(D_MODEL, 1), ada_w, ada_b.reshape(DEPTH, 1, n))


def _modulated_norm(x, g, scale, shift):
    ms = jnp.mean(x * x, axis=-1, keepdims=True)
    return x * lax.rsqrt(ms + NORM_EPS) * g * (1.0 + scale) + shift


def _inproj_kernel(x_ref, g_ref, sc_ref, sh_ref, w_ref, cn_ref, s1_ref, s2_ref, cr_ref, sr_ref,
                   o_ref, h_ref):
    j = pl.program_id(1)
    nsub = o_ref.shape[-1] // LANES

    @pl.when(j == 0)
    def _():
        h_ref[...] = _modulated_norm(x_ref[...], g_ref[...], sc_ref[...], sh_ref[...]).astype(BF16)

    def matmul():
        return jnp.dot(h_ref[...], w_ref[...], preferred_element_type=F32)

    @pl.when(j < NSA_ROT_TILES)
    def _():
        acc = matmul()
        for c in range(nsub):
            sub = acc[:, c * LANES:(c + 1) * LANES]
            o_ref[:, c * LANES:(c + 1) * LANES] = (
                sub * cn_ref[...] + pltpu.roll(sub, LANES - ROPE_DIM // 2, 1) * s1_ref[...]
                + pltpu.roll(sub, ROPE_DIM // 2, 1) * s2_ref[...])

    @pl.when((j >= NSA_ROT_TILES) & (j < RET_ROT_TILES))
    def _():
        acc = matmul()
        for c in range(nsub):
            sub = acc[:, c * LANES:(c + 1) * LANES]
            o_ref[:, c * LANES:(c + 1) * LANES] = sub * cr_ref[...] + pltpu.roll(sub, HD // 2, 1) * sr_ref[...]

    @pl.when(j >= RET_ROT_TILES)
    def _():
        o_ref[...] = matmul()


def _in_proj(x2, g, scale, shift, w_bf, tabs):
    S = x2.shape[0]
    tm, tn = 1024, PROJ_TN
    row = pl.BlockSpec((1, D_MODEL), lambda i, j: (0, 0))
    tab = pl.BlockSpec((tm, LANES), lambda i, j: (i, 0))
    return pl.pallas_call(
        _inproj_kernel,
        out_shape=jax.ShapeDtypeStruct((S, PROJ_W), F32),
        grid=(S // tm, PROJ_W // tn),
        in_specs=[pl.BlockSpec((tm, D_MODEL), lambda i, j: (i, 0)), row, row, row,
                  pl.BlockSpec((D_MODEL, tn), lambda i, j: (0, j)), tab, tab, tab, tab, tab],
        out_specs=pl.BlockSpec((tm, tn), lambda i, j: (i, j)),
        scratch_shapes=[pltpu.VMEM((tm, D_MODEL), BF16)],
        compiler_params=_cparams(("arbitrary", "arbitrary")),
        name="in_proj",
    )(x2, g, scale, shift, w_bf, *tabs)


def _rotary_tables(S):
    pos = jnp.arange(S, dtype=F32)[:, None]
    half = ROPE_DIM // 2
    inv = ROPE_THETA ** (-jnp.arange(half, dtype=F32) * 2.0 / ROPE_DIM)
    ang = pos * inv[None, :]
    cos, sin = jnp.cos(ang), jnp.sin(ang)
    ones = jnp.ones((S, LANES - ROPE_DIM), F32)
    zeros_r = jnp.zeros((S, LANES - ROPE_DIM), F32)
    zeros_h = jnp.zeros((S, half), F32)
    cn = jnp.concatenate([cos, cos, ones], axis=1)
    s1 = jnp.concatenate([-sin, zeros_h, zeros_r], axis=1)
    s2 = jnp.concatenate([zeros_h, sin, zeros_r], axis=1)
    halfr = HD // 2
    invr = RET_THETA ** (-jnp.arange(halfr, dtype=F32) * 2.0 / HD)
    angr = pos * invr[None, :]
    cosr, sinr = jnp.cos(angr), jnp.sin(angr)
    cr = jnp.concatenate([cosr, cosr], axis=1)
    sr = jnp.concatenate([-sinr, sinr], axis=1)
    return cn, s1, s2, cr, sr


def _conv_kernel(cv_ref, cg_ref, dw_ref, db_ref, lg_ref, lb_ref, o_ref, hb_ref):
    i = pl.program_id(0)
    ts = o_ref.shape[0]
    halo = 32

    @pl.when(i == 0)
    def _():
        hb_ref[0:halo, :] = jnp.zeros((halo, CONV_CH), F32)

    @pl.when(i > 0)
    def _():
        hb_ref[0:halo, :] = hb_ref[ts:ts + halo, :]

    hb_ref[halo:halo + ts, :] = cv_ref[...] * _sigmoid(cg_ref[...])
    acc = jnp.zeros((ts, CONV_CH), F32) + db_ref[...]
    for w in range(CONV_WIDTH):
        acc = acc + hb_ref[pl.ds(halo - (CONV_WIDTH - 1) + w, ts), :] * dw_ref[w:w + 1, :]
    mu = jnp.mean(acc, axis=-1, keepdims=True)
    var = jnp.mean(jnp.square(acc - mu), axis=-1, keepdims=True)
    y = (acc - mu) * lax.rsqrt(var + NORM_EPS) * lg_ref[...] + lb_ref[...]
    o_ref[...] = (y * _sigmoid(y)).astype(o_ref.dtype)


def _conv_group(proj, dw_w, dw_b, ln_g, ln_b):
    S = proj.shape[0]
    ts = 256
    vec = pl.BlockSpec((1, CONV_CH), lambda i: (0, 0))
    return pl.pallas_call(
        _conv_kernel,
        out_shape=jax.ShapeDtypeStruct((S, CONV_CH), BF16),
        grid=(S // ts,),
        in_specs=[pl.BlockSpec((ts, CONV_CH), lambda i: (i, CB_CV * LANES // CONV_CH)),
                  pl.BlockSpec((ts, CONV_CH), lambda i: (i, CB_CG * LANES // CONV_CH)),
                  pl.BlockSpec((CONV_WIDTH, CONV_CH), lambda i: (0, 0)), vec, vec, vec],
        out_specs=pl.BlockSpec((ts, CONV_CH), lambda i: (i, 0)),
        scratch_shapes=[pltpu.VMEM((ts + 32, CONV_CH), F32)],
        compiler_params=_cparams(("arbitrary",)),
        name="conv_group",
    )(proj, proj, dw_w, dw_b.reshape(1, -1), ln_g.reshape(1, -1), ln_b.reshape(1, -1))


def _compress_kernel(x_ref, pe_ref, w1_ref, w2_ref, o_ref):
    X = x_ref[0]
    nrow = X.shape[0]
    half = CMP_STRIDE * HD
    A = jnp.dot(X, w1_ref[0:half, :], precision=HI, preferred_element_type=F32)
    B = jnp.dot(X, w1_ref[half:2 * half, :], precision=HI, preferred_element_type=F32)
    pe8 = jnp.broadcast_to(pe_ref[...], (8, 2 * half))
    cst = jnp.dot(pe8, w1_ref[...], precision=HI, preferred_element_type=F32)[0:1]
    pre = A + pltpu.roll(B, nrow - 1, 0) + cst
    act = pre * _sigmoid(pre)
    out = jnp.dot(act, w2_ref[...], precision=HI, preferred_element_type=F32)
    rows = lax.broadcasted_iota(jnp.int32, out.shape, 0)
    o_ref[0] = jnp.where(rows < nrow - 1, out, 0.0)


def _compress(strips, pe, w1, w2):
    nh, nrow, width = strips.shape
    return pl.pallas_call(
        _compress_kernel,
        out_shape=jax.ShapeDtypeStruct((nh, nrow, HD), F32),
        grid=(nh,),
        in_specs=[pl.BlockSpec((1, nrow, width), lambda h: (h, 0, 0)),
                  pl.BlockSpec((1, width * 2), lambda h: (0, 0)),
                  pl.BlockSpec((width * 2, HD), lambda h: (0, 0)),
                  pl.BlockSpec((HD, HD), lambda h: (0, 0))],
        out_specs=pl.BlockSpec((1, nrow, HD), lambda h: (h, 0, 0)),
        compiler_params=_cparams(("arbitrary",)),
        name="nsa_compress",
    )(strips, pe.reshape(1, -1), w1, w2)


def _lane_pick(vals, lane, idx):
    return jnp.sum(jnp.where(lane == idx, vals, 0.0), axis=-1, keepdims=True)


def _cmp_kernel(q_ref, gt_ref, kc_ref, vc_ref, cov_ref, oc_ref, sb_ref, *, n_sel):
    k = pl.program_id(0)
    i = pl.program_id(1)
    tq = q_ref.shape[0]
    ncp = kc_ref.shape[1]
    scale = HD ** -0.5
    tpos = i * tq + lax.broadcasted_iota(jnp.int32, (tq, 1), 0)
    ncol = lax.broadcasted_iota(jnp.int32, (1, ncp), 1)
    mc = (ncol * CMP_STRIDE + (CMP_BLOCK - 1)) <= tpos
    anyv = tpos >= (CMP_BLOCK - 1)
    kc = kc_ref[0]
    kc_hi = kc.astype(BF16)
    kc_lo = (kc - kc_hi.astype(F32)).astype(BF16)
    kc_cat = jnp.concatenate([kc_hi, kc_lo, kc_hi], axis=1)
    cov = cov_ref[...].astype(BF16)
    vc = vc_ref[0].astype(BF16)
    lane = lax.broadcasted_iota(jnp.int32, (tq, LANES), 1)
    sig = _sigmoid(gt_ref[...])
    psum = jnp.zeros((tq, ncp), F32)
    for g in range(NSA_GROUP):
        qg = q_ref[:, g * HD:(g + 1) * HD] * scale
        q_hi = qg.astype(BF16)
        q_lo = (qg - q_hi.astype(F32)).astype(BF16)
        s = lax.dot_general(jnp.concatenate([q_hi, q_hi, q_lo], axis=1), kc_cat, (((1,), (1,)), ((), ())),
                            preferred_element_type=F32)
        s = jnp.where(mc, s, NEG)
        m = jnp.max(s, axis=-1, keepdims=True)
        e = jnp.where(mc, jnp.exp(s - m), 0.0)
        l = jnp.sum(e, axis=-1, keepdims=True)
        p = e / jnp.where(anyv, l, 1.0)
        o = jnp.dot(p.astype(BF16), vc, preferred_element_type=F32)
        gate = _lane_pick(sig, lane, k * NSA_GROUP + g)
        oc_ref[:, g * HD:(g + 1) * HD] = gate * o
        psum = psum + p
    p_hi = psum.astype(BF16)
    p_lo = (psum - p_hi.astype(F32)).astype(BF16)
    imp = (jnp.dot(p_hi, cov, preferred_element_type=F32)
           + jnp.dot(p_lo, cov, preferred_element_type=F32))
    valid = (lane * SEL_BLOCK <= tpos) & (lane < n_sel)
    forced = (lane == 0) | (lane == tpos // SEL_BLOCK)
    work = jnp.where(forced, BIG, jnp.where(valid, imp, NEG))
    lane_f = lane.astype(F32)
    chosen = jnp.zeros((tq, LANES), F32)
    for _ in range(min(SEL_TOPN, n_sel)):
        m = jnp.max(work, axis=-1, keepdims=True)
        first = jnp.min(jnp.where(work == m, lane_f, float(LANES)), axis=-1, keepdims=True)
        hit = lane_f == first
        chosen = jnp.where(hit, 1.0, chosen)
        work = jnp.where(hit, -jnp.inf, work)
    keep = (chosen > 0.5) & valid
    sb_ref[0] = jnp.where(keep, 0.0, NEG).astype(BF16)


def _cmp_attention(proj, kc, vc, cover):
    S = proj.shape[0]
    tq = 256
    ncp = kc.shape[1]
    return pl.pallas_call(
        functools.partial(_cmp_kernel, n_sel=S // SEL_BLOCK),
        out_shape=(jax.ShapeDtypeStruct((S, NSA_HEADS * HD), F32),
                   jax.ShapeDtypeStruct((NSA_KV_HEADS, S, LANES), BF16)),
        grid=(NSA_KV_HEADS, S // tq),
        in_specs=[pl.BlockSpec((tq, NSA_GROUP * HD), lambda k, i: (i, CB_Q // NSA_GROUP + k)),
                  pl.BlockSpec((tq, LANES), lambda k, i: (i, CB_GT)),
                  pl.BlockSpec((1, ncp, HD), lambda k, i: (k, 0, 0)),
                  pl.BlockSpec((1, ncp, HD), lambda k, i: (k, 0, 0)),
                  pl.BlockSpec((ncp, LANES), lambda k, i: (0, 0))],
        out_specs=(pl.BlockSpec((tq, NSA_GROUP * HD), lambda k, i: (i, k)),
                   pl.BlockSpec((1, tq, LANES), lambda k, i: (k, i, 0))),
        compiler_params=_cparams(("arbitrary", "arbitrary")),
        name="nsa_cmp_select",
    )(proj, proj, kc, vc, cover)


def _kvprep_kernel(ks_ref, vs_ref, kw_ref, vw_ref, ksa_ref, vsb_ref, kwb_ref, vwb_ref):
    i = pl.program_id(1)
    ts = ks_ref.shape[0]
    rows = i * ts + lax.broadcasted_iota(jnp.int32, (ts, LANES), 0)
    lane = lax.broadcasted_iota(jnp.int32, (ts, LANES), 1)
    ksa_ref[0, :, 0:HD] = ks_ref[...].astype(BF16)
    ksa_ref[0, :, HD:2 * HD] = jnp.where(lane == rows // SEL_BLOCK, 1.0, 0.0).astype(BF16)
    vsb_ref[0, 0] = vs_ref[...].T.astype(BF16)
    kwb_ref[0] = kw_ref[...].astype(BF16)
    for t in range(ts // LANES):
        vwb_ref[0, t] = vw_ref[t * LANES:(t + 1) * LANES, :].T.astype(BF16)


def _kv_prep(proj):
    S = proj.shape[0]
    ts = NSA_TK
    nw = ts // LANES

    def col(cb):
        return pl.BlockSpec((ts, HD), lambda k, i: (i, cb + k))

    return pl.pallas_call(
        _kvprep_kernel,
        out_shape=(jax.ShapeDtypeStruct((NSA_KV_HEADS, S, 2 * HD), BF16),
                   jax.ShapeDtypeStruct((NSA_KV_HEADS, S // ts, HD, ts), BF16),
                   jax.ShapeDtypeStruct((NSA_KV_HEADS, S, HD), BF16),
                   jax.ShapeDtypeStruct((NSA_KV_HEADS, S // LANES, HD, LANES), BF16)),
        grid=(NSA_KV_HEADS, S // ts),
        in_specs=[col(CB_KS), col(CB_VS), col(CB_KW), col(CB_VW)],
        out_specs=(pl.BlockSpec((1, ts, 2 * HD), lambda k, i: (k, i, 0)),
                   pl.BlockSpec((1, 1, HD, ts), lambda k, i: (k, i, 0, 0)),
                   pl.BlockSpec((1, ts, HD), lambda k, i: (k, i, 0)),
                   pl.BlockSpec((1, nw, HD, LANES), lambda k, i: (k, i, 0, 0))),
        compiler_params=_cparams(("arbitrary", "arbitrary")),
        name="nsa_kv_prep",
    )(proj, proj, proj, proj)


NSA_TQ = 256
NSA_TK = 512
NSA_WK = WINDOW + NSA_TQ
NSA_CH = 32


def _nsa_kernel(q_ref, sb_ref, oc_ref, gt_ref, ks_ref, vst_ref, kw_ref, vwt_ref, y_ref, s_ref, p_ref, acc_ref):
    k = pl.program_id(0)
    i = pl.program_id(1)
    tq = NSA_TQ
    cols = NSA_GROUP * tq
    q0 = i * tq
    nt = (((1,), (1,)), ((), ()))
    nsubq = tq // LANES
    qs = jnp.concatenate([q_ref[sub * LANES:(sub + 1) * LANES, g * HD:(g + 1) * HD]
                          for sub in range(nsubq) for g in range(NSA_GROUP)], axis=0)
    qs = (qs * (HD ** -0.5 * LOG2E)).astype(BF16)
    sbs = jnp.concatenate([sb_ref[0, sub * LANES:(sub + 1) * LANES, :]
                           for sub in range(nsubq) for _ in range(NSA_GROUP)], axis=0)
    qaug = jnp.concatenate([qs, sbs], axis=1)
    lane_q = lax.broadcasted_iota(jnp.int32, (1, LANES), 1)

    def qpos(c):
        return q0 + (c // NSA_GROUP) * LANES + lane_q

    def softmax_tile(n_keys, k0, m_old, mask_fn, c_list):
        def scores(r0, c):
            s = s_ref[r0:r0 + NSA_CH, c * LANES:(c + 1) * LANES]
            if mask_fn is None:
                return s
            kpos = k0 + r0 + lax.broadcasted_iota(jnp.int32, (NSA_CH, 1), 0)
            return jnp.where(mask_fn(kpos, qpos(c)), s, NEG)

        def fold(x):
            return x.reshape(NSA_CH // 8, 8, LANES)

        m_new, sums = [], []
        for c in c_list:
            mx8 = jnp.max(fold(scores(0, c)), axis=0)
            for r0 in range(NSA_CH, n_keys, NSA_CH):
                mx8 = jnp.maximum(mx8, jnp.max(fold(scores(r0, c)), axis=0))
            mx = jnp.max(mx8, axis=0, keepdims=True)
            mc = mx if m_old is None else jnp.maximum(m_old[:, c * LANES:(c + 1) * LANES], mx)
            tot8 = jnp.zeros((8, LANES), F32)
            for r0 in range(0, n_keys, NSA_CH):
                p = jnp.exp2(scores(r0, c) - mc)
                tot8 = tot8 + jnp.sum(fold(p), axis=0)
                p_ref[r0:r0 + NSA_CH, c * LANES:(c + 1) * LANES] = p.astype(BF16)
            m_new.append(mc)
            sums.append(jnp.sum(tot8, axis=0, keepdims=True))
        return m_new, sums

    acc_ref[...] = jnp.zeros((HD, cols), F32)

    def sel_tile(j, carry, causal):
        m_old, l_old = carry
        k0 = pl.multiple_of(j * NSA_TK, NSA_TK)
        kt = ks_ref[0, pl.ds(k0, NSA_TK), :]
        vt = vst_ref[0, j]
        mask_fn = (lambda kpos, qp: kpos <= qp) if causal else None
        halves = [slice(h * 2 * LANES, (h + 1) * 2 * LANES) for h in range(cols // (2 * LANES))]
        for hs in halves:
            s_ref[0:NSA_TK, hs] = lax.dot_general(kt, qaug[hs], nt, preferred_element_type=F32)
        m_out, l_out = [], []
        for h, hs in enumerate(halves):
            m_new, sums = softmax_tile(NSA_TK, k0, m_old, mask_fn, [2 * h, 2 * h + 1])
            m_new, sums = jnp.concatenate(m_new, axis=1), jnp.concatenate(sums, axis=1)
            alpha = jnp.exp2(m_old[:, hs] - m_new)
            acc_ref[:, hs] = acc_ref[:, hs] * alpha + jnp.dot(vt, p_ref[0:NSA_TK, hs],
                                                              preferred_element_type=F32)
            m_out.append(m_new)
            l_out.append(alpha * l_old[:, hs] + sums)
        return jnp.concatenate(m_out, axis=1), jnp.concatenate(l_out, axis=1)

    n_full = (q0 + tq - 1) // NSA_TK
    init = (jnp.full((1, cols), NEG, F32), jnp.zeros((1, cols), F32))
    carry = lax.fori_loop(0, n_full, lambda j, c: sel_tile(j, c, False), init)
    _, l_s = sel_tile(n_full, carry, True)
    ot_s = acc_ref[...] / l_s

    w0 = pl.multiple_of(jnp.maximum(q0 - WINDOW, 0), tq)
    s_ref[...] = lax.dot_general(kw_ref[0, pl.ds(w0, NSA_WK), :], qs, nt, preferred_element_type=F32)
    _, l_w = softmax_tile(NSA_WK, w0, None, lambda kpos, qp: (kpos <= qp) & (kpos > qp - WINDOW),
                          list(range(cols // LANES)))
    l_w = jnp.concatenate(l_w, axis=1)
    ot_w = jnp.zeros((HD, cols), F32)
    for t in range(NSA_WK // LANES):
        ot_w = ot_w + jnp.dot(vwt_ref[0, w0 // LANES + t], p_ref[t * LANES:(t + 1) * LANES, :],
                              preferred_element_type=F32)
    ot_w = ot_w / l_w

    lane = lax.broadcasted_iota(jnp.int32, (LANES, LANES), 1)
    for sub in range(nsubq):
        rs = slice(sub * LANES, (sub + 1) * LANES)
        sig = _sigmoid(gt_ref[rs, :])
        for g in range(NSA_GROUP):
            head = k * NSA_GROUP + g
            cs = slice((sub * NSA_GROUP + g) * LANES, (sub * NSA_GROUP + g + 1) * LANES)
            g_s = _lane_pick(sig, lane, NSA_HEADS + head)
            g_w = _lane_pick(sig, lane, 2 * NSA_HEADS + head)
            y = oc_ref[rs, g * HD:(g + 1) * HD] + g_s * ot_s[:, cs].T + g_w * ot_w[:, cs].T
            y_ref[rs, g * HD:(g + 1) * HD] = y.astype(y_ref.dtype)


def _nsa_attention(proj, selbias, oc, ksa, vst, kwb, vwt):
    S = proj.shape[0]
    tq = NSA_TQ
    cols = NSA_GROUP * tq
    assert S >= NSA_WK and tq % LANES == 0 and NSA_TK % tq == 0

    def full(a):
        return pl.BlockSpec((1,) + a.shape[1:], lambda k, i: (k,) + (0,) * (a.ndim - 1))

    return pl.pallas_call(
        _nsa_kernel,
        out_shape=jax.ShapeDtypeStruct((S, NSA_HEADS * HD), BF16),
        grid=(NSA_KV_HEADS, S // tq),
        in_specs=[pl.BlockSpec((tq, NSA_GROUP * HD), lambda k, i: (i, CB_Q // NSA_GROUP + k)),
                  pl.BlockSpec((1, tq, LANES), lambda k, i: (k, i, 0)),
                  pl.BlockSpec((tq, NSA_GROUP * HD), lambda k, i: (i, k)),
                  pl.BlockSpec((tq, LANES), lambda k, i: (i, CB_GT)),
                  full(ksa), full(vst), full(kwb), full(vwt)],
        out_specs=pl.BlockSpec((tq, NSA_GROUP * HD), lambda k, i: (i, k)),
        scratch_shapes=[pltpu.VMEM((NSA_WK, cols), F32), pltpu.VMEM((NSA_WK, cols), BF16),
                        pltpu.VMEM((HD, cols), F32)],
        compiler_params=_cparams(("arbitrary", "arbitrary")),
        name="nsa_sel_win",
    )(proj, selbias, oc, proj, ksa, vst, kwb, vwt)


def _ret_kernel(q_ref, k_ref, v_ref, g_ref, dm_ref, ze_ref, xi_ref, dc_ref, gg_ref, gb_ref, o_ref, st_ref):
    n = pl.program_id(1)
    C = RET_CHUNK
    nt = (((1,), (1,)), ((), ()))
    tn = (((0,), (0,)), ((), ()))

    @pl.when(n == 0)
    def _():
        st_ref[...] = jnp.zeros_like(st_ref)

    dmat = dm_ref[0]
    zeta = ze_ref[0]
    xi = xi_ref[0]
    decay = dc_ref[0]
    for c in range(q_ref.shape[0] // C):
        sl = slice(c * C, (c + 1) * C)
        q = q_ref[sl, :]
        kk = k_ref[sl, :] * (HD ** -0.5)
        v = v_ref[sl, :]
        qb, kb, vb = q.astype(BF16), kk.astype(BF16), v.astype(BF16)
        inner = lax.dot_general(qb, kb, nt, preferred_element_type=F32) * dmat
        o = jnp.dot(inner.astype(BF16), vb, preferred_element_type=F32)
        state = st_ref[...]
        o = o + jnp.dot(qb, state.astype(BF16), preferred_element_type=F32) * xi
        kv = lax.dot_general((kk * zeta).astype(BF16), vb, tn, preferred_element_type=F32)
        st_ref[...] = decay * state + kv
        mu = jnp.mean(o, axis=-1, keepdims=True)
        var = jnp.mean(jnp.square(o - mu), axis=-1, keepdims=True)
        y = (o - mu) * lax.rsqrt(var + NORM_EPS) * gg_ref[...] + gb_ref[...]
        gt = g_ref[sl, :]
        o_ref[sl, :] = (y * (gt * _sigmoid(gt))).astype(o_ref.dtype)


def _retention(proj, gn_g, gn_b):
    S = proj.shape[0]
    C = RET_CHUNK
    ts = 1024 if S % 1024 == 0 else C
    H = RET_HEADS
    log_g = jnp.log(1.0 - 2.0 ** (-5.0 - jnp.arange(H, dtype=F32)))
    i = jnp.arange(C, dtype=F32)
    diff = i[:, None] - i[None, :]
    dmat = jnp.where(diff >= 0, jnp.exp(log_g[:, None, None] * jnp.maximum(diff, 0.0)), 0.0)
    zeta = jnp.exp(log_g[:, None] * (C - 1.0 - i))[:, :, None]
    xi = jnp.exp(log_g[:, None] * (i + 1.0))[:, :, None]
    decay = jnp.broadcast_to(jnp.exp(log_g * C)[:, None, None], (H, 1, HD))

    def col(cb):
        return pl.BlockSpec((ts, HD), lambda h, n: (n, cb + h))

    return pl.pallas_call(
        _ret_kernel,
        out_shape=jax.ShapeDtypeStruct((S, H * HD), BF16),
        grid=(H, S // ts),
        in_specs=[col(CB_RQ), col(CB_RK), col(CB_RV), col(CB_RG),
                  pl.BlockSpec((1, C, C), lambda h, n: (h, 0, 0)),
                  pl.BlockSpec((1, C, 1), lambda h, n: (h, 0, 0)),
                  pl.BlockSpec((1, C, 1), lambda h, n: (h, 0, 0)),
                  pl.BlockSpec((1, 1, HD), lambda h, n: (h, 0, 0)),
                  pl.BlockSpec((1, HD), lambda h, n: (0, h)),
                  pl.BlockSpec((1, HD), lambda h, n: (0, h))],
        out_specs=pl.BlockSpec((ts, HD), lambda h, n: (n, h)),
        scratch_shapes=[pltpu.VMEM((HD, HD), F32)],
        compiler_params=_cparams(("arbitrary", "arbitrary")),
        name="retention",
    )(proj, proj, proj, proj, dmat, zeta, xi, decay, gn_g.reshape(1, -1), gn_b.reshape(1, -1))


def _outproj_kernel(x_ref, yc_ref, yn_ref, yr_ref, wc_ref, wn_ref, wr_ref, ga_ref, o_ref):
    y = jnp.dot(yc_ref[...], wc_ref[...], preferred_element_type=F32)
    y = y + jnp.dot(yn_ref[...], wn_ref[...], preferred_element_type=F32)
    y = y + jnp.dot(yr_ref[...], wr_ref[...], preferred_element_type=F32)
    o_ref[...] = x_ref[...] + ga_ref[...] * y


def _out_proj(x2, y_conv, y_nsa, y_ret, w_out_bf, gate_a):
    S = x2.shape[0]
    tm = 512
    wc, wn, wr = w_out_bf[:CONV_CH], w_out_bf[CONV_CH:CONV_CH + NSA_HEADS * HD], w_out_bf[CONV_CH + NSA_HEADS * HD:]

    def rows(w):
        return pl.BlockSpec((tm, w), lambda i: (i, 0))

    def whole(a):
        return pl.BlockSpec(a.shape, lambda i: (0, 0))

    return pl.pallas_call(
        _outproj_kernel,
        out_shape=jax.ShapeDtypeStruct((S, D_MODEL), F32),
        grid=(S // tm,),
        in_specs=[rows(D_MODEL), rows(y_conv.shape[1]), rows(y_nsa.shape[1]), rows(y_ret.shape[1]),
                  whole(wc), whole(wn), whole(wr), pl.BlockSpec((1, D_MODEL), lambda i: (0, 0))],
        out_specs=rows(D_MODEL),
        compiler_params=_cparams(("arbitrary",)),
        name="out_proj",
    )(x2, y_conv, y_nsa, y_ret, wc, wn, wr, gate_a)


def _router_kernel(x_ref, g_ref, sc_ref, sh_ref, rw_ref, rb_ref, h_ref, rt_ref, cnt_ref, carry_ref):
    i = pl.program_id(0)
    tm = x_ref.shape[0]

    @pl.when(i == 0)
    def _():
        carry_ref[...] = jnp.zeros_like(carry_ref)

    h = _modulated_norm(x_ref[...], g_ref[...], sc_ref[...], sh_ref[...])
    h_ref[...] = h
    logits = jnp.dot(h, rw_ref[...], precision=HI, preferred_element_type=F32) + rb_ref[...]
    lane = lax.broadcasted_iota(jnp.int32, (tm, LANES), 1)
    lane_f = lane.astype(F32)
    work = jnp.where(lane < N_EXPERTS, logits, -jnp.inf)
    onehot = jnp.zeros((tm, LANES), F32)
    vals, idxs = [], []
    for _ in range(TOP_K):
        m = jnp.max(work, axis=-1, keepdims=True)
        first = jnp.min(jnp.where(work == m, lane_f, float(LANES)), axis=-1, keepdims=True)
        hit = lane_f == first
        onehot = jnp.where(hit, 1.0, onehot)
        work = jnp.where(hit, -jnp.inf, work)
        vals.append(m)
        idxs.append(first)
    ex = [jnp.exp(v - vals[0]) for v in vals]
    den = ex[0] + ex[1] + ex[2] + ex[3]
    r = lax.broadcasted_iota(jnp.int32, (tm, tm), 0)
    c = lax.broadcasted_iota(jnp.int32, (tm, tm), 1)
    tri = jnp.where(c < r, 1.0, 0.0).astype(BF16)
    cum = jnp.dot(tri, onehot.astype(BF16), preferred_element_type=F32) + carry_ref[...]
    out = jnp.zeros((tm, LANES), F32)
    for kk in range(TOP_K):
        rank = jnp.sum(jnp.where(lane_f == idxs[kk], cum, 0.0), axis=-1, keepdims=True)
        out = jnp.where(lane == kk, idxs[kk], out)
        out = jnp.where(lane == TOP_K + kk, ex[kk] / den, out)
        out = jnp.where(lane == 2 * TOP_K + kk, rank, out)
    rt_ref[...] = out
    carry_ref[...] = carry_ref[...] + jnp.sum(onehot, axis=0, keepdims=True)
    cnt_ref[...] = carry_ref[...]


def _router(x2, g, scale, shift, router_w, router_b):
    T = x2.shape[0]
    tm = 256
    rw = jnp.pad(router_w, ((0, 0), (0, LANES - N_EXPERTS)))
    rb = jnp.pad(router_b, (0, LANES - N_EXPERTS)).reshape(1, LANES)
    row = pl.BlockSpec((1, D_MODEL), lambda i: (0, 0))
    return pl.pallas_call(
        _router_kernel,
        out_shape=(jax.ShapeDtypeStruct((T, D_MODEL), F32),
                   jax.ShapeDtypeStruct((T, LANES), F32),
                   jax.ShapeDtypeStruct((1, LANES), F32)),
        grid=(T // tm,),
        in_specs=[pl.BlockSpec((tm, D_MODEL), lambda i: (i, 0)), row, row, row,
                  pl.BlockSpec((D_MODEL, LANES), lambda i: (0, 0)),
                  pl.BlockSpec((1, LANES), lambda i: (0, 0))],
        out_specs=(pl.BlockSpec((tm, D_MODEL), lambda i: (i, 0)),
                   pl.BlockSpec((tm, LANES), lambda i: (i, 0)),
                   pl.BlockSpec((1, LANES), lambda i: (0, 0))),
        scratch_shapes=[pltpu.VMEM((1, LANES), F32)],
        compiler_params=_cparams(("arbitrary",)),
        name="moe_router",
    )(x2, g, scale, shift, rw, rb)


GATHER_UNROLL = 8


def _dispatch_kernel(tok_ref, nu_ref, h_hbm, o_ref, buf, sem):
    b = pl.program_id(0)
    nrow = o_ref.shape[0]
    n_used = nu_ref[0]

    def issue(blk):
        slot = blk % 2

        def body(c, carry):
            for u in range(GATHER_UNROLL):
                r = c * GATHER_UNROLL + u
                t = tok_ref[blk * nrow + r]
                pltpu.make_async_copy(h_hbm.at[pl.ds(t, 1)], buf.at[slot, pl.ds(r, 1)], sem.at[slot]).start()
            return carry

        lax.fori_loop(0, nrow // GATHER_UNROLL, body, 0)

    @pl.when((b == 0) & (n_used > 0))
    def _():
        issue(b)

    @pl.when(b + 1 < n_used)
    def _():
        issue(b + 1)

    @pl.when(b < n_used)
    def _():
        slot = b % 2
        pltpu.make_async_copy(h_hbm.at[pl.ds(0, nrow)], buf.at[slot], sem.at[slot]).wait()
        o_ref[...] = buf[slot].astype(o_ref.dtype)

    @pl.when(b >= n_used)
    def _():
        o_ref[...] = jnp.zeros_like(o_ref)


def _dispatch(h2, slot_tok, n_used):
    n_slots = slot_tok.shape[0]
    return pl.pallas_call(
        _dispatch_kernel,
        out_shape=jax.ShapeDtypeStruct((n_slots, D_MODEL), BF16),
        grid_spec=pltpu.PrefetchScalarGridSpec(
            num_scalar_prefetch=2,
            grid=(n_slots // MOE_SUB,),
            in_specs=[pl.BlockSpec(memory_space=pl.ANY)],
            out_specs=pl.BlockSpec((MOE_SUB, D_MODEL), lambda b, tok, nu: (b, 0)),
            scratch_shapes=[pltpu.VMEM((2, MOE_SUB, D_MODEL), F32), pltpu.SemaphoreType.DMA((2,))]),
        compiler_params=_cparams(("arbitrary",)),
        name="moe_dispatch",
    )(slot_tok, n_used, h2)


def _expert_kernel(ie_ref, ir_ref, ins_ref, xs_hbm, wgl_ref, wli_ref, bgl_ref, bli_ref, wdn_ref, bdn_ref,
                   ys_hbm, xbuf, acc, wg_s, wl_s, wd_s, sem_in, sem_out):
    w = pl.program_id(0)
    ct = pl.program_id(1)
    n_w = pl.num_programs(0)
    n_ct = pl.num_programs(1)
    nsub = ins_ref[w]

    def x_copy(item, sb):
        row = pl.multiple_of(ir_ref[item] + sb * MOE_SUB, MOE_SUB)
        slot = item % 2
        return pltpu.make_async_copy(xs_hbm.at[pl.ds(row, MOE_SUB)],
                                     xbuf.at[slot, pl.ds(sb * MOE_SUB, MOE_SUB)], sem_in.at[slot, sb])

    def y_copy(item, sb):
        row = pl.multiple_of(ir_ref[item] + sb * MOE_SUB, MOE_SUB)
        return pltpu.make_async_copy(acc.at[pl.ds(sb * MOE_SUB, MOE_SUB)],
                                     ys_hbm.at[pl.ds(row, MOE_SUB)], sem_out.at[sb])

    def for_subs(item, fn):
        n = ins_ref[item]
        for sb in range(MOE_SUBMAX):
            @pl.when(sb < n)
            def _():
                fn(item, sb)

    @pl.when(nsub > 0)
    def _():
        @pl.when((ct == 0) & (w == 0))
        def _():
            for_subs(w, lambda it, sb: x_copy(it, sb).start())

        @pl.when((ct == 1) & (w + 1 < n_w))
        def _():
            for_subs(jnp.minimum(w + 1, n_w - 1), lambda it, sb: x_copy(it, sb).start())

        @pl.when(ct == 0)
        def _():
            for_subs(w, lambda it, sb: x_copy(it, sb).wait())

        wg_s[...] = wgl_ref[0, 0].astype(BF16)
        wl_s[...] = wli_ref[0, 0].astype(BF16)
        wd_s[...] = wdn_ref[0, 0].astype(BF16)
        slot = w % 2

        @pl.when((ct == 0) & (w > 0))
        def _():
            for_subs(jnp.maximum(w - 1, 0), lambda it, sb: y_copy(it, sb).wait())

        def ffn_rows(r0, m_rows):
            x = xbuf[slot, pl.ds(r0, m_rows), :]
            glu = jnp.dot(x, wg_s[...], preferred_element_type=F32) + bgl_ref[0]
            lin = jnp.dot(x, wl_s[...], preferred_element_type=F32) + bli_ref[0]
            glu = jnp.minimum(glu, SWIGLU_LIMIT)
            lin = jnp.clip(lin, -SWIGLU_LIMIT, SWIGLU_LIMIT)
            act = glu * _sigmoid(SWIGLU_ALPHA * glu) * (lin + 1.0)
            y = jnp.dot(act.astype(BF16), wd_s[...], preferred_element_type=F32)

            @pl.when(ct == 0)
            def _():
                acc[pl.ds(r0, m_rows), :] = y + bdn_ref[0]

            @pl.when(ct > 0)
            def _():
                acc[pl.ds(r0, m_rows), :] += y

        def quad_body(qd, carry):
            ffn_rows(pl.multiple_of(qd * (4 * MOE_SUB), 4 * MOE_SUB), 4 * MOE_SUB)
            return carry

        lax.fori_loop(0, nsub // 4, quad_body, 0)

        @pl.when(nsub % 4 >= 2)
        def _():
            ffn_rows(pl.multiple_of((nsub // 4) * (4 * MOE_SUB), 2 * MOE_SUB), 2 * MOE_SUB)

        @pl.when(nsub % 2 == 1)
        def _():
            ffn_rows(pl.multiple_of((nsub - 1) * MOE_SUB, MOE_SUB), MOE_SUB)

        @pl.when(ct == n_ct - 1)
        def _():
            for_subs(w, lambda it, sb: y_copy(it, sb).start())
            last = (w == n_w - 1) | (ins_ref[jnp.minimum(w + 1, n_w - 1)] == 0)

            @pl.when(last)
            def _():
                for_subs(w, lambda it, sb: y_copy(it, sb).wait())


def _experts(xs, item_e, item_row0, item_nsub, w_gu, b_gu, w_dn, b_dn, layer):
    n_slots = xs.shape[0]
    W = item_e.shape[0]
    te = MOE_TE
    n_ct = D_EXPERT // te
    rows = MOE_SUBMAX * MOE_SUB

    def ct_eff(w, ct, ins):
        return jnp.where(ins[w] > 0, ct, n_ct - 1)

    return pl.pallas_call(
        _expert_kernel,
        out_shape=jax.ShapeDtypeStruct((n_slots, D_MODEL), F32),
        grid_spec=pltpu.PrefetchScalarGridSpec(
            num_scalar_prefetch=3,
            grid=(W, n_ct),
            in_specs=[pl.BlockSpec(memory_space=pl.ANY),
                      pl.BlockSpec((1, 1, D_MODEL, te),
                                   lambda w, ct, ie, ir, ins: (layer, ie[w], 0, ct_eff(w, ct, ins))),
                      pl.BlockSpec((1, 1, D_MODEL, te),
                                   lambda w, ct, ie, ir, ins: (layer, ie[w], 0, n_ct + ct_eff(w, ct, ins))),
                      pl.BlockSpec((1, 1, te), lambda w, ct, ie, ir, ins: (ie[w], 0, ct_eff(w, ct, ins))),
                      pl.BlockSpec((1, 1, te), lambda w, ct, ie, ir, ins: (ie[w], 0, n_ct + ct_eff(w, ct, ins))),
                      pl.BlockSpec((1, 1, te, D_MODEL),
                                   lambda w, ct, ie, ir, ins: (layer, ie[w], ct_eff(w, ct, ins), 0)),
                      pl.BlockSpec((1, 1, D_MODEL), lambda w, ct, ie, ir, ins: (ie[w], 0, 0))],
            out_specs=pl.BlockSpec(memory_space=pl.ANY),
            scratch_shapes=[pltpu.VMEM((2, rows, D_MODEL), BF16), pltpu.VMEM((rows, D_MODEL), F32),
                            pltpu.VMEM((D_MODEL, te), BF16), pltpu.VMEM((D_MODEL, te), BF16),
                            pltpu.VMEM((te, D_MODEL), BF16),
                            pltpu.SemaphoreType.DMA((2, MOE_SUBMAX)), pltpu.SemaphoreType.DMA((MOE_SUBMAX,))]),
        compiler_params=_cparams(("arbitrary", "arbitrary"), vmem=60 * 1024 * 1024),
        name="moe_experts",
    )(item_e, item_row0, item_nsub, xs, w_gu, w_gu, b_gu.reshape(N_EXPERTS, 1, -1), b_gu.reshape(N_EXPERTS, 1, -1),
      w_dn, b_dn.reshape(N_EXPERTS, 1, -1))


def _combine_kernel(dest_ref, x_ref, rt_ref, gf_ref, fg_ref, ys_hbm, o_ref, buf, sem, *, final_norm):
    i = pl.program_id(0)
    n = pl.num_programs(0)
    tm = x_ref.shape[0]

    def issue(tile):
        slot = tile % 2

        def body(c, carry):
            for u in range(GATHER_UNROLL // TOP_K):
                r = c * (GATHER_UNROLL // TOP_K) + u
                for kk in range(TOP_K):
                    d = dest_ref[(tile * tm + r) * TOP_K + kk]
                    pltpu.make_async_copy(ys_hbm.at[pl.ds(d, 1)], buf.at[slot, kk, pl.ds(r, 1)],
                                          sem.at[slot]).start()
            return carry

        lax.fori_loop(0, tm * TOP_K // GATHER_UNROLL, body, 0)

    @pl.when(i == 0)
    def _():
        issue(i)

    @pl.when(i + 1 < n)
    def _():
        issue(i + 1)

    slot = i % 2
    for kk in range(TOP_K):
        pltpu.make_async_copy(ys_hbm.at[pl.ds(0, tm)], buf.at[slot, kk], sem.at[slot]).wait()
    rt = rt_ref[...]
    lane = lax.broadcasted_iota(jnp.int32, rt.shape, 1)
    moe = jnp.zeros((tm, D_MODEL), F32)
    for kk in range(TOP_K):
        moe = moe + buf[slot, kk] * _lane_pick(rt, lane, TOP_K + kk)
    out = x_ref[...] + gf_ref[...] * moe
    if final_norm:
        ms = jnp.mean(out * out, axis=-1, keepdims=True)
        out = out * lax.rsqrt(ms + NORM_EPS) * fg_ref[...]
    o_ref[...] = out


def _combine(x2, route, gate_f, final_g, ys, dest_flat, final_norm):
    T = x2.shape[0]
    tm = 128
    row = pl.BlockSpec((1, D_MODEL), lambda i, d: (0, 0))
    return pl.pallas_call(
        functools.partial(_combine_kernel, final_norm=final_norm),
        out_shape=jax.ShapeDtypeStruct((T, D_MODEL), F32),
        grid_spec=pltpu.PrefetchScalarGridSpec(
            num_scalar_prefetch=1,
            grid=(T // tm,),
            in_specs=[pl.BlockSpec((tm, D_MODEL), lambda i, d: (i, 0)),
                      pl.BlockSpec((tm, LANES), lambda i, d: (i, 0)), row, row,
                      pl.BlockSpec(memory_space=pl.ANY)],
            out_specs=pl.BlockSpec((tm, D_MODEL), lambda i, d: (i, 0)),
            scratch_shapes=[pltpu.VMEM((2, TOP_K, tm, D_MODEL), F32), pltpu.SemaphoreType.DMA((2,))]),
        compiler_params=_cparams(("arbitrary",)),
        name="moe_combine",
    )(dest_flat, x2, route, gate_f, final_g, ys)


def _moe_plan(route, counts_f):
    T = route.shape[0]
    e = route[:, 0:TOP_K].astype(jnp.int32)
    rank = route[:, 2 * TOP_K:3 * TOP_K].astype(jnp.int32)
    counts = counts_f[0, :N_EXPERTS].astype(jnp.int32)
    nsub = (counts + MOE_SUB - 1) // MOE_SUB
    padded = nsub * MOE_SUB
    pend = jnp.cumsum(padded)
    pstart = pend - padded
    dest = pstart[e] + rank
    n_slots = (T * TOP_K + MOE_SUB - 1) // MOE_SUB * MOE_SUB + N_EXPERTS * MOE_SUB
    tok = jnp.repeat(jnp.arange(T, dtype=jnp.int32), TOP_K)
    slot_tok = jnp.zeros((n_slots,), jnp.int32).at[dest.reshape(-1)].set(tok)
    n_used = (pend[-1] // MOE_SUB).astype(jnp.int32).reshape(1)
    n_items_max = N_EXPERTS + (n_slots // MOE_SUB) // MOE_SUBMAX
    per_e = (nsub + MOE_SUBMAX - 1) // MOE_SUBMAX
    iend = jnp.cumsum(per_e)
    w = jnp.arange(n_items_max, dtype=jnp.int32)
    ew = jnp.minimum(jnp.searchsorted(iend, w, side='right'), N_EXPERTS - 1).astype(jnp.int32)
    local = w - (iend[ew] - per_e[ew])
    live = w < iend[-1]
    item_nsub = jnp.where(live, jnp.clip(nsub[ew] - local * MOE_SUBMAX, 0, MOE_SUBMAX), 0).astype(jnp.int32)
    item_row0 = jnp.where(live, pstart[ew] + local * (MOE_SUBMAX * MOE_SUB), 0).astype(jnp.int32)
    last_e = ew[jnp.maximum(iend[-1] - 1, 0)]
    item_e = jnp.where(live, ew, last_e).astype(jnp.int32)
    return dest.reshape(-1).astype(jnp.int32), slot_tok, n_used, item_e, item_row0, item_nsub


def _repack_w_in(w_in_l):
    parts, width = [], 0
    for name in _ORDER:
        off, size = _SRC[name]
        parts.append(w_in_l[:, off:off + size])
        width += size
        if size % LANES:
            parts.append(jnp.zeros((D_MODEL, LANES - size % LANES), w_in_l.dtype))
            width += LANES - size % LANES
    parts.append(jnp.zeros((D_MODEL, PROJ_W - width), w_in_l.dtype))
    return jnp.concatenate(parts, axis=1).astype(BF16)


def _cover_matrix(S):
    n_strip = S // CMP_STRIDE
    n_cmp = (S - CMP_BLOCK) // CMP_STRIDE + 1
    n = np.arange(n_strip)[:, None]
    j = np.arange(LANES)[None, :]
    start, end = n * CMP_STRIDE, n * CMP_STRIDE + CMP_BLOCK - 1
    cov = (start <= j * SEL_BLOCK + SEL_BLOCK - 1) & (end >= j * SEL_BLOCK) & (n < n_cmp) & (j < S // SEL_BLOCK)
    return jnp.asarray(cov.astype(np.float32))


def _strips(proj, cb):
    S = proj.shape[0]
    t = proj[:, cb * LANES:(cb + NSA_KV_HEADS) * LANES].reshape(S, NSA_KV_HEADS, HD)
    return t.transpose(1, 0, 2).reshape(NSA_KV_HEADS, S // CMP_STRIDE, CMP_STRIDE * HD)


def kernel(x, c, ada_w, ada_b, norm_mix_g, w_in, conv_dw_w, conv_dw_b, conv_ln_g, conv_ln_b, nsa_pe_k, nsa_pe_v, nsa_cmp_k_w1, nsa_cmp_k_w2, nsa_cmp_v_w1, nsa_cmp_v_w2, ret_gn_g, ret_gn_b, w_out, norm_ffn_g, router_w, router_b, moe_w_gate_up, moe_b_gate_up, moe_w_down, moe_b_down, final_norm_g):
    B, S, _ = x.shape
    assert B == 1 and c.shape[0] == 1
    x2 = x.reshape(S, D_MODEL)
    mod = _ada_mod(c, ada_w, ada_b).reshape(DEPTH, 6, 1, D_MODEL)
    tabs = _rotary_tables(S)
    cover = _cover_matrix(S)
    final_g = final_norm_g.reshape(1, D_MODEL)
    for l in range(DEPTH):
        shift_a, scale_a, gate_a, shift_f, scale_f, gate_f = [mod[l, i] for i in range(6)]
        proj = _in_proj(x2, norm_mix_g[l].reshape(1, -1), scale_a, shift_a, _repack_w_in(w_in[l]), tabs)
        y_conv = _conv_group(proj, conv_dw_w[l], conv_dw_b[l], conv_ln_g[l], conv_ln_b[l])
        kc = _compress(_strips(proj, CB_KC), nsa_pe_k[l], nsa_cmp_k_w1[l], nsa_cmp_k_w2[l])
        vc = _compress(_strips(proj, CB_VC), nsa_pe_v[l], nsa_cmp_v_w1[l], nsa_cmp_v_w2[l])
        oc, selbias = _cmp_attention(proj, kc, vc, cover)
        ksa, vsb, kwb, vwb = _kv_prep(proj)
        y_nsa = _nsa_attention(proj, selbias, oc, ksa, vsb, kwb, vwb)
        y_ret = _retention(proj, ret_gn_g[l], ret_gn_b[l])
        x2 = _out_proj(x2, y_conv, y_nsa, y_ret, w_out[l].astype(BF16), gate_a)
        h2, route, counts = _router(x2, norm_ffn_g[l].reshape(1, -1), scale_f, shift_f, router_w[l], router_b[l])
        dest, slot_tok, n_used, item_e, item_row0, item_nsub = _moe_plan(route, counts)
        xs = _dispatch(h2, slot_tok, n_used)
        ys = _experts(xs, item_e, item_row0, item_nsub, moe_w_gate_up, moe_b_gate_up[l], moe_w_down, moe_b_down[l], l)
        x2 = _combine(x2, route, gate_f, final_g, ys, dest, final_norm=(l == DEPTH - 1))
    return x2.reshape(B, S, D_MODEL)
```

```python
import functools
import math

import numpy as np
import jax
import jax.numpy as jnp
from jax import lax
from jax.experimental import pallas as pl
from jax.experimental.pallas import tpu as pltpu

F32 = jnp.float32
BF16 = jnp.bfloat16
HI = lax.Precision.HIGHEST

D_MODEL = 2048
DEPTH = 2
CONV_CH = 512
CONV_WIDTH = 31
HD = 128
NSA_HEADS = 8
NSA_KV_HEADS = 2
NSA_GROUP = 4
CMP_BLOCK = 32
CMP_STRIDE = 16
SEL_BLOCK = 64
SEL_TOPN = 16
WINDOW = 512
ROPE_THETA = 500000.0
ROPE_DIM = 32
RET_HEADS = 4
RET_CHUNK = 128
RET_THETA = 10000.0
N_EXPERTS = 32
TOP_K = 4
D_EXPERT = 2048
SWIGLU_LIMIT = 7.0
SWIGLU_ALPHA = 1.702
NORM_EPS = 1e-6
NEG = -1e30
BIG = 1e30
LOG2E = math.log2(math.e)

LANES = 128
VMEM_LIMIT = 56 * 1024 * 1024

CB_Q, CB_KC, CB_KS, CB_KW = 0, 8, 10, 12
CB_RQ, CB_RK = 14, 18
CB_VC, CB_CV, CB_CG, CB_VS, CB_VW, CB_GT, CB_RV, CB_RG = 22, 24, 28, 32, 34, 36, 37, 41
PROJ_BLOCKS = 46
PROJ_W = PROJ_BLOCKS * LANES
PROJ_TN = 256
NSA_ROT_TILES = CB_RQ * LANES // PROJ_TN
RET_ROT_TILES = CB_VC * LANES // PROJ_TN
_SRC = {'cv': (0, 512), 'cg': (512, 512), 'q': (1024, 1024), 'kc': (2048, 256), 'vc': (2304, 256),
        'ks': (2560, 256), 'vs': (2816, 256), 'kw': (3072, 256), 'vw': (3328, 256), 'gt': (3584, 24),
        'rq': (3608, 512), 'rk': (4120, 512), 'rv': (4632, 512), 'rg': (5144, 512)}
_ORDER = ['q', 'kc', 'ks', 'kw', 'rq', 'rk', 'vc', 'cv', 'cg', 'vs', 'vw', 'gt', 'rv', 'rg']

MOE_SUB = 256
MOE_SUBMAX = 5
MOE_TE = 512
MOE_NCT = D_EXPERT // MOE_TE


def _sigmoid(x):
    return 1.0 / (1.0 + jnp.exp(-x))


def _cparams(sem, vmem=VMEM_LIMIT):
    return pltpu.CompilerParams(dimension_semantics=sem, vmem_limit_bytes=vmem)


def _ada_kernel(c_ref, w_ref, b_ref, o_ref, sc_ref):
    @pl.when(pl.program_id(1) == 0)
    def _():
        cv = c_ref[...]
        sc_ref[...] = cv * _sigmoid(cv)

    tn = o_ref.shape[-1]

    def body(i, acc):
        k0 = pl.multiple_of(i * 64, 64)
        p = w_ref[0, pl.ds(k0, 64), :] * sc_ref[pl.ds(k0, 64), :]
        return acc + p.reshape(8, 8, tn).sum(axis=0)

    acc = lax.fori_loop(0, D_MODEL // 64, body, jnp.zeros((8, tn), F32))
    o_ref[0] = jnp.sum(acc, axis=0, keepdims=True) + b_ref[0]


def _ada_mod(c, ada_w, ada_b):
    tn = 1024
    n = 6 * D_MODEL
    return pl.pallas_call(
        _ada_kernel,
        out_shape=jax.ShapeDtypeStruct((DEPTH, 1, n), F32),
        grid=(DEPTH, n // tn),
        in_specs=[pl.BlockSpec((D_MODEL, 1), lambda l, j: (0, 0)),
                  pl.BlockSpec((1, D_MODEL, tn), lambda l, j: (l, 0, j)),
                  pl.BlockSpec((1, 1, tn), lambda l, j: (l, 0, j))],
        out_specs=pl.BlockSpec((1, 1, tn), lambda l, j: (l, 0, j)),
        scratch_shapes=[pltpu.VMEM((D_MODEL, 1), F32)],
        compiler_params=_cparams(("arbitrary", "arbitrary")),
        name="ada_mod",
    )(c.reshape(D_MODEL, 1), ada_w, ada_b.reshape(DEPTH, 1, n))


def _modulated_norm(x, g, scale, shift):
    ms = jnp.mean(x * x, axis=-1, keepdims=True)
    return x * lax.rsqrt(ms + NORM_EPS) * g * (1.0 + scale) + shift


def _inproj_kernel(x_ref, g_ref, sc_ref, sh_ref, w_ref, cn_ref, s1_ref, s2_ref, cr_ref, sr_ref,
                   o_ref, h_ref):
    j = pl.program_id(1)
    nsub = o_ref.shape[-1] // LANES

    @pl.when(j == 0)
    def _():
        h_ref[...] = _modulated_norm(x_ref[...], g_ref[...], sc_ref[...], sh_ref[...]).astype(BF16)

    def matmul():
        return jnp.dot(h_ref[...], w_ref[...], preferred_element_type=F32)

    @pl.when(j < NSA_ROT_TILES)
    def _():
        acc = matmul()
        for c in range(nsub):
            sub = acc[:, c * LANES:(c + 1) * LANES]
            o_ref[:, c * LANES:(c + 1) * LANES] = (
                sub * cn_ref[...] + pltpu.roll(sub, LANES - ROPE_DIM // 2, 1) * s1_ref[...]
                + pltpu.roll(sub, ROPE_DIM // 2, 1) * s2_ref[...])

    @pl.when((j >= NSA_ROT_TILES) & (j < RET_ROT_TILES))
    def _():
        acc = matmul()
        for c in range(nsub):
            sub = acc[:, c * LANES:(c + 1) * LANES]
            o_ref[:, c * LANES:(c + 1) * LANES] = sub * cr_ref[...] + pltpu.roll(sub, HD // 2, 1) * sr_ref[...]

    @pl.when(j >= RET_ROT_TILES)
    def _():
        o_ref[...] = matmul()


def _in_proj(x2, g, scale, shift, w_bf, tabs):
    S = x2.shape[0]
    tm, tn = 1024, PROJ_TN
    row = pl.BlockSpec((1, D_MODEL), lambda i, j: (0, 0))
    tab = pl.BlockSpec((tm, LANES), lambda i, j: (i, 0))
    return pl.pallas_call(
        _inproj_kernel,
        out_shape=jax.ShapeDtypeStruct((S, PROJ_W), F32),
        grid=(S // tm, PROJ_W // tn),
        in_specs=[pl.BlockSpec((tm, D_MODEL), lambda i, j: (i, 0)), row, row, row,
                  pl.BlockSpec((D_MODEL, tn), lambda i, j: (0, j)), tab, tab, tab, tab, tab],
        out_specs=pl.BlockSpec((tm, tn), lambda i, j: (i, j)),
        scratch_shapes=[pltpu.VMEM((tm, D_MODEL), BF16)],
        compiler_params=_cparams(("arbitrary", "arbitrary")),
        name="in_proj",
    )(x2, g, scale, shift, w_bf, *tabs)


def _rotary_tables(S):
    pos = jnp.arange(S, dtype=F32)[:, None]
    half = ROPE_DIM // 2
    inv = ROPE_THETA ** (-jnp.arange(half, dtype=F32) * 2.0 / ROPE_DIM)
    ang = pos * inv[None, :]
    cos, sin = jnp.cos(ang), jnp.sin(ang)
    ones = jnp.ones((S, LANES - ROPE_DIM), F32)
    zeros_r = jnp.zeros((S, LANES - ROPE_DIM), F32)
    zeros_h = jnp.zeros((S, half), F32)
    cn = jnp.concatenate([cos, cos, ones], axis=1)
    s1 = jnp.concatenate([-sin, zeros_h, zeros_r], axis=1)
    s2 = jnp.concatenate([zeros_h, sin, zeros_r], axis=1)
    halfr = HD // 2
    invr = RET_THETA ** (-jnp.arange(halfr, dtype=F32) * 2.0 / HD)
    angr = pos * invr[None, :]
    cosr, sinr = jnp.cos(angr), jnp.sin(angr)
    cr = jnp.concatenate([cosr, cosr], axis=1)
    sr = jnp.concatenate([-sinr, sinr], axis=1)
    return cn, s1, s2, cr, sr


def _conv_kernel(cv_ref, cg_ref, dw_ref, db_ref, lg_ref, lb_ref, o_ref, hb_ref):
    i = pl.program_id(0)
    ts = o_ref.shape[0]
    halo = 32

    @pl.when(i == 0)
    def _():
        hb_ref[0:halo, :] = jnp.zeros((halo, CONV_CH), F32)

    @pl.when(i > 0)
    def _():
        hb_ref[0:halo, :] = hb_ref[ts:ts + halo, :]

    hb_ref[halo:halo + ts, :] = cv_ref[...] * _sigmoid(cg_ref[...])
    acc = jnp.zeros((ts, CONV_CH), F32) + db_ref[...]
    for w in range(CONV_WIDTH):
        acc = acc + hb_ref[pl.ds(halo - (CONV_WIDTH - 1) + w, ts), :] * dw_ref[w:w + 1, :]
    mu = jnp.mean(acc, axis=-1, keepdims=True)
    var = jnp.mean(jnp.square(acc - mu), axis=-1, keepdims=True)
    y = (acc - mu) * lax.rsqrt(var + NORM_EPS) * lg_ref[...] + lb_ref[...]
    o_ref[...] = (y * _sigmoid(y)).astype(o_ref.dtype)


def _conv_group(proj, dw_w, dw_b, ln_g, ln_b):
    S = proj.shape[0]
    ts = 256
    vec = pl.BlockSpec((1, CONV_CH), lambda i: (0, 0))
    return pl.pallas_call(
        _conv_kernel,
        out_shape=jax.ShapeDtypeStruct((S, CONV_CH), BF16),
        grid=(S // ts,),
        in_specs=[pl.BlockSpec((ts, CONV_CH), lambda i: (i, CB_CV * LANES // CONV_CH)),
                  pl.BlockSpec((ts, CONV_CH), lambda i: (i, CB_CG * LANES // CONV_CH)),
                  pl.BlockSpec((CONV_WIDTH, CONV_CH), lambda i: (0, 0)), vec, vec, vec],
        out_specs=pl.BlockSpec((ts, CONV_CH), lambda i: (i, 0)),
        scratch_shapes=[pltpu.VMEM((ts + 32, CONV_CH), F32)],
        compiler_params=_cparams(("arbitrary",)),
        name="conv_group",
    )(proj, proj, dw_w, dw_b.reshape(1, -1), ln_g.reshape(1, -1), ln_b.reshape(1, -1))


def _compress_kernel(x_ref, pe_ref, w1_ref, w2_ref, o_ref):
    X = x_ref[0]
    nrow = X.shape[0]
    half = CMP_STRIDE * HD
    A = jnp.dot(X, w1_ref[0:half, :], precision=HI, preferred_element_type=F32)
    B = jnp.dot(X, w1_ref[half:2 * half, :], precision=HI, preferred_element_type=F32)
    pe8 = jnp.broadcast_to(pe_ref[...], (8, 2 * half))
    cst = jnp.dot(pe8, w1_ref[...], precision=HI, preferred_element_type=F32)[0:1]
    pre = A + pltpu.roll(B, nrow - 1, 0) + cst
    act = pre * _sigmoid(pre)
    out = jnp.dot(act, w2_ref[...], precision=HI, preferred_element_type=F32)
    rows = lax.broadcasted_iota(jnp.int32, out.shape, 0)
    o_ref[0] = jnp.where(rows < nrow - 1, out, 0.0)


def _compress(strips, pe, w1, w2):
    nh, nrow, width = strips.shape
    return pl.pallas_call(
        _compress_kernel,
        out_shape=jax.ShapeDtypeStruct((nh, nrow, HD), F32),
        grid=(nh,),
        in_specs=[pl.BlockSpec((1, nrow, width), lambda h: (h, 0, 0)),
                  pl.BlockSpec((1, width * 2), lambda h: (0, 0)),
                  pl.BlockSpec((width * 2, HD), lambda h: (0, 0)),
                  pl.BlockSpec((HD, HD), lambda h: (0, 0))],
        out_specs=pl.BlockSpec((1, nrow, HD), lambda h: (h, 0, 0)),
        compiler_params=_cparams(("arbitrary",)),
        name="nsa_compress",
    )(strips, pe.reshape(1, -1), w1, w2)


def _lane_pick(vals, lane, idx):
    return jnp.sum(jnp.where(lane == idx, vals, 0.0), axis=-1, keepdims=True)


def _cmp_kernel(q_ref, gt_ref, kc_ref, vc_ref, cov_ref, oc_ref, sb_ref, *, n_sel):
    k = pl.program_id(0)
    i = pl.program_id(1)
    tq = q_ref.shape[0]
    ncp = kc_ref.shape[1]
    scale = HD ** -0.5
    tpos = i * tq + lax.broadcasted_iota(jnp.int32, (tq, 1), 0)
    ncol = lax.broadcasted_iota(jnp.int32, (1, ncp), 1)
    mc = (ncol * CMP_STRIDE + (CMP_BLOCK - 1)) <= tpos
    anyv = tpos >= (CMP_BLOCK - 1)
    kc = kc_ref[0]
    kc_hi = kc.astype(BF16)
    kc_lo = (kc - kc_hi.astype(F32)).astype(BF16)
    kc_cat = jnp.concatenate([kc_hi, kc_lo, kc_hi], axis=1)
    cov = cov_ref[...].astype(BF16)
    vc = vc_ref[0].astype(BF16)
    lane = lax.broadcasted_iota(jnp.int32, (tq, LANES), 1)
    sig = _sigmoid(gt_ref[...])
    psum = jnp.zeros((tq, ncp), F32)
    for g in range(NSA_GROUP):
        qg = q_ref[:, g * HD:(g + 1) * HD] * scale
        q_hi = qg.astype(BF16)
        q_lo = (qg - q_hi.astype(F32)).astype(BF16)
        s = lax.dot_general(jnp.concatenate([q_hi, q_hi, q_lo], axis=1), kc_cat, (((1,), (1,)), ((), ())),
                            preferred_element_type=F32)
        s = jnp.where(mc, s, NEG)
        m = jnp.max(s, axis=-1, keepdims=True)
        e = jnp.where(mc, jnp.exp(s - m), 0.0)
        l = jnp.sum(e, axis=-1, keepdims=True)
        p = e / jnp.where(anyv, l, 1.0)
        o = jnp.dot(p.astype(BF16), vc, preferred_element_type=F32)
        gate = _lane_pick(sig, lane, k * NSA_GROUP + g)
        oc_ref[:, g * HD:(g + 1) * HD] = gate * o
        psum = psum + p
    p_hi = psum.astype(BF16)
    p_lo = (psum - p_hi.astype(F32)).astype(BF16)
    imp = (jnp.dot(p_hi, cov, preferred_element_type=F32)
           + jnp.dot(p_lo, cov, preferred_element_type=F32))
    valid = (lane * SEL_BLOCK <= tpos) & (lane < n_sel)
    forced = (lane == 0) | (lane == tpos // SEL_BLOCK)
    work = jnp.where(forced, BIG, jnp.where(valid, imp, NEG))
    lane_f = lane.astype(F32)
    chosen = jnp.zeros((tq, LANES), F32)
    for _ in range(min(SEL_TOPN, n_sel)):
        m = jnp.max(work, axis=-1, keepdims=True)
        first = jnp.min(jnp.where(work == m, lane_f, float(LANES)), axis=-1, keepdims=True)
        hit = lane_f == first
        chosen = jnp.where(hit, 1.0, chosen)
        work = jnp.where(hit, -jnp.inf, work)
    keep = (chosen > 0.5) & valid
    sb_ref[0] = jnp.where(keep, 0.0, NEG).astype(BF16)


def _cmp_attention(proj, kc, vc, cover):
    S = proj.shape[0]
    tq = 256
    ncp = kc.shape[1]
    return pl.pallas_call(
        functools.partial(_cmp_kernel, n_sel=S // SEL_BLOCK),
        out_shape=(jax.ShapeDtypeStruct((S, NSA_HEADS * HD), F32),
                   jax.ShapeDtypeStruct((NSA_KV_HEADS, S, LANES), BF16)),
        grid=(NSA_KV_HEADS, S // tq),
        in_specs=[pl.BlockSpec((tq, NSA_GROUP * HD), lambda k, i: (i, CB_Q // NSA_GROUP + k)),
                  pl.BlockSpec((tq, LANES), lambda k, i: (i, CB_GT)),
                  pl.BlockSpec((1, ncp, HD), lambda k, i: (k, 0, 0)),
                  pl.BlockSpec((1, ncp, HD), lambda k, i: (k, 0, 0)),
                  pl.BlockSpec((ncp, LANES), lambda k, i: (0, 0))],
        out_specs=(pl.BlockSpec((tq, NSA_GROUP * HD), lambda k, i: (i, k)),
                   pl.BlockSpec((1, tq, LANES), lambda k, i: (k, i, 0))),
        compiler_params=_cparams(("arbitrary", "arbitrary")),
        name="nsa_cmp_select",
    )(proj, proj, kc, vc, cover)


def _kvprep_kernel(ks_ref, vs_ref, kw_ref, vw_ref, ksa_ref, vsb_ref, kwb_ref, vwb_ref):
    i = pl.program_id(1)
    ts = ks_ref.shape[0]
    rows = i * ts + lax.broadcasted_iota(jnp.int32, (ts, LANES), 0)
    lane = lax.broadcasted_iota(jnp.int32, (ts, LANES), 1)
    ksa_ref[0, :, 0:HD] = ks_ref[...].astype(BF16)
    ksa_ref[0, :, HD:2 * HD] = jnp.where(lane == rows // SEL_BLOCK, 1.0, 0.0).astype(BF16)
    vsb_ref[0, 0] = vs_ref[...].T.astype(BF16)
    kwb_ref[0] = kw_ref[...].astype(BF16)
    for t in range(ts // LANES):
        vwb_ref[0, t] = vw_ref[t * LANES:(t + 1) * LANES, :].T.astype(BF16)


def _kv_prep(proj):
    S = proj.shape[0]
    ts = NSA_TK
    nw = ts // LANES

    def col(cb):
        return pl.BlockSpec((ts, HD), lambda k, i: (i, cb + k))

    return pl.pallas_call(
        _kvprep_kernel,
        out_shape=(jax.ShapeDtypeStruct((NSA_KV_HEADS, S, 2 * HD), BF16),
                   jax.ShapeDtypeStruct((NSA_KV_HEADS, S // ts, HD, ts), BF16),
                   jax.ShapeDtypeStruct((NSA_KV_HEADS, S, HD), BF16),
                   jax.ShapeDtypeStruct((NSA_KV_HEADS, S // LANES, HD, LANES), BF16)),
        grid=(NSA_KV_HEADS, S // ts),
        in_specs=[col(CB_KS), col(CB_VS), col(CB_KW), col(CB_VW)],
        out_specs=(pl.BlockSpec((1, ts, 2 * HD), lambda k, i: (k, i, 0)),
                   pl.BlockSpec((1, 1, HD, ts), lambda k, i: (k, i, 0, 0)),
                   pl.BlockSpec((1, ts, HD), lambda k, i: (k, i, 0)),
                   pl.BlockSpec((1, nw, HD, LANES), lambda k, i: (k, i, 0, 0))),
        compiler_params=_cparams(("arbitrary", "arbitrary")),
        name="nsa_kv_prep",
    )(proj, proj, proj, proj)


NSA_TQ = 256
NSA_TK = 512
NSA_WK = WINDOW + NSA_TQ
NSA_CH = 32
NSA_PW = 256


def _nsa_kernel(q_ref, sb_ref, oc_ref, gt_ref, ks_ref, vst_ref, kw_ref, vwt_ref, y_ref, *scratch):
    k = pl.program_id(0)
    i = pl.program_id(1)
    tq = NSA_TQ
    cols = NSA_GROUP * tq
    q0 = i * tq
    nt = (((1,), (1,)), ((), ()))
    npart = cols // NSA_PW
    s_refs, p_refs, acc_refs = scratch[0:npart], scratch[npart:2 * npart], scratch[2 * npart:3 * npart]
    gpp = NSA_PW // LANES
    nsubq = tq // LANES
    qs = jnp.concatenate([q_ref[sub * LANES:(sub + 1) * LANES, g * HD:(g + 1) * HD]
                          for sub in range(nsubq) for g in range(NSA_GROUP)], axis=0)
    qs = (qs * (HD ** -0.5 * LOG2E)).astype(BF16)
    sbs = jnp.concatenate([sb_ref[0, sub * LANES:(sub + 1) * LANES, :]
                           for sub in range(nsubq) for _ in range(NSA_GROUP)], axis=0)
    qaug = jnp.concatenate([qs, sbs], axis=1)
    lane_q = lax.broadcasted_iota(jnp.int32, (1, LANES), 1)

    def qpos(c):
        return q0 + (c // NSA_GROUP) * LANES + lane_q

    def softmax_tile(n_keys, k0, m_old, mask_fn, c_list):
        def scores(r0, c):
            s = s_refs[c // gpp][r0:r0 + NSA_CH, (c % gpp) * LANES:(c % gpp + 1) * LANES]
            if mask_fn is None:
                return s
            kpos = k0 + r0 + lax.broadcasted_iota(jnp.int32, (NSA_CH, 1), 0)
            return jnp.where(mask_fn(kpos, qpos(c)), s, NEG)

        def fold(x):
            return x.reshape(NSA_CH // 8, 8, LANES)

        m_new, sums = [], []
        for c in c_list:
            mx8 = jnp.max(fold(scores(0, c)), axis=0)
            for r0 in range(NSA_CH, n_keys, NSA_CH):
                mx8 = jnp.maximum(mx8, jnp.max(fold(scores(r0, c)), axis=0))
            mx = jnp.max(mx8, axis=0, keepdims=True)
            mc = mx if m_old is None else jnp.maximum(m_old[:, c * LANES:(c + 1) * LANES], mx)
            tot8 = jnp.zeros((8, LANES), F32)
            for r0 in range(0, n_keys, NSA_CH):
                p = jnp.exp2(scores(r0, c) - mc)
                tot8 = tot8 + jnp.sum(fold(p), axis=0)
                p_refs[c // gpp][r0:r0 + NSA_CH, (c % gpp) * LANES:(c % gpp + 1) * LANES] = p.astype(BF16)
            m_new.append(mc)
            sums.append(jnp.sum(tot8, axis=0, keepdims=True))
        return m_new, sums

    for a in acc_refs:
        a[...] = jnp.zeros((HD, NSA_PW), F32)

    def sel_tile(j, carry, causal):
        m_old, l_old = carry
        k0 = pl.multiple_of(j * NSA_TK, NSA_TK)
        kt = ks_ref[0, pl.ds(k0, NSA_TK), :]
        vt = vst_ref[0, j]
        mask_fn = (lambda kpos, qp: kpos <= qp) if causal else None
        halves = [slice(h * NSA_PW, (h + 1) * NSA_PW) for h in range(npart)]
        for h, hs in enumerate(halves):
            s_refs[h][0:NSA_TK, :] = lax.dot_general(kt, qaug[hs], nt, preferred_element_type=F32)
        m_out, l_out = [], []
        for h, hs in enumerate(halves):
            m_new, sums = softmax_tile(NSA_TK, k0, m_old, mask_fn, list(range(h * gpp, (h + 1) * gpp)))
            m_new, sums = jnp.concatenate(m_new, axis=1), jnp.concatenate(sums, axis=1)
            alpha = jnp.exp2(m_old[:, hs] - m_new)
            acc_refs[h][...] = acc_refs[h][...] * alpha + jnp.dot(vt, p_refs[h][0:NSA_TK, :],
                                                                  preferred_element_type=F32)
            m_out.append(m_new)
            l_out.append(alpha * l_old[:, hs] + sums)
        return jnp.concatenate(m_out, axis=1), jnp.concatenate(l_out, axis=1)

    n_full = (q0 + tq - 1) // NSA_TK
    init = (jnp.full((1, cols), NEG, F32), jnp.zeros((1, cols), F32))
    carry = lax.fori_loop(0, n_full, lambda j, c: sel_tile(j, c, False), init)
    _, l_s = sel_tile(n_full, carry, True)
    ot_s = jnp.concatenate([a[...] for a in acc_refs], axis=1) / l_s

    w0 = pl.multiple_of(jnp.maximum(q0 - WINDOW, 0), tq)
    kwt = kw_ref[0, pl.ds(w0, NSA_WK), :]
    for h in range(npart):
        s_refs[h][...] = lax.dot_general(kwt, qs[h * NSA_PW:(h + 1) * NSA_PW], nt, preferred_element_type=F32)
    _, l_w = softmax_tile(NSA_WK, w0, None, lambda kpos, qp: (kpos <= qp) & (kpos > qp - WINDOW),
                          list(range(cols // LANES)))
    l_w = jnp.concatenate(l_w, axis=1)
    ot_w = []
    for h in range(npart):
        o = jnp.zeros((HD, NSA_PW), F32)
        for t in range(NSA_WK // LANES):
            o = o + jnp.dot(vwt_ref[0, w0 // LANES + t], p_refs[h][t * LANES:(t + 1) * LANES, :],
                            preferred_element_type=F32)
        ot_w.append(o)
    ot_w = jnp.concatenate(ot_w, axis=1) / l_w

    lane = lax.broadcasted_iota(jnp.int32, (LANES, LANES), 1)
    for sub in range(nsubq):
        rs = slice(sub * LANES, (sub + 1) * LANES)
        sig = _sigmoid(gt_ref[rs, :])
        for g in range(NSA_GROUP):
            head = k * NSA_GROUP + g
            cs = slice((sub * NSA_GROUP + g) * LANES, (sub * NSA_GROUP + g + 1) * LANES)
            g_s = _lane_pick(sig, lane, NSA_HEADS + head)
            g_w = _lane_pick(sig, lane, 2 * NSA_HEADS + head)
            y = oc_ref[rs, g * HD:(g + 1) * HD] + g_s * ot_s[:, cs].T + g_w * ot_w[:, cs].T
            y_ref[rs, g * HD:(g + 1) * HD] = y.astype(y_ref.dtype)


def _nsa_attention(proj, selbias, oc, ksa, vst, kwb, vwt):
    S = proj.shape[0]
    tq = NSA_TQ
    cols = NSA_GROUP * tq
    npart = cols // NSA_PW
    assert S >= NSA_WK and tq % LANES == 0 and NSA_TK % tq == 0

    def full(a):
        return pl.BlockSpec((1,) + a.shape[1:], lambda k, i: (k,) + (0,) * (a.ndim - 1))

    return pl.pallas_call(
        _nsa_kernel,
        out_shape=jax.ShapeDtypeStruct((S, NSA_HEADS * HD), BF16),
        grid=(NSA_KV_HEADS, S // tq),
        in_specs=[pl.BlockSpec((tq, NSA_GROUP * HD), lambda k, i: (i, CB_Q // NSA_GROUP + k)),
                  pl.BlockSpec((1, tq, LANES), lambda k, i: (k, i, 0)),
                  pl.BlockSpec((tq, NSA_GROUP * HD), lambda k, i: (i, k)),
                  pl.BlockSpec((tq, LANES), lambda k, i: (i, CB_GT)),
                  full(ksa), full(vst), full(kwb), full(vwt)],
        out_specs=pl.BlockSpec((tq, NSA_GROUP * HD), lambda k, i: (i, k)),
        scratch_shapes=([pltpu.VMEM((NSA_WK, NSA_PW), F32)] * npart + [pltpu.VMEM((NSA_WK, NSA_PW), BF16)] * npart
                        + [pltpu.VMEM((HD, NSA_PW), F32)] * npart),
        compiler_params=_cparams(("arbitrary", "arbitrary")),
        name="nsa_sel_win",
    )(proj, selbias, oc, proj, ksa, vst, kwb, vwt)


def _ret_kernel(q_ref, k_ref, v_ref, g_ref, dm_ref, ze_ref, xi_ref, dc_ref, gg_ref, gb_ref, o_ref, st_ref):
    n = pl.program_id(1)
    C = RET_CHUNK
    nt = (((1,), (1,)), ((), ()))
    tn = (((0,), (0,)), ((), ()))

    @pl.when(n == 0)
    def _():
        st_ref[...] = jnp.zeros_like(st_ref)

    dmat = dm_ref[0]
    zeta = ze_ref[0]
    xi = xi_ref[0]
    decay = dc_ref[0]
    for c in range(q_ref.shape[0] // C):
        sl = slice(c * C, (c + 1) * C)
        q = q_ref[sl, :]
        kk = k_ref[sl, :] * (HD ** -0.5)
        v = v_ref[sl, :]
        qb, kb, vb = q.astype(BF16), kk.astype(BF16), v.astype(BF16)
        inner = lax.dot_general(qb, kb, nt, preferred_element_type=F32) * dmat
        o = jnp.dot(inner.astype(BF16), vb, preferred_element_type=F32)
        state = st_ref[...]
        o = o + jnp.dot(qb, state.astype(BF16), preferred_element_type=F32) * xi
        kv = lax.dot_general((kk * zeta).astype(BF16), vb, tn, preferred_element_type=F32)
        st_ref[...] = decay * state + kv
        mu = jnp.mean(o, axis=-1, keepdims=True)
        var = jnp.mean(jnp.square(o - mu), axis=-1, keepdims=True)
        y = (o - mu) * lax.rsqrt(var + NORM_EPS) * gg_ref[...] + gb_ref[...]
        gt = g_ref[sl, :]
        o_ref[sl, :] = (y * (gt * _sigmoid(gt))).astype(o_ref.dtype)


def _retention(proj, gn_g, gn_b):
    S = proj.shape[0]
    C = RET_CHUNK
    ts = 1024 if S % 1024 == 0 else C
    H = RET_HEADS
    log_g = jnp.log(1.0 - 2.0 ** (-5.0 - jnp.arange(H, dtype=F32)))
    i = jnp.arange(C, dtype=F32)
    diff = i[:, None] - i[None, :]
    dmat = jnp.where(diff >= 0, jnp.exp(log_g[:, None, None] * jnp.maximum(diff, 0.0)), 0.0)
    zeta = jnp.exp(log_g[:, None] * (C - 1.0 - i))[:, :, None]
    xi = jnp.exp(log_g[:, None] * (i + 1.0))[:, :, None]
    decay = jnp.broadcast_to(jnp.exp(log_g * C)[:, None, None], (H, 1, HD))

    def col(cb):
        return pl.BlockSpec((ts, HD), lambda h, n: (n, cb + h))

    return pl.pallas_call(
        _ret_kernel,
        out_shape=jax.ShapeDtypeStruct((S, H * HD), BF16),
        grid=(H, S // ts),
        in_specs=[col(CB_RQ), col(CB_RK), col(CB_RV), col(CB_RG),
                  pl.BlockSpec((1, C, C), lambda h, n: (h, 0, 0)),
                  pl.BlockSpec((1, C, 1), lambda h, n: (h, 0, 0)),
                  pl.BlockSpec((1, C, 1), lambda h, n: (h, 0, 0)),
                  pl.BlockSpec((1, 1, HD), lambda h, n: (h, 0, 0)),
                  pl.BlockSpec((1, HD), lambda h, n: (0, h)),
                  pl.BlockSpec((1, HD), lambda h, n: (0, h))],
        out_specs=pl.BlockSpec((ts, HD), lambda h, n: (n, h)),
        scratch_shapes=[pltpu.VMEM((HD, HD), F32)],
        compiler_params=_cparams(("arbitrary", "arbitrary")),
        name="retention",
    )(proj, proj, proj, proj, dmat, zeta, xi, decay, gn_g.reshape(1, -1), gn_b.reshape(1, -1))


def _outproj_kernel(x_ref, yc_ref, yn_ref, yr_ref, wc_ref, wn_ref, wr_ref, ga_ref, o_ref):
    y = jnp.dot(yc_ref[...], wc_ref[...], preferred_element_type=F32)
    y = y + jnp.dot(yn_ref[...], wn_ref[...], preferred_element_type=F32)
    y = y + jnp.dot(yr_ref[...], wr_ref[...], preferred_element_type=F32)
    o_ref[...] = x_ref[...] + ga_ref[...] * y


def _out_proj(x2, y_conv, y_nsa, y_ret, w_out_bf, gate_a):
    S = x2.shape[0]
    tm = 512
    wc, wn, wr = w_out_bf[:CONV_CH], w_out_bf[CONV_CH:CONV_CH + NSA_HEADS * HD], w_out_bf[CONV_CH + NSA_HEADS * HD:]

    def rows(w):
        return pl.BlockSpec((tm, w), lambda i: (i, 0))

    def whole(a):
        return pl.BlockSpec(a.shape, lambda i: (0, 0))

    return pl.pallas_call(
        _outproj_kernel,
        out_shape=jax.ShapeDtypeStruct((S, D_MODEL), F32),
        grid=(S // tm,),
        in_specs=[rows(D_MODEL), rows(y_conv.shape[1]), rows(y_nsa.shape[1]), rows(y_ret.shape[1]),
                  whole(wc), whole(wn), whole(wr), pl.BlockSpec((1, D_MODEL), lambda i: (0, 0))],
        out_specs=rows(D_MODEL),
        compiler_params=_cparams(("arbitrary",)),
        name="out_proj",
    )(x2, y_conv, y_nsa, y_ret, wc, wn, wr, gate_a)


def _router_kernel(x_ref, g_ref, sc_ref, sh_ref, rw_ref, rb_ref, h_ref, rt_ref, cnt_ref, carry_ref):
    i = pl.program_id(0)
    tm = x_ref.shape[0]

    @pl.when(i == 0)
    def _():
        carry_ref[...] = jnp.zeros_like(carry_ref)

    h = _modulated_norm(x_ref[...], g_ref[...], sc_ref[...], sh_ref[...])
    half = D_MODEL // 2
    hi = lax.bitcast_convert_type(h[:, :half].astype(BF16).astype(F32), jnp.uint32)
    lo = lax.bitcast_convert_type(h[:, half:].astype(BF16).astype(F32), jnp.uint32)
    h_ref[...] = hi | (lo >> 16)
    logits = jnp.dot(h, rw_ref[...], precision=HI, preferred_element_type=F32) + rb_ref[...]
    lane = lax.broadcasted_iota(jnp.int32, (tm, LANES), 1)
    lane_f = lane.astype(F32)
    work = jnp.where(lane < N_EXPERTS, logits, -jnp.inf)
    onehot = jnp.zeros((tm, LANES), F32)
    vals, idxs = [], []
    for _ in range(TOP_K):
        m = jnp.max(work, axis=-1, keepdims=True)
        first = jnp.min(jnp.where(work == m, lane_f, float(LANES)), axis=-1, keepdims=True)
        hit = lane_f == first
        onehot = jnp.where(hit, 1.0, onehot)
        work = jnp.where(hit, -jnp.inf, work)
        vals.append(m)
        idxs.append(first)
    ex = [jnp.exp(v - vals[0]) for v in vals]
    den = ex[0] + ex[1] + ex[2] + ex[3]
    r = lax.broadcasted_iota(jnp.int32, (tm, tm), 0)
    c = lax.broadcasted_iota(jnp.int32, (tm, tm), 1)
    tri = jnp.where(c < r, 1.0, 0.0).astype(BF16)
    cum = jnp.dot(tri, onehot.astype(BF16), preferred_element_type=F32) + carry_ref[...]
    out = jnp.zeros((tm, LANES), F32)
    for kk in range(TOP_K):
        rank = jnp.sum(jnp.where(lane_f == idxs[kk], cum, 0.0), axis=-1, keepdims=True)
        out = jnp.where(lane == kk, idxs[kk], out)
        out = jnp.where(lane == TOP_K + kk, ex[kk] / den, out)
        out = jnp.where(lane == 2 * TOP_K + kk, rank, out)
    rt_ref[...] = out
    carry_ref[...] = carry_ref[...] + jnp.sum(onehot, axis=0, keepdims=True)
    cnt_ref[...] = carry_ref[...]


def _router(x2, g, scale, shift, router_w, router_b):
    T = x2.shape[0]
    tm = 256
    rw = jnp.pad(router_w, ((0, 0), (0, LANES - N_EXPERTS)))
    rb = jnp.pad(router_b, (0, LANES - N_EXPERTS)).reshape(1, LANES)
    row = pl.BlockSpec((1, D_MODEL), lambda i: (0, 0))
    return pl.pallas_call(
        _router_kernel,
        out_shape=(jax.ShapeDtypeStruct((T, D_MODEL // 2), jnp.uint32),
                   jax.ShapeDtypeStruct((T, LANES), F32),
                   jax.ShapeDtypeStruct((1, LANES), F32)),
        grid=(T // tm,),
        in_specs=[pl.BlockSpec((tm, D_MODEL), lambda i: (i, 0)), row, row, row,
                  pl.BlockSpec((D_MODEL, LANES), lambda i: (0, 0)),
                  pl.BlockSpec((1, LANES), lambda i: (0, 0))],
        out_specs=(pl.BlockSpec((tm, D_MODEL // 2), lambda i: (i, 0)),
                   pl.BlockSpec((tm, LANES), lambda i: (i, 0)),
                   pl.BlockSpec((1, LANES), lambda i: (0, 0))),
        scratch_shapes=[pltpu.VMEM((1, LANES), F32)],
        compiler_params=_cparams(("arbitrary",)),
        name="moe_router",
    )(x2, g, scale, shift, rw, rb)


GATHER_UNROLL = 8


def _expert_kernel(ie_ref, ir_ref, ins_ref, tok_ref, hp_hbm, wgl_ref, wli_ref, bgl_ref, bli_ref, wdn_ref, bdn_ref,
                   ys_hbm, stage, xbf, acc, wg_s, wl_s, wd_s, sem_g, sem_out):
    w = pl.program_id(0)
    ct = pl.program_id(1)
    n_w = pl.num_programs(0)
    n_ct = pl.num_programs(1)
    nsub = ins_ref[w]
    rows = stage.shape[0]
    n_slots = tok_ref.shape[0]
    per_step = rows // MOE_NCT
    half = D_MODEL // 2

    def gather_row(item, r):
        t = tok_ref[jnp.minimum(ir_ref[item] + r, n_slots - 1)]
        pltpu.make_async_copy(hp_hbm.at[pl.ds(t, 1)], stage.at[pl.ds(r, 1)], sem_g).start()

    def y_copy(item, sb):
        row = pl.multiple_of(ir_ref[item] + sb * MOE_SUB, MOE_SUB)
        return pltpu.make_async_copy(acc.at[pl.ds(sb * MOE_SUB, MOE_SUB)],
                                     ys_hbm.at[pl.ds(row, MOE_SUB)], sem_out.at[sb])

    def for_subs(item, fn):
        n = ins_ref[item]
        for sb in range(MOE_SUBMAX):
            @pl.when(sb < n)
            def _():
                fn(item, sb)

    @pl.when(nsub > 0)
    def _():
        @pl.when((ct == 0) & (w == 0))
        def _():
            def body(c, carry):
                for u in range(GATHER_UNROLL):
                    gather_row(w, c * GATHER_UNROLL + u)
                return carry

            lax.fori_loop(0, rows // GATHER_UNROLL, body, 0)

        @pl.when(ct == 0)
        def _():
            for sb in range(MOE_SUBMAX):
                pltpu.make_async_copy(hp_hbm.at[pl.ds(0, MOE_SUB)], stage.at[pl.ds(sb * MOE_SUB, MOE_SUB)],
                                      sem_g).wait()
            for sb in range(MOE_SUBMAX):
                wv = stage[sb * MOE_SUB:(sb + 1) * MOE_SUB, :]
                xbf[sb * MOE_SUB:(sb + 1) * MOE_SUB, 0:half] = lax.bitcast_convert_type(
                    wv & jnp.uint32(0xFFFF0000), F32).astype(BF16)
                xbf[sb * MOE_SUB:(sb + 1) * MOE_SUB, half:D_MODEL] = lax.bitcast_convert_type(
                    wv << 16, F32).astype(BF16)

        nxt = jnp.minimum(w + 1, n_w - 1)

        @pl.when((w + 1 < n_w) & (ins_ref[nxt] > 0))
        def _():
            for u in range(per_step):
                gather_row(nxt, ct * per_step + u)
            wg_s[...] = wgl_ref[0, 0].astype(BF16)
            wl_s[...] = wli_ref[0, 0].astype(BF16)
            wd_s[...] = wdn_ref[0, 0].astype(BF16)

        @pl.when(jnp.logical_not((w + 1 < n_w) & (ins_ref[nxt] > 0)))
        def _():
            wg_s[...] = wgl_ref[0, 0].astype(BF16)
            wl_s[...] = wli_ref[0, 0].astype(BF16)
            wd_s[...] = wdn_ref[0, 0].astype(BF16)

        @pl.when((ct == 0) & (w > 0))
        def _():
            for_subs(jnp.maximum(w - 1, 0), lambda it, sb: y_copy(it, sb).wait())

        def ffn_rows(r0, m_rows):
            x = xbf[pl.ds(r0, m_rows), :]
            glu = jnp.dot(x, wg_s[...], preferred_element_type=F32) + bgl_ref[0]
            lin = jnp.dot(x, wl_s[...], preferred_element_type=F32) + bli_ref[0]
            glu = jnp.minimum(glu, SWIGLU_LIMIT)
            lin = jnp.clip(lin, -SWIGLU_LIMIT, SWIGLU_LIMIT)
            act = glu * _sigmoid(SWIGLU_ALPHA * glu) * (lin + 1.0)
            y = jnp.dot(act.astype(BF16), wd_s[...], preferred_element_type=F32)

            @pl.when(ct == 0)
            def _():
                acc[pl.ds(r0, m_rows), :] = y + bdn_ref[0]

            @pl.when(ct > 0)
            def _():
                acc[pl.ds(r0, m_rows), :] += y

        def quad_body(qd, carry):
            ffn_rows(pl.multiple_of(qd * (4 * MOE_SUB), 4 * MOE_SUB), 4 * MOE_SUB)
            return carry

        lax.fori_loop(0, nsub // 4, quad_body, 0)

        @pl.when(nsub % 4 >= 2)
        def _():
            ffn_rows(pl.multiple_of((nsub // 4) * (4 * MOE_SUB), 2 * MOE_SUB), 2 * MOE_SUB)

        @pl.when(nsub % 2 == 1)
        def _():
            ffn_rows(pl.multiple_of((nsub - 1) * MOE_SUB, MOE_SUB), MOE_SUB)

        @pl.when(ct == n_ct - 1)
        def _():
            for_subs(w, lambda it, sb: y_copy(it, sb).start())
            last = (w == n_w - 1) | (ins_ref[nxt] == 0)

            @pl.when(last)
            def _():
                for_subs(w, lambda it, sb: y_copy(it, sb).wait())


def _experts(hp, slot_tok, item_e, item_row0, item_nsub, w_gu, b_gu, w_dn, b_dn, layer):
    n_slots = slot_tok.shape[0]
    W = item_e.shape[0]
    te = MOE_TE
    n_ct = MOE_NCT
    rows = MOE_SUBMAX * MOE_SUB
    assert rows % n_ct == 0

    def ct_eff(w, ct, ins):
        return jnp.where(ins[w] > 0, ct, n_ct - 1)

    return pl.pallas_call(
        _expert_kernel,
        out_shape=jax.ShapeDtypeStruct((n_slots, D_MODEL), F32),
        grid_spec=pltpu.PrefetchScalarGridSpec(
            num_scalar_prefetch=4,
            grid=(W, n_ct),
            in_specs=[pl.BlockSpec(memory_space=pl.ANY),
                      pl.BlockSpec((1, 1, D_MODEL, te),
                                   lambda w, ct, ie, ir, ins, tok: (layer, ie[w], 0, ct_eff(w, ct, ins))),
                      pl.BlockSpec((1, 1, D_MODEL, te),
                                   lambda w, ct, ie, ir, ins, tok: (layer, ie[w], 0, n_ct + ct_eff(w, ct, ins))),
                      pl.BlockSpec((1, 1, te), lambda w, ct, ie, ir, ins, tok: (ie[w], 0, ct_eff(w, ct, ins))),
                      pl.BlockSpec((1, 1, te), lambda w, ct, ie, ir, ins, tok: (ie[w], 0, n_ct + ct_eff(w, ct, ins))),
                      pl.BlockSpec((1, 1, te, D_MODEL),
                                   lambda w, ct, ie, ir, ins, tok: (layer, ie[w], ct_eff(w, ct, ins), 0)),
                      pl.BlockSpec((1, 1, D_MODEL), lambda w, ct, ie, ir, ins, tok: (ie[w], 0, 0))],
            out_specs=pl.BlockSpec(memory_space=pl.ANY),
            scratch_shapes=[pltpu.VMEM((rows, D_MODEL // 2), jnp.uint32), pltpu.VMEM((rows, D_MODEL), BF16),
                            pltpu.VMEM((rows, D_MODEL), F32),
                            pltpu.VMEM((D_MODEL, te), BF16), pltpu.VMEM((D_MODEL, te), BF16),
                            pltpu.VMEM((te, D_MODEL), BF16),
                            pltpu.SemaphoreType.DMA(()), pltpu.SemaphoreType.DMA((MOE_SUBMAX,))]),
        compiler_params=_cparams(("arbitrary", "arbitrary"), vmem=60 * 1024 * 1024),
        name="moe_experts",
    )(item_e, item_row0, item_nsub, slot_tok, hp, w_gu, w_gu, b_gu.reshape(N_EXPERTS, 1, -1),
      b_gu.reshape(N_EXPERTS, 1, -1), w_dn, b_dn.reshape(N_EXPERTS, 1, -1))


def _combine_kernel(dest_ref, x_ref, rt_ref, gf_ref, fg_ref, ys_hbm, o_ref, buf, sem, *, final_norm):
    i = pl.program_id(0)
    n = pl.num_programs(0)
    tm = x_ref.shape[0]

    def issue(tile):
        slot = tile % 2

        def body(c, carry):
            for u in range(GATHER_UNROLL // TOP_K):
                r = c * (GATHER_UNROLL // TOP_K) + u
                for kk in range(TOP_K):
                    d = dest_ref[(tile * tm + r) * TOP_K + kk]
                    pltpu.make_async_copy(ys_hbm.at[pl.ds(d, 1)], buf.at[slot, kk, pl.ds(r, 1)],
                                          sem.at[slot]).start()
            return carry

        lax.fori_loop(0, tm * TOP_K // GATHER_UNROLL, body, 0)

    @pl.when(i == 0)
    def _():
        issue(i)

    @pl.when(i + 1 < n)
    def _():
        issue(i + 1)

    slot = i % 2
    for kk in range(TOP_K):
        pltpu.make_async_copy(ys_hbm.at[pl.ds(0, tm)], buf.at[slot, kk], sem.at[slot]).wait()
    rt = rt_ref[...]
    lane = lax.broadcasted_iota(jnp.int32, rt.shape, 1)
    moe = jnp.zeros((tm, D_MODEL), F32)
    for kk in range(TOP_K):
        moe = moe + buf[slot, kk] * _lane_pick(rt, lane, TOP_K + kk)
    out = x_ref[...] + gf_ref[...] * moe
    if final_norm:
        ms = jnp.mean(out * out, axis=-1, keepdims=True)
        out = out * lax.rsqrt(ms + NORM_EPS) * fg_ref[...]
    o_ref[...] = out


def _combine(x2, route, gate_f, final_g, ys, dest_flat, final_norm):
    T = x2.shape[0]
    tm = 128
    row = pl.BlockSpec((1, D_MODEL), lambda i, d: (0, 0))
    return pl.pallas_call(
        functools.partial(_combine_kernel, final_norm=final_norm),
        out_shape=jax.ShapeDtypeStruct((T, D_MODEL), F32),
        grid_spec=pltpu.PrefetchScalarGridSpec(
            num_scalar_prefetch=1,
            grid=(T // tm,),
            in_specs=[pl.BlockSpec((tm, D_MODEL), lambda i, d: (i, 0)),
                      pl.BlockSpec((tm, LANES), lambda i, d: (i, 0)), row, row,
                      pl.BlockSpec(memory_space=pl.ANY)],
            out_specs=pl.BlockSpec((tm, D_MODEL), lambda i, d: (i, 0)),
            scratch_shapes=[pltpu.VMEM((2, TOP_K, tm, D_MODEL), F32), pltpu.SemaphoreType.DMA((2,))]),
        compiler_params=_cparams(("arbitrary",)),
        name="moe_combine",
    )(dest_flat, x2, route, gate_f, final_g, ys)


def _moe_plan(route, counts_f):
    T = route.shape[0]
    e = route[:, 0:TOP_K].astype(jnp.int32)
    rank = route[:, 2 * TOP_K:3 * TOP_K].astype(jnp.int32)
    counts = counts_f[0, :N_EXPERTS].astype(jnp.int32)
    nsub = (counts + MOE_SUB - 1) // MOE_SUB
    padded = nsub * MOE_SUB
    pend = jnp.cumsum(padded)
    pstart = pend - padded
    dest = pstart[e] + rank
    n_slots = (T * TOP_K + MOE_SUB - 1) // MOE_SUB * MOE_SUB + N_EXPERTS * MOE_SUB
    tok = jnp.repeat(jnp.arange(T, dtype=jnp.int32), TOP_K)
    slot_tok = jnp.zeros((n_slots,), jnp.int32).at[dest.reshape(-1)].set(tok)
    n_used = (pend[-1] // MOE_SUB).astype(jnp.int32).reshape(1)
    n_items_max = N_EXPERTS + (n_slots // MOE_SUB) // MOE_SUBMAX
    per_e = (nsub + MOE_SUBMAX - 1) // MOE_SUBMAX
    iend = jnp.cumsum(per_e)
    w = jnp.arange(n_items_max, dtype=jnp.int32)
    ew = jnp.minimum(jnp.searchsorted(iend, w, side='right'), N_EXPERTS - 1).astype(jnp.int32)
    local = w - (iend[ew] - per_e[ew])
    live = w < iend[-1]
    item_nsub = jnp.where(live, jnp.clip(nsub[ew] - local * MOE_SUBMAX, 0, MOE_SUBMAX), 0).astype(jnp.int32)
    item_row0 = jnp.where(live, pstart[ew] + local * (MOE_SUBMAX * MOE_SUB), 0).astype(jnp.int32)
    last_e = ew[jnp.maximum(iend[-1] - 1, 0)]
    item_e = jnp.where(live, ew, last_e).astype(jnp.int32)
    return dest.reshape(-1).astype(jnp.int32), slot_tok, n_used, item_e, item_row0, item_nsub


def _repack_w_in(w_in_l):
    parts, width = [], 0
    for name in _ORDER:
        off, size = _SRC[name]
        parts.append(w_in_l[:, off:off + size])
        width += size
        if size % LANES:
            parts.append(jnp.zeros((D_MODEL, LANES - size % LANES), w_in_l.dtype))
            width += LANES - size % LANES
    parts.append(jnp.zeros((D_MODEL, PROJ_W - width), w_in_l.dtype))
    return jnp.concatenate(parts, axis=1).astype(BF16)


def _cover_matrix(S):
    n_strip = S // CMP_STRIDE
    n_cmp = (S - CMP_BLOCK) // CMP_STRIDE + 1
    n = np.arange(n_strip)[:, None]
    j = np.arange(LANES)[None, :]
    start, end = n * CMP_STRIDE, n * CMP_STRIDE + CMP_BLOCK - 1
    cov = (start <= j * SEL_BLOCK + SEL_BLOCK - 1) & (end >= j * SEL_BLOCK) & (n < n_cmp) & (j < S // SEL_BLOCK)
    return jnp.asarray(cov.astype(np.float32))


def _strips(proj, cb):
    S = proj.shape[0]
    t = proj[:, cb * LANES:(cb + NSA_KV_HEADS) * LANES].reshape(S, NSA_KV_HEADS, HD)
    return t.transpose(1, 0, 2).reshape(NSA_KV_HEADS, S // CMP_STRIDE, CMP_STRIDE * HD)


def kernel(x, c, ada_w, ada_b, norm_mix_g, w_in, conv_dw_w, conv_dw_b, conv_ln_g, conv_ln_b, nsa_pe_k, nsa_pe_v, nsa_cmp_k_w1, nsa_cmp_k_w2, nsa_cmp_v_w1, nsa_cmp_v_w2, ret_gn_g, ret_gn_b, w_out, norm_ffn_g, router_w, router_b, moe_w_gate_up, moe_b_gate_up, moe_w_down, moe_b_down, final_norm_g):
    B, S, _ = x.shape
    assert B == 1 and c.shape[0] == 1
    x2 = x.reshape(S, D_MODEL)
    mod = _ada_mod(c, ada_w, ada_b).reshape(DEPTH, 6, 1, D_MODEL)
    tabs = _rotary_tables(S)
    cover = _cover_matrix(S)
    final_g = final_norm_g.reshape(1, D_MODEL)
    for l in range(DEPTH):
        shift_a, scale_a, gate_a, shift_f, scale_f, gate_f = [mod[l, i] for i in range(6)]
        proj = _in_proj(x2, norm_mix_g[l].reshape(1, -1), scale_a, shift_a, _repack_w_in(w_in[l]), tabs)
        y_conv = _conv_group(proj, conv_dw_w[l], conv_dw_b[l], conv_ln_g[l], conv_ln_b[l])
        kc = _compress(_strips(proj, CB_KC), nsa_pe_k[l], nsa_cmp_k_w1[l], nsa_cmp_k_w2[l])
        vc = _compress(_strips(proj, CB_VC), nsa_pe_v[l], nsa_cmp_v_w1[l], nsa_cmp_v_w2[l])
        oc, selbias = _cmp_attention(proj, kc, vc, cover)
        ksa, vsb, kwb, vwb = _kv_prep(proj)
        y_nsa = _nsa_attention(proj, selbias, oc, ksa, vsb, kwb, vwb)
        y_ret = _retention(proj, ret_gn_g[l], ret_gn_b[l])
        x2 = _out_proj(x2, y_conv, y_nsa, y_ret, w_out[l].astype(BF16), gate_a)
        h2, route, counts = _router(x2, norm_ffn_g[l].reshape(1, -1), scale_f, shift_f, router_w[l], router_b[l])
        dest, slot_tok, n_used, item_e, item_row0, item_nsub = _moe_plan(route, counts)
        ys = _experts(h2, slot_tok, item_e, item_row0, item_nsub, moe_w_gate_up, moe_b_gate_up[l], moe_w_down,
                      moe_b_down[l], l)
        x2 = _combine(x2, route, gate_f, final_g, ys, dest, final_norm=(l == DEPTH - 1))
    return x2.reshape(B, S, D_MODEL)
```

```python
import functools
import math

import numpy as np
import jax
import jax.numpy as jnp
from jax import lax
from jax.experimental import pallas as pl
from jax.experimental.pallas import tpu as pltpu

F32 = jnp.float32
BF16 = jnp.bfloat16
HI = lax.Precision.HIGHEST

D_MODEL = 2048
DEPTH = 2
CONV_CH = 512
CONV_WIDTH = 31
HD = 128
NSA_HEADS = 8
NSA_KV_HEADS = 2
NSA_GROUP = 4
CMP_BLOCK = 32
CMP_STRIDE = 16
SEL_BLOCK = 64
SEL_TOPN = 16
WINDOW = 512
ROPE_THETA = 500000.0
ROPE_DIM = 32
RET_HEADS = 4
RET_CHUNK = 128
RET_THETA = 10000.0
N_EXPERTS = 32
TOP_K = 4
D_EXPERT = 2048
SWIGLU_LIMIT = 7.0
SWIGLU_ALPHA = 1.702
NORM_EPS = 1e-6
NEG = -1e30
BIG = 1e30
LOG2E = math.log2(math.e)

LANES = 128
VMEM_LIMIT = 56 * 1024 * 1024

CB_Q, CB_KC, CB_KS, CB_KW = 0, 8, 10, 12
CB_RQ, CB_RK = 14, 18
CB_VC, CB_CV, CB_CG, CB_VS, CB_VW, CB_GT, CB_RV, CB_RG = 22, 24, 28, 32, 34, 36, 37, 41
PROJ_BLOCKS = 46
PROJ_W = PROJ_BLOCKS * LANES
PROJ_TN = 256
NSA_ROT_TILES = CB_RQ * LANES // PROJ_TN
RET_ROT_TILES = CB_VC * LANES // PROJ_TN
_SRC = {'cv': (0, 512), 'cg': (512, 512), 'q': (1024, 1024), 'kc': (2048, 256), 'vc': (2304, 256),
        'ks': (2560, 256), 'vs': (2816, 256), 'kw': (3072, 256), 'vw': (3328, 256), 'gt': (3584, 24),
        'rq': (3608, 512), 'rk': (4120, 512), 'rv': (4632, 512), 'rg': (5144, 512)}
_ORDER = ['q', 'kc', 'ks', 'kw', 'rq', 'rk', 'vc', 'cv', 'cg', 'vs', 'vw', 'gt', 'rv', 'rg']

MOE_SUB = 256
MOE_SUBMAX = 5
MOE_TE = 512
MOE_NCT = D_EXPERT // MOE_TE


def _sigmoid(x):
    return 1.0 / (1.0 + jnp.exp(-x))


def _cparams(sem, vmem=VMEM_LIMIT):
    return pltpu.CompilerParams(dimension_semantics=sem, vmem_limit_bytes=vmem)


def _ada_kernel(c_ref, w_ref, b_ref, o_ref, sc_ref):
    @pl.when(pl.program_id(1) == 0)
    def _():
        cv = c_ref[...]
        sc_ref[...] = cv * _sigmoid(cv)

    tn = o_ref.shape[-1]

    def body(i, acc):
        k0 = pl.multiple_of(i * 64, 64)
        p = w_ref[0, pl.ds(k0, 64), :] * sc_ref[pl.ds(k0, 64), :]
        return acc + p.reshape(8, 8, tn).sum(axis=0)

    acc = lax.fori_loop(0, D_MODEL // 64, body, jnp.zeros((8, tn), F32))
    o_ref[0] = jnp.sum(acc, axis=0, keepdims=True) + b_ref[0]


def _ada_mod(c, ada_w, ada_b):
    tn = 1024
    n = 6 * D_MODEL
    return pl.pallas_call(
        _ada_kernel,
        out_shape=jax.ShapeDtypeStruct((DEPTH, 1, n), F32),
        grid=(DEPTH, n // tn),
        in_specs=[pl.BlockSpec((D_MODEL, 1), lambda l, j: (0, 0)),
                  pl.BlockSpec((1, D_MODEL, tn), lambda l, j: (l, 0, j)),
                  pl.BlockSpec((1, 1, tn), lambda l, j: (l, 0, j))],
        out_specs=pl.BlockSpec((1, 1, tn), lambda l, j: (l, 0, j)),
        scratch_shapes=[pltpu.VMEM((D_MODEL, 1), F32)],
        compiler_params=_cparams(("arbitrary", "arbitrary")),
        name="ada_mod",
    )(c.reshape(D_MODEL, 1), ada_w, ada_b.reshape(DEPTH, 1, n))


def _modulated_norm(x, g, scale, shift):
    ms = jnp.mean(x * x, axis=-1, keepdims=True)
    return x * lax.rsqrt(ms + NORM_EPS) * g * (1.0 + scale) + shift


def _inproj_kernel(x_ref, g_ref, sc_ref, sh_ref, w_ref, cn_ref, s1_ref, s2_ref, cr_ref, sr_ref,
                   o_ref, h_ref):
    j = pl.program_id(1)
    nsub = o_ref.shape[-1] // LANES

    @pl.when(j == 0)
    def _():
        h_ref[...] = _modulated_norm(x_ref[...], g_ref[...], sc_ref[...], sh_ref[...]).astype(BF16)

    def matmul():
        return jnp.dot(h_ref[...], w_ref[...], preferred_element_type=F32)

    @pl.when(j < NSA_ROT_TILES)
    def _():
        acc = matmul()
        for c in range(nsub):
            sub = acc[:, c * LANES:(c + 1) * LANES]
            o_ref[:, c * LANES:(c + 1) * LANES] = (
                sub * cn_ref[...] + pltpu.roll(sub, LANES - ROPE_DIM // 2, 1) * s1_ref[...]
                + pltpu.roll(sub, ROPE_DIM // 2, 1) * s2_ref[...])

    @pl.when((j >= NSA_ROT_TILES) & (j < RET_ROT_TILES))
    def _():
        acc = matmul()
        for c in range(nsub):
            sub = acc[:, c * LANES:(c + 1) * LANES]
            o_ref[:, c * LANES:(c + 1) * LANES] = sub * cr_ref[...] + pltpu.roll(sub, HD // 2, 1) * sr_ref[...]

    @pl.when(j >= RET_ROT_TILES)
    def _():
        o_ref[...] = matmul()


def _in_proj(x2, g, scale, shift, w_bf, tabs):
    S = x2.shape[0]
    tm, tn = 1024, PROJ_TN
    row = pl.BlockSpec((1, D_MODEL), lambda i, j: (0, 0))
    tab = pl.BlockSpec((tm, LANES), lambda i, j: (i, 0))
    return pl.pallas_call(
        _inproj_kernel,
        out_shape=jax.ShapeDtypeStruct((S, PROJ_W), F32),
        grid=(S // tm, PROJ_W // tn),
        in_specs=[pl.BlockSpec((tm, D_MODEL), lambda i, j: (i, 0)), row, row, row,
                  pl.BlockSpec((D_MODEL, tn), lambda i, j: (0, j)), tab, tab, tab, tab, tab],
        out_specs=pl.BlockSpec((tm, tn), lambda i, j: (i, j)),
        scratch_shapes=[pltpu.VMEM((tm, D_MODEL), BF16)],
        compiler_params=_cparams(("arbitrary", "arbitrary")),
        name="in_proj",
    )(x2, g, scale, shift, w_bf, *tabs)


def _rotary_tables(S):
    pos = jnp.arange(S, dtype=F32)[:, None]
    half = ROPE_DIM // 2
    inv = ROPE_THETA ** (-jnp.arange(half, dtype=F32) * 2.0 / ROPE_DIM)
    ang = pos * inv[None, :]
    cos, sin = jnp.cos(ang), jnp.sin(ang)
    ones = jnp.ones((S, LANES - ROPE_DIM), F32)
    zeros_r = jnp.zeros((S, LANES - ROPE_DIM), F32)
    zeros_h = jnp.zeros((S, half), F32)
    cn = jnp.concatenate([cos, cos, ones], axis=1)
    s1 = jnp.concatenate([-sin, zeros_h, zeros_r], axis=1)
    s2 = jnp.concatenate([zeros_h, sin, zeros_r], axis=1)
    halfr = HD // 2
    invr = RET_THETA ** (-jnp.arange(halfr, dtype=F32) * 2.0 / HD)
    angr = pos * invr[None, :]
    cosr, sinr = jnp.cos(angr), jnp.sin(angr)
    cr = jnp.concatenate([cosr, cosr], axis=1)
    sr = jnp.concatenate([-sinr, sinr], axis=1)
    return cn, s1, s2, cr, sr


def _conv_kernel(cv_ref, cg_ref, dw_ref, db_ref, lg_ref, lb_ref, o_ref, hb_ref, xs_ref):
    i = pl.program_id(0)
    ts = o_ref.shape[0]
    halo = 32

    @pl.when(i == 0)
    def _():
        hb_ref[0:halo, :] = jnp.zeros((halo, CONV_CH), F32)

    @pl.when(i > 0)
    def _():
        hb_ref[0:halo, :] = hb_ref[ts:ts + halo, :]

    hb_ref[halo:halo + ts, :] = cv_ref[...] * _sigmoid(cg_ref[...])
    acc = jnp.zeros((ts, CONV_CH), F32) + db_ref[...]
    first = halo - (CONV_WIDTH - 1)
    for sh in range(8):
        offs = [first + w for w in range(CONV_WIDTH) if (first + w) % 8 == sh]
        span = max(offs) - sh + ts
        xs_ref[sh, 0:span, :] = hb_ref[sh:sh + span, :]
        for off in offs:
            acc = acc + xs_ref[sh, off - sh:off - sh + ts, :] * dw_ref[off - first:off - first + 1, :]
    mu = jnp.mean(acc, axis=-1, keepdims=True)
    var = jnp.mean(jnp.square(acc - mu), axis=-1, keepdims=True)
    y = (acc - mu) * lax.rsqrt(var + NORM_EPS) * lg_ref[...] + lb_ref[...]
    o_ref[...] = (y * _sigmoid(y)).astype(o_ref.dtype)


def _conv_group(proj, dw_w, dw_b, ln_g, ln_b):
    S = proj.shape[0]
    ts = 256
    vec = pl.BlockSpec((1, CONV_CH), lambda i: (0, 0))
    return pl.pallas_call(
        _conv_kernel,
        out_shape=jax.ShapeDtypeStruct((S, CONV_CH), BF16),
        grid=(S // ts,),
        in_specs=[pl.BlockSpec((ts, CONV_CH), lambda i: (i, CB_CV * LANES // CONV_CH)),
                  pl.BlockSpec((ts, CONV_CH), lambda i: (i, CB_CG * LANES // CONV_CH)),
                  pl.BlockSpec((CONV_WIDTH, CONV_CH), lambda i: (0, 0)), vec, vec, vec],
        out_specs=pl.BlockSpec((ts, CONV_CH), lambda i: (i, 0)),
        scratch_shapes=[pltpu.VMEM((ts + 32, CONV_CH), F32), pltpu.VMEM((8, ts + 32, CONV_CH), F32)],
        compiler_params=_cparams(("arbitrary",)),
        name="conv_group",
    )(proj, proj, dw_w, dw_b.reshape(1, -1), ln_g.reshape(1, -1), ln_b.reshape(1, -1))


def _compress_kernel(x_ref, pe_ref, w1_ref, w2_ref, o_ref):
    X = x_ref[0]
    nrow = X.shape[0]
    half = CMP_STRIDE * HD
    A = jnp.dot(X, w1_ref[0:half, :], precision=HI, preferred_element_type=F32)
    B = jnp.dot(X, w1_ref[half:2 * half, :], precision=HI, preferred_element_type=F32)
    pe8 = jnp.broadcast_to(pe_ref[...], (8, 2 * half))
    cst = jnp.dot(pe8, w1_ref[...], precision=HI, preferred_element_type=F32)[0:1]
    pre = A + pltpu.roll(B, nrow - 1, 0) + cst
    act = pre * _sigmoid(pre)
    out = jnp.dot(act, w2_ref[...], precision=HI, preferred_element_type=F32)
    rows = lax.broadcasted_iota(jnp.int32, out.shape, 0)
    o_ref[0] = jnp.where(rows < nrow - 1, out, 0.0)


def _compress(strips, pe, w1, w2):
    nh, nrow, width = strips.shape
    return pl.pallas_call(
        _compress_kernel,
        out_shape=jax.ShapeDtypeStruct((nh, nrow, HD), F32),
        grid=(nh,),
        in_specs=[pl.BlockSpec((1, nrow, width), lambda h: (h, 0, 0)),
                  pl.BlockSpec((1, width * 2), lambda h: (0, 0)),
                  pl.BlockSpec((width * 2, HD), lambda h: (0, 0)),
                  pl.BlockSpec((HD, HD), lambda h: (0, 0))],
        out_specs=pl.BlockSpec((1, nrow, HD), lambda h: (h, 0, 0)),
        compiler_params=_cparams(("arbitrary",)),
        name="nsa_compress",
    )(strips, pe.reshape(1, -1), w1, w2)


def _lane_pick(vals, lane, idx):
    return jnp.sum(jnp.where(lane == idx, vals, 0.0), axis=-1, keepdims=True)


def _cmp_kernel(q_ref, gt_ref, kc_ref, vc_ref, covt_ref, oc_ref, sb_ref, *, n_sel):
    k = pl.program_id(0)
    i = pl.program_id(1)
    tq = q_ref.shape[0]
    ncp = kc_ref.shape[1]
    scale = HD ** -0.5
    nt = (((1,), (1,)), ((), ()))
    tpos = i * tq + lax.broadcasted_iota(jnp.int32, (1, tq), 1)
    nrow = lax.broadcasted_iota(jnp.int32, (ncp, 1), 0)
    mc = (nrow * CMP_STRIDE + (CMP_BLOCK - 1)) <= tpos
    anyv = tpos >= (CMP_BLOCK - 1)
    kc = kc_ref[0]
    kc_hi = kc.astype(BF16)
    kc_lo = (kc - kc_hi.astype(F32)).astype(BF16)
    kc_cat = jnp.concatenate([kc_hi, kc_lo, kc_hi], axis=1)
    covt = covt_ref[...].astype(BF16)
    vct = vc_ref[0].T.astype(BF16)
    lane = lax.broadcasted_iota(jnp.int32, (tq, LANES), 1)
    sig = _sigmoid(gt_ref[...])
    psum = jnp.zeros((ncp, tq), F32)
    for g in range(NSA_GROUP):
        qg = q_ref[:, g * HD:(g + 1) * HD] * scale
        q_hi = qg.astype(BF16)
        q_lo = (qg - q_hi.astype(F32)).astype(BF16)
        s = lax.dot_general(kc_cat, jnp.concatenate([q_hi, q_hi, q_lo], axis=1), nt,
                            preferred_element_type=F32)
        s = jnp.where(mc, s, NEG)
        m = jnp.max(s, axis=0, keepdims=True)
        e = jnp.where(mc, jnp.exp(s - m), 0.0)
        l = jnp.sum(e, axis=0, keepdims=True)
        p = e / jnp.where(anyv, l, 1.0)
        ot = jnp.dot(vct, p.astype(BF16), preferred_element_type=F32)
        gate = _lane_pick(sig, lane, k * NSA_GROUP + g)
        oc_ref[:, g * HD:(g + 1) * HD] = gate * ot.T
        psum = psum + p
    p_hi = psum.astype(BF16)
    p_lo = (psum - p_hi.astype(F32)).astype(BF16)
    imp = (jnp.dot(covt, p_hi, preferred_element_type=F32)
           + jnp.dot(covt, p_lo, preferred_element_type=F32))
    blk = lax.broadcasted_iota(jnp.int32, (LANES, tq), 0)
    valid = (blk * SEL_BLOCK <= tpos) & (blk < n_sel)
    forced = (blk == 0) | (blk == tpos // SEL_BLOCK)
    work = jnp.where(forced, BIG, jnp.where(valid, imp, NEG))
    blk_f = blk.astype(F32)
    chosen = jnp.zeros((LANES, tq), F32)
    for _ in range(min(SEL_TOPN, n_sel)):
        m = jnp.max(work, axis=0, keepdims=True)
        first = jnp.min(jnp.where(work == m, blk_f, float(LANES)), axis=0, keepdims=True)
        hit = blk_f == first
        chosen = jnp.where(hit, 1.0, chosen)
        work = jnp.where(hit, -jnp.inf, work)
    keep = (chosen > 0.5) & valid
    sb_ref[0] = jnp.where(keep, 0.0, NEG).T.astype(BF16)


def _cmp_attention(proj, kc, vc, cover):
    S = proj.shape[0]
    tq = 256
    ncp = kc.shape[1]
    return pl.pallas_call(
        functools.partial(_cmp_kernel, n_sel=S // SEL_BLOCK),
        out_shape=(jax.ShapeDtypeStruct((S, NSA_HEADS * HD), F32),
                   jax.ShapeDtypeStruct((NSA_KV_HEADS, S, LANES), BF16)),
        grid=(NSA_KV_HEADS, S // tq),
        in_specs=[pl.BlockSpec((tq, NSA_GROUP * HD), lambda k, i: (i, CB_Q // NSA_GROUP + k)),
                  pl.BlockSpec((tq, LANES), lambda k, i: (i, CB_GT)),
                  pl.BlockSpec((1, ncp, HD), lambda k, i: (k, 0, 0)),
                  pl.BlockSpec((1, ncp, HD), lambda k, i: (k, 0, 0)),
                  pl.BlockSpec((LANES, ncp), lambda k, i: (0, 0))],
        out_specs=(pl.BlockSpec((tq, NSA_GROUP * HD), lambda k, i: (i, k)),
                   pl.BlockSpec((1, tq, LANES), lambda k, i: (k, i, 0))),
        compiler_params=_cparams(("arbitrary", "arbitrary")),
        name="nsa_cmp_select",
    )(proj, proj, kc, vc, cover)


def _kvprep_kernel(ks_ref, vs_ref, kw_ref, vw_ref, ksa_ref, vsb_ref, kwb_ref, vwb_ref):
    i = pl.program_id(1)
    ts = ks_ref.shape[0]
    rows = i * ts + lax.broadcasted_iota(jnp.int32, (ts, LANES), 0)
    lane = lax.broadcasted_iota(jnp.int32, (ts, LANES), 1)
    ksa_ref[0, :, 0:HD] = ks_ref[...].astype(BF16)
    ksa_ref[0, :, HD:2 * HD] = jnp.where(lane == rows // SEL_BLOCK, 1.0, 0.0).astype(BF16)
    vsb_ref[0, 0] = vs_ref[...].T.astype(BF16)
    kwb_ref[0] = kw_ref[...].astype(BF16)
    for t in range(ts // LANES):
        vwb_ref[0, t] = vw_ref[t * LANES:(t + 1) * LANES, :].T.astype(BF16)


def _kv_prep(proj):
    S = proj.shape[0]
    ts = NSA_TK
    nw = ts // LANES

    def col(cb):
        return pl.BlockSpec((ts, HD), lambda k, i: (i, cb + k))

    return pl.pallas_call(
        _kvprep_kernel,
        out_shape=(jax.ShapeDtypeStruct((NSA_KV_HEADS, S, 2 * HD), BF16),
                   jax.ShapeDtypeStruct((NSA_KV_HEADS, S // ts, HD, ts), BF16),
                   jax.ShapeDtypeStruct((NSA_KV_HEADS, S, HD), BF16),
                   jax.ShapeDtypeStruct((NSA_KV_HEADS, S // LANES, HD, LANES), BF16)),
        grid=(NSA_KV_HEADS, S // ts),
        in_specs=[col(CB_KS), col(CB_VS), col(CB_KW), col(CB_VW)],
        out_specs=(pl.BlockSpec((1, ts, 2 * HD), lambda k, i: (k, i, 0)),
                   pl.BlockSpec((1, 1, HD, ts), lambda k, i: (k, i, 0, 0)),
                   pl.BlockSpec((1, ts, HD), lambda k, i: (k, i, 0)),
                   pl.BlockSpec((1, nw, HD, LANES), lambda k, i: (k, i, 0, 0))),
        compiler_params=_cparams(("arbitrary", "arbitrary")),
        name="nsa_kv_prep",
    )(proj, proj, proj, proj)


NSA_TQ = 256
NSA_TK = 512
NSA_WK = WINDOW + NSA_TQ
NSA_CH = 32
NSA_PW = 256


def _nsa_kernel(q_ref, sb_ref, oc_ref, gt_ref, ks_ref, vst_ref, kw_ref, vwt_ref, y_ref, *scratch):
    k = pl.program_id(0)
    i = pl.program_id(1)
    tq = NSA_TQ
    cols = NSA_GROUP * tq
    q0 = i * tq
    nt = (((1,), (1,)), ((), ()))
    npart = cols // NSA_PW
    s_refs, p_refs, acc_refs = scratch[0:npart], scratch[npart:2 * npart], scratch[2 * npart:3 * npart]
    gpp = NSA_PW // LANES
    nsubq = tq // LANES
    qs = jnp.concatenate([q_ref[sub * LANES:(sub + 1) * LANES, g * HD:(g + 1) * HD]
                          for sub in range(nsubq) for g in range(NSA_GROUP)], axis=0)
    qs = (qs * (HD ** -0.5 * LOG2E)).astype(BF16)
    sbs = jnp.concatenate([sb_ref[0, sub * LANES:(sub + 1) * LANES, :]
                           for sub in range(nsubq) for _ in range(NSA_GROUP)], axis=0)
    qaug = jnp.concatenate([qs, sbs], axis=1)
    lane_q = lax.broadcasted_iota(jnp.int32, (1, LANES), 1)

    def qpos(c):
        return q0 + (c // NSA_GROUP) * LANES + lane_q

    def softmax_tile(n_keys, k0, m_old, mask_fn, c_list):
        def scores(r0, c):
            s = s_refs[c // gpp][r0:r0 + NSA_CH, (c % gpp) * LANES:(c % gpp + 1) * LANES]
            if mask_fn is None:
                return s
            kpos = k0 + r0 + lax.broadcasted_iota(jnp.int32, (NSA_CH, 1), 0)
            return jnp.where(mask_fn(kpos, qpos(c)), s, NEG)

        def fold(x):
            return x.reshape(NSA_CH // 8, 8, LANES)

        m_new, sums = [], []
        for c in c_list:
            mx8 = jnp.max(fold(scores(0, c)), axis=0)
            for r0 in range(NSA_CH, n_keys, NSA_CH):
                mx8 = jnp.maximum(mx8, jnp.max(fold(scores(r0, c)), axis=0))
            mx = jnp.max(mx8, axis=0, keepdims=True)
            mc = mx if m_old is None else jnp.maximum(m_old[:, c * LANES:(c + 1) * LANES], mx)
            tot8 = jnp.zeros((8, LANES), F32)
            for r0 in range(0, n_keys, NSA_CH):
                p = jnp.exp2(scores(r0, c) - mc)
                tot8 = tot8 + jnp.sum(fold(p), axis=0)
                p_refs[c // gpp][r0:r0 + NSA_CH, (c % gpp) * LANES:(c % gpp + 1) * LANES] = p.astype(BF16)
            m_new.append(mc)
            sums.append(jnp.sum(tot8, axis=0, keepdims=True))
        return m_new, sums

    for a in acc_refs:
        a[...] = jnp.zeros((HD, NSA_PW), F32)

    def sel_tile(j, carry, causal):
        m_old, l_old = carry
        k0 = pl.multiple_of(j * NSA_TK, NSA_TK)
        kt = ks_ref[0, pl.ds(k0, NSA_TK), :]
        vt = vst_ref[0, j]
        mask_fn = (lambda kpos, qp: kpos <= qp) if causal else None
        halves = [slice(h * NSA_PW, (h + 1) * NSA_PW) for h in range(npart)]
        for h, hs in enumerate(halves):
            s_refs[h][0:NSA_TK, :] = lax.dot_general(kt, qaug[hs], nt, preferred_element_type=F32)
        m_out, l_out = [], []
        for h, hs in enumerate(halves):
            m_new, sums = softmax_tile(NSA_TK, k0, m_old, mask_fn, list(range(h * gpp, (h + 1) * gpp)))
            m_new, sums = jnp.concatenate(m_new, axis=1), jnp.concatenate(sums, axis=1)
            alpha = jnp.exp2(m_old[:, hs] - m_new)
            acc_refs[h][...] = acc_refs[h][...] * alpha + jnp.dot(vt, p_refs[h][0:NSA_TK, :],
                                                                  preferred_element_type=F32)
            m_out.append(m_new)
            l_out.append(alpha * l_old[:, hs] + sums)
        return jnp.concatenate(m_out, axis=1), jnp.concatenate(l_out, axis=1)

    n_full = (q0 + tq - 1) // NSA_TK
    init = (jnp.full((1, cols), NEG, F32), jnp.zeros((1, cols), F32))
    carry = lax.fori_loop(0, n_full, lambda j, c: sel_tile(j, c, False), init)
    _, l_s = sel_tile(n_full, carry, True)
    ot_s = jnp.concatenate([a[...] for a in acc_refs], axis=1) / l_s

    w0 = pl.multiple_of(jnp.maximum(q0 - WINDOW, 0), tq)
    kwt = kw_ref[0, pl.ds(w0, NSA_WK), :]
    for h in range(npart):
        s_refs[h][...] = lax.dot_general(kwt, qs[h * NSA_PW:(h + 1) * NSA_PW], nt, preferred_element_type=F32)
    _, l_w = softmax_tile(NSA_WK, w0, None, lambda kpos, qp: (kpos <= qp) & (kpos > qp - WINDOW),
                          list(range(cols // LANES)))
    l_w = jnp.concatenate(l_w, axis=1)
    ot_w = []
    for h in range(npart):
        o = jnp.zeros((HD, NSA_PW), F32)
        for t in range(NSA_WK // LANES):
            o = o + jnp.dot(vwt_ref[0, w0 // LANES + t], p_refs[h][t * LANES:(t + 1) * LANES, :],
                            preferred_element_type=F32)
        ot_w.append(o)
    ot_w = jnp.concatenate(ot_w, axis=1) / l_w

    lane = lax.broadcasted_iota(jnp.int32, (LANES, LANES), 1)
    for sub in range(nsubq):
        rs = slice(sub * LANES, (sub + 1) * LANES)
        sig = _sigmoid(gt_ref[rs, :])
        for g in range(NSA_GROUP):
            head = k * NSA_GROUP + g
            cs = slice((sub * NSA_GROUP + g) * LANES, (sub * NSA_GROUP + g + 1) * LANES)
            g_s = _lane_pick(sig, lane, NSA_HEADS + head)
            g_w = _lane_pick(sig, lane, 2 * NSA_HEADS + head)
            y = oc_ref[rs, g * HD:(g + 1) * HD] + g_s * ot_s[:, cs].T + g_w * ot_w[:, cs].T
            y_ref[rs, g * HD:(g + 1) * HD] = y.astype(y_ref.dtype)


def _nsa_attention(proj, selbias, oc, ksa, vst, kwb, vwt):
    S = proj.shape[0]
    tq = NSA_TQ
    cols = NSA_GROUP * tq
    npart = cols // NSA_PW
    assert S >= NSA_WK and tq % LANES == 0 and NSA_TK % tq == 0

    def full(a):
        return pl.BlockSpec((1,) + a.shape[1:], lambda k, i: (k,) + (0,) * (a.ndim - 1))

    return pl.pallas_call(
        _nsa_kernel,
        out_shape=jax.ShapeDtypeStruct((S, NSA_HEADS * HD), BF16),
        grid=(NSA_KV_HEADS, S // tq),
        in_specs=[pl.BlockSpec((tq, NSA_GROUP * HD), lambda k, i: (i, CB_Q // NSA_GROUP + k)),
                  pl.BlockSpec((1, tq, LANES), lambda k, i: (k, i, 0)),
                  pl.BlockSpec((tq, NSA_GROUP * HD), lambda k, i: (i, k)),
                  pl.BlockSpec((tq, LANES), lambda k, i: (i, CB_GT)),
                  full(ksa), full(vst), full(kwb), full(vwt)],
        out_specs=pl.BlockSpec((tq, NSA_GROUP * HD), lambda k, i: (i, k)),
        scratch_shapes=([pltpu.VMEM((NSA_WK, NSA_PW), F32)] * npart + [pltpu.VMEM((NSA_WK, NSA_PW), BF16)] * npart
                        + [pltpu.VMEM((HD, NSA_PW), F32)] * npart),
        compiler_params=_cparams(("arbitrary", "arbitrary")),
        name="nsa_sel_win",
    )(proj, selbias, oc, proj, ksa, vst, kwb, vwt)


def _ret_kernel(q_ref, k_ref, v_ref, g_ref, dm_ref, ze_ref, xi_ref, dc_ref, gg_ref, gb_ref, o_ref, st_ref):
    n = pl.program_id(1)
    C = RET_CHUNK
    nt = (((1,), (1,)), ((), ()))
    tn = (((0,), (0,)), ((), ()))

    @pl.when(n == 0)
    def _():
        st_ref[...] = jnp.zeros_like(st_ref)

    dmat = dm_ref[0]
    zeta = ze_ref[0]
    xi = xi_ref[0]
    decay = dc_ref[0]
    for c in range(q_ref.shape[0] // C):
        sl = slice(c * C, (c + 1) * C)
        q = q_ref[sl, :]
        kk = k_ref[sl, :] * (HD ** -0.5)
        v = v_ref[sl, :]
        qb, kb, vb = q.astype(BF16), kk.astype(BF16), v.astype(BF16)
        inner = lax.dot_general(qb, kb, nt, preferred_element_type=F32) * dmat
        o = jnp.dot(inner.astype(BF16), vb, preferred_element_type=F32)
        state = st_ref[...]
        o = o + jnp.dot(qb, state.astype(BF16), preferred_element_type=F32) * xi
        kv = lax.dot_general((kk * zeta).astype(BF16), vb, tn, preferred_element_type=F32)
        st_ref[...] = decay * state + kv
        mu = jnp.mean(o, axis=-1, keepdims=True)
        var = jnp.mean(jnp.square(o - mu), axis=-1, keepdims=True)
        y = (o - mu) * lax.rsqrt(var + NORM_EPS) * gg_ref[...] + gb_ref[...]
        gt = g_ref[sl, :]
        o_ref[sl, :] = (y * (gt * _sigmoid(gt))).astype(o_ref.dtype)


def _retention(proj, gn_g, gn_b):
    S = proj.shape[0]
    C = RET_CHUNK
    ts = 1024 if S % 1024 == 0 else C
    H = RET_HEADS
    log_g = jnp.log(1.0 - 2.0 ** (-5.0 - jnp.arange(H, dtype=F32)))
    i = jnp.arange(C, dtype=F32)
    diff = i[:, None] - i[None, :]
    dmat = jnp.where(diff >= 0, jnp.exp(log_g[:, None, None] * jnp.maximum(diff, 0.0)), 0.0)
    zeta = jnp.exp(log_g[:, None] * (C - 1.0 - i))[:, :, None]
    xi = jnp.exp(log_g[:, None] * (i + 1.0))[:, :, None]
    decay = jnp.broadcast_to(jnp.exp(log_g * C)[:, None, None], (H, 1, HD))

    def col(cb):
        return pl.BlockSpec((ts, HD), lambda h, n: (n, cb + h))

    return pl.pallas_call(
        _ret_kernel,
        out_shape=jax.ShapeDtypeStruct((S, H * HD), BF16),
        grid=(H, S // ts),
        in_specs=[col(CB_RQ), col(CB_RK), col(CB_RV), col(CB_RG),
                  pl.BlockSpec((1, C, C), lambda h, n: (h, 0, 0)),
                  pl.BlockSpec((1, C, 1), lambda h, n: (h, 0, 0)),
                  pl.BlockSpec((1, C, 1), lambda h, n: (h, 0, 0)),
                  pl.BlockSpec((1, 1, HD), lambda h, n: (h, 0, 0)),
                  pl.BlockSpec((1, HD), lambda h, n: (0, h)),
                  pl.BlockSpec((1, HD), lambda h, n: (0, h))],
        out_specs=pl.BlockSpec((ts, HD), lambda h, n: (n, h)),
        scratch_shapes=[pltpu.VMEM((HD, HD), F32)],
        compiler_params=_cparams(("arbitrary", "arbitrary")),
        name="retention",
    )(proj, proj, proj, proj, dmat, zeta, xi, decay, gn_g.reshape(1, -1), gn_b.reshape(1, -1))


def _outproj_kernel(x_ref, yc_ref, yn_ref, yr_ref, wc_ref, wn_ref, wr_ref, ga_ref, o_ref):
    y = jnp.dot(yc_ref[...], wc_ref[...], preferred_element_type=F32)
    y = y + jnp.dot(yn_ref[...], wn_ref[...], preferred_element_type=F32)
    y = y + jnp.dot(yr_ref[...], wr_ref[...], preferred_element_type=F32)
    o_ref[...] = x_ref[...] + ga_ref[...] * y


def _out_proj(x2, y_conv, y_nsa, y_ret, w_out_bf, gate_a):
    S = x2.shape[0]
    tm = 512
    wc, wn, wr = w_out_bf[:CONV_CH], w_out_bf[CONV_CH:CONV_CH + NSA_HEADS * HD], w_out_bf[CONV_CH + NSA_HEADS * HD:]

    def rows(w):
        return pl.BlockSpec((tm, w), lambda i: (i, 0))

    def whole(a):
        return pl.BlockSpec(a.shape, lambda i: (0, 0))

    return pl.pallas_call(
        _outproj_kernel,
        out_shape=jax.ShapeDtypeStruct((S, D_MODEL), F32),
        grid=(S // tm,),
        in_specs=[rows(D_MODEL), rows(y_conv.shape[1]), rows(y_nsa.shape[1]), rows(y_ret.shape[1]),
                  whole(wc), whole(wn), whole(wr), pl.BlockSpec((1, D_MODEL), lambda i: (0, 0))],
        out_specs=rows(D_MODEL),
        compiler_params=_cparams(("arbitrary",)),
        name="out_proj",
    )(x2, y_conv, y_nsa, y_ret, wc, wn, wr, gate_a)


def _router_kernel(x_ref, g_ref, sc_ref, sh_ref, rw_ref, rb_ref, h_ref, rt_ref, cnt_ref, carry_ref):
    i = pl.program_id(0)
    tm = x_ref.shape[0]

    @pl.when(i == 0)
    def _():
        carry_ref[...] = jnp.zeros_like(carry_ref)

    h = _modulated_norm(x_ref[...], g_ref[...], sc_ref[...], sh_ref[...])
    half = D_MODEL // 2
    hi = lax.bitcast_convert_type(h[:, :half].astype(BF16).astype(F32), jnp.uint32)
    lo = lax.bitcast_convert_type(h[:, half:].astype(BF16).astype(F32), jnp.uint32)
    h_ref[...] = hi | (lo >> 16)
    rw = rw_ref[...]
    rw_hi = rw.astype(BF16)
    rw_lo = (rw - rw_hi.astype(F32)).astype(BF16)
    h_hi = h.astype(BF16)
    h_lo = (h - h_hi.astype(F32)).astype(BF16)
    logits = (jnp.dot(h_hi, rw_hi, preferred_element_type=F32) + jnp.dot(h_hi, rw_lo, preferred_element_type=F32)
              + jnp.dot(h_lo, rw_hi, preferred_element_type=F32) + rb_ref[...])
    lane = lax.broadcasted_iota(jnp.int32, (tm, LANES), 1)
    lane_f = lane.astype(F32)
    work = jnp.where(lane < N_EXPERTS, logits, -jnp.inf)
    onehot = jnp.zeros((tm, LANES), F32)
    vals, idxs = [], []
    for _ in range(TOP_K):
        m = jnp.max(work, axis=-1, keepdims=True)
        first = jnp.min(jnp.where(work == m, lane_f, float(LANES)), axis=-1, keepdims=True)
        hit = lane_f == first
        onehot = jnp.where(hit, 1.0, onehot)
        work = jnp.where(hit, -jnp.inf, work)
        vals.append(m)
        idxs.append(first)
    ex = [jnp.exp(v - vals[0]) for v in vals]
    den = ex[0] + ex[1] + ex[2] + ex[3]
    r = lax.broadcasted_iota(jnp.int32, (tm, tm), 0)
    c = lax.broadcasted_iota(jnp.int32, (tm, tm), 1)
    tri = jnp.where(c < r, 1.0, 0.0).astype(BF16)
    cum = jnp.dot(tri, onehot.astype(BF16), preferred_element_type=F32) + carry_ref[...]
    out = jnp.zeros((tm, LANES), F32)
    for kk in range(TOP_K):
        rank = jnp.sum(jnp.where(lane_f == idxs[kk], cum, 0.0), axis=-1, keepdims=True)
        out = jnp.where(lane == kk, idxs[kk], out)
        out = jnp.where(lane == TOP_K + kk, ex[kk] / den, out)
        out = jnp.where(lane == 2 * TOP_K + kk, rank, out)
    rt_ref[...] = out
    carry_ref[...] = carry_ref[...] + jnp.sum(onehot, axis=0, keepdims=True)
    cnt_ref[...] = carry_ref[...]


def _router(x2, g, scale, shift, router_w, router_b):
    T = x2.shape[0]
    tm = 256
    rw = jnp.pad(router_w, ((0, 0), (0, LANES - N_EXPERTS)))
    rb = jnp.pad(router_b, (0, LANES - N_EXPERTS)).reshape(1, LANES)
    row = pl.BlockSpec((1, D_MODEL), lambda i: (0, 0))
    return pl.pallas_call(
        _router_kernel,
        out_shape=(jax.ShapeDtypeStruct((T, D_MODEL // 2), jnp.uint32),
                   jax.ShapeDtypeStruct((T, LANES), F32),
                   jax.ShapeDtypeStruct((1, LANES), F32)),
        grid=(T // tm,),
        in_specs=[pl.BlockSpec((tm, D_MODEL), lambda i: (i, 0)), row, row, row,
                  pl.BlockSpec((D_MODEL, LANES), lambda i: (0, 0)),
                  pl.BlockSpec((1, LANES), lambda i: (0, 0))],
        out_specs=(pl.BlockSpec((tm, D_MODEL // 2), lambda i: (i, 0)),
                   pl.BlockSpec((tm, LANES), lambda i: (i, 0)),
                   pl.BlockSpec((1, LANES), lambda i: (0, 0))),
        scratch_shapes=[pltpu.VMEM((1, LANES), F32)],
        compiler_params=_cparams(("arbitrary",)),
        name="moe_router",
    )(x2, g, scale, shift, rw, rb)


GATHER_UNROLL = 8


def _expert_kernel(ie_ref, ir_ref, ins_ref, tok_ref, hp_hbm, wgl_ref, wli_ref, bgl_ref, bli_ref, wdn_ref, bdn_ref,
                   ys_hbm, stage, xbf, acc, wg_s, wl_s, wd_s, sem_g, sem_out):
    w = pl.program_id(0)
    ct = pl.program_id(1)
    n_w = pl.num_programs(0)
    n_ct = pl.num_programs(1)
    nsub = ins_ref[w]
    rows = stage.shape[0]
    n_slots = tok_ref.shape[0]
    per_step = rows // MOE_NCT
    half = D_MODEL // 2

    def gather_row(item, r):
        t = tok_ref[jnp.minimum(ir_ref[item] + r, n_slots - 1)]
        pltpu.make_async_copy(hp_hbm.at[pl.ds(t, 1)], stage.at[pl.ds(r, 1)], sem_g).start()

    def y_copy(item, sb):
        row = pl.multiple_of(ir_ref[item] + sb * MOE_SUB, MOE_SUB)
        return pltpu.make_async_copy(acc.at[pl.ds(sb * MOE_SUB, MOE_SUB)],
                                     ys_hbm.at[pl.ds(row, MOE_SUB)], sem_out.at[sb])

    def for_subs(item, fn):
        n = ins_ref[item]
        for sb in range(MOE_SUBMAX):
            @pl.when(sb < n)
            def _():
                fn(item, sb)

    @pl.when(nsub > 0)
    def _():
        @pl.when((ct == 0) & (w == 0))
        def _():
            def body(c, carry):
                for u in range(GATHER_UNROLL):
                    gather_row(w, c * GATHER_UNROLL + u)
                return carry

            lax.fori_loop(0, rows // GATHER_UNROLL, body, 0)

        @pl.when(ct == 0)
        def _():
            for sb in range(MOE_SUBMAX):
                pltpu.make_async_copy(hp_hbm.at[pl.ds(0, MOE_SUB)], stage.at[pl.ds(sb * MOE_SUB, MOE_SUB)],
                                      sem_g).wait()
            for sb in range(MOE_SUBMAX):
                wv = stage[sb * MOE_SUB:(sb + 1) * MOE_SUB, :]
                xbf[sb * MOE_SUB:(sb + 1) * MOE_SUB, 0:half] = lax.bitcast_convert_type(
                    wv & jnp.uint32(0xFFFF0000), F32).astype(BF16)
                xbf[sb * MOE_SUB:(sb + 1) * MOE_SUB, half:D_MODEL] = lax.bitcast_convert_type(
                    wv << 16, F32).astype(BF16)

        nxt = jnp.minimum(w + 1, n_w - 1)

        @pl.when((w + 1 < n_w) & (ins_ref[nxt] > 0))
        def _():
            for u in range(per_step):
                gather_row(nxt, ct * per_step + u)
            wg_s[...] = wgl_ref[0, 0].astype(BF16)
            wl_s[...] = wli_ref[0, 0].astype(BF16)
            wd_s[...] = wdn_ref[0, 0].astype(BF16)

        @pl.when(jnp.logical_not((w + 1 < n_w) & (ins_ref[nxt] > 0)))
        def _():
            wg_s[...] = wgl_ref[0, 0].astype(BF16)
            wl_s[...] = wli_ref[0, 0].astype(BF16)
            wd_s[...] = wdn_ref[0, 0].astype(BF16)

        @pl.when((ct == 0) & (w > 0))
        def _():
            for_subs(jnp.maximum(w - 1, 0), lambda it, sb: y_copy(it, sb).wait())

        def ffn_rows(r0, m_rows):
            x = xbf[pl.ds(r0, m_rows), :]
            glu = jnp.dot(x, wg_s[...], preferred_element_type=F32) + bgl_ref[0]
            lin = jnp.dot(x, wl_s[...], preferred_element_type=F32) + bli_ref[0]
            glu = jnp.minimum(glu, SWIGLU_LIMIT)
            lin = jnp.clip(lin, -SWIGLU_LIMIT, SWIGLU_LIMIT)
            act = glu * _sigmoid(SWIGLU_ALPHA * glu) * (lin + 1.0)
            y = jnp.dot(act.astype(BF16), wd_s[...], preferred_element_type=F32)

            @pl.when(ct == 0)
            def _():
                acc[pl.ds(r0, m_rows), :] = y + bdn_ref[0]

            @pl.when(ct > 0)
            def _():
                acc[pl.ds(r0, m_rows), :] += y

        def quad_body(qd, carry):
            ffn_rows(pl.multiple_of(qd * (4 * MOE_SUB), 4 * MOE_SUB), 4 * MOE_SUB)
            return carry

        lax.fori_loop(0, nsub // 4, quad_body, 0)

        @pl.when(nsub % 4 >= 2)
        def _():
            ffn_rows(pl.multiple_of((nsub // 4) * (4 * MOE_SUB), 2 * MOE_SUB), 2 * MOE_SUB)

        @pl.when(nsub % 2 == 1)
        def _():
            ffn_rows(pl.multiple_of((nsub - 1) * MOE_SUB, MOE_SUB), MOE_SUB)

        @pl.when(ct == n_ct - 1)
        def _():
            for_subs(w, lambda it, sb: y_copy(it, sb).start())
            last = (w == n_w - 1) | (ins_ref[nxt] == 0)

            @pl.when(last)
            def _():
                for_subs(w, lambda it, sb: y_copy(it, sb).wait())


def _experts(hp, slot_tok, item_e, item_row0, item_nsub, w_gu, b_gu, w_dn, b_dn, layer):
    n_slots = slot_tok.shape[0]
    W = item_e.shape[0]
    te = MOE_TE
    n_ct = MOE_NCT
    rows = MOE_SUBMAX * MOE_SUB
    assert rows % n_ct == 0

    def ct_eff(w, ct, ins):
        return jnp.where(ins[w] > 0, ct, n_ct - 1)

    return pl.pallas_call(
        _expert_kernel,
        out_shape=jax.ShapeDtypeStruct((n_slots, D_MODEL), F32),
        grid_spec=pltpu.PrefetchScalarGridSpec(
            num_scalar_prefetch=4,
            grid=(W, n_ct),
            in_specs=[pl.BlockSpec(memory_space=pl.ANY),
                      pl.BlockSpec((1, 1, D_MODEL, te),
                                   lambda w, ct, ie, ir, ins, tok: (layer, ie[w], 0, ct_eff(w, ct, ins))),
                      pl.BlockSpec((1, 1, D_MODEL, te),
                                   lambda w, ct, ie, ir, ins, tok: (layer, ie[w], 0, n_ct + ct_eff(w, ct, ins))),
                      pl.BlockSpec((1, 1, te), lambda w, ct, ie, ir, ins, tok: (ie[w], 0, ct_eff(w, ct, ins))),
                      pl.BlockSpec((1, 1, te), lambda w, ct, ie, ir, ins, tok: (ie[w], 0, n_ct + ct_eff(w, ct, ins))),
                      pl.BlockSpec((1, 1, te, D_MODEL),
                                   lambda w, ct, ie, ir, ins, tok: (layer, ie[w], ct_eff(w, ct, ins), 0)),
                      pl.BlockSpec((1, 1, D_MODEL), lambda w, ct, ie, ir, ins, tok: (ie[w], 0, 0))],
            out_specs=pl.BlockSpec(memory_space=pl.ANY),
            scratch_shapes=[pltpu.VMEM((rows, D_MODEL // 2), jnp.uint32), pltpu.VMEM((rows, D_MODEL), BF16),
                            pltpu.VMEM((rows, D_MODEL), F32),
                            pltpu.VMEM((D_MODEL, te), BF16), pltpu.VMEM((D_MODEL, te), BF16),
                            pltpu.VMEM((te, D_MODEL), BF16),
                            pltpu.SemaphoreType.DMA(()), pltpu.SemaphoreType.DMA((MOE_SUBMAX,))]),
        compiler_params=_cparams(("arbitrary", "arbitrary"), vmem=60 * 1024 * 1024),
        name="moe_experts",
    )(item_e, item_row0, item_nsub, slot_tok, hp, w_gu, w_gu, b_gu.reshape(N_EXPERTS, 1, -1),
      b_gu.reshape(N_EXPERTS, 1, -1), w_dn, b_dn.reshape(N_EXPERTS, 1, -1))


def _combine_kernel(dest_ref, x_ref, rt_ref, gf_ref, fg_ref, ys_hbm, o_ref, buf, sem, *, final_norm):
    i = pl.program_id(0)
    n = pl.num_programs(0)
    tm = x_ref.shape[0]

    def issue(tile):
        slot = tile % 2

        def body(c, carry):
            for u in range(GATHER_UNROLL // TOP_K):
                r = c * (GATHER_UNROLL // TOP_K) + u
                for kk in range(TOP_K):
                    d = dest_ref[(tile * tm + r) * TOP_K + kk]
                    pltpu.make_async_copy(ys_hbm.at[pl.ds(d, 1)], buf.at[slot, kk, pl.ds(r, 1)],
                                          sem.at[slot]).start()
            return carry

        lax.fori_loop(0, tm * TOP_K // GATHER_UNROLL, body, 0)

    @pl.when(i == 0)
    def _():
        issue(i)

    @pl.when(i + 1 < n)
    def _():
        issue(i + 1)

    slot = i % 2
    for kk in range(TOP_K):
        pltpu.make_async_copy(ys_hbm.at[pl.ds(0, tm)], buf.at[slot, kk], sem.at[slot]).wait()
    rt = rt_ref[...]
    lane = lax.broadcasted_iota(jnp.int32, rt.shape, 1)
    moe = jnp.zeros((tm, D_MODEL), F32)
    for kk in range(TOP_K):
        moe = moe + buf[slot, kk] * _lane_pick(rt, lane, TOP_K + kk)
    out = x_ref[...] + gf_ref[...] * moe
    if final_norm:
        ms = jnp.mean(out * out, axis=-1, keepdims=True)
        out = out * lax.rsqrt(ms + NORM_EPS) * fg_ref[...]
    o_ref[...] = out


def _combine(x2, route, gate_f, final_g, ys, dest_flat, final_norm):
    T = x2.shape[0]
    tm = 128
    row = pl.BlockSpec((1, D_MODEL), lambda i, d: (0, 0))
    return pl.pallas_call(
        functools.partial(_combine_kernel, final_norm=final_norm),
        out_shape=jax.ShapeDtypeStruct((T, D_MODEL), F32),
        grid_spec=pltpu.PrefetchScalarGridSpec(
            num_scalar_prefetch=1,
            grid=(T // tm,),
            in_specs=[pl.BlockSpec((tm, D_MODEL), lambda i, d: (i, 0)),
                      pl.BlockSpec((tm, LANES), lambda i, d: (i, 0)), row, row,
                      pl.BlockSpec(memory_space=pl.ANY)],
            out_specs=pl.BlockSpec((tm, D_MODEL), lambda i, d: (i, 0)),
            scratch_shapes=[pltpu.VMEM((2, TOP_K, tm, D_MODEL), F32), pltpu.SemaphoreType.DMA((2,))]),
        compiler_params=_cparams(("arbitrary",)),
        name="moe_combine",
    )(dest_flat, x2, route, gate_f, final_g, ys)


def _moe_plan(route, counts_f):
    T = route.shape[0]
    e = route[:, 0:TOP_K].astype(jnp.int32)
    rank = route[:, 2 * TOP_K:3 * TOP_K].astype(jnp.int32)
    counts = counts_f[0, :N_EXPERTS].astype(jnp.int32)
    nsub = (counts + MOE_SUB - 1) // MOE_SUB
    padded = nsub * MOE_SUB
    pend = jnp.cumsum(padded)
    pstart = pend - padded
    dest = pstart[e] + rank
    n_slots = (T * TOP_K + MOE_SUB - 1) // MOE_SUB * MOE_SUB + N_EXPERTS * MOE_SUB
    tok = jnp.repeat(jnp.arange(T, dtype=jnp.int32), TOP_K)
    slot_tok = jnp.zeros((n_slots,), jnp.int32).at[dest.reshape(-1)].set(tok)
    n_used = (pend[-1] // MOE_SUB).astype(jnp.int32).reshape(1)
    n_items_max = N_EXPERTS + (n_slots // MOE_SUB) // MOE_SUBMAX
    per_e = (nsub + MOE_SUBMAX - 1) // MOE_SUBMAX
    iend = jnp.cumsum(per_e)
    w = jnp.arange(n_items_max, dtype=jnp.int32)
    ew = jnp.minimum(jnp.searchsorted(iend, w, side='right'), N_EXPERTS - 1).astype(jnp.int32)
    local = w - (iend[ew] - per_e[ew])
    live = w < iend[-1]
    item_nsub = jnp.where(live, jnp.clip(nsub[ew] - local * MOE_SUBMAX, 0, MOE_SUBMAX), 0).astype(jnp.int32)
    item_row0 = jnp.where(live, pstart[ew] + local * (MOE_SUBMAX * MOE_SUB), 0).astype(jnp.int32)
    last_e = ew[jnp.maximum(iend[-1] - 1, 0)]
    item_e = jnp.where(live, ew, last_e).astype(jnp.int32)
    return dest.reshape(-1).astype(jnp.int32), slot_tok, n_used, item_e, item_row0, item_nsub


def _repack_w_in(w_in_l):
    parts, width = [], 0
    for name in _ORDER:
        off, size = _SRC[name]
        parts.append(w_in_l[:, off:off + size])
        width += size
        if size % LANES:
            parts.append(jnp.zeros((D_MODEL, LANES - size % LANES), w_in_l.dtype))
            width += LANES - size % LANES
    parts.append(jnp.zeros((D_MODEL, PROJ_W - width), w_in_l.dtype))
    return jnp.concatenate(parts, axis=1).astype(BF16)


def _cover_matrix(S):
    n_strip = S // CMP_STRIDE
    n_cmp = (S - CMP_BLOCK) // CMP_STRIDE + 1
    n = np.arange(n_strip)[:, None]
    j = np.arange(LANES)[None, :]
    start, end = n * CMP_STRIDE, n * CMP_STRIDE + CMP_BLOCK - 1
    cov = (start <= j * SEL_BLOCK + SEL_BLOCK - 1) & (end >= j * SEL_BLOCK) & (n < n_cmp) & (j < S // SEL_BLOCK)
    return jnp.asarray(cov.astype(np.float32).T)


def _strips(proj, cb):
    S = proj.shape[0]
    t = proj[:, cb * LANES:(cb + NSA_KV_HEADS) * LANES].reshape(S, NSA_KV_HEADS, HD)
    return t.transpose(1, 0, 2).reshape(NSA_KV_HEADS, S // CMP_STRIDE, CMP_STRIDE * HD)


def kernel(x, c, ada_w, ada_b, norm_mix_g, w_in, conv_dw_w, conv_dw_b, conv_ln_g, conv_ln_b, nsa_pe_k, nsa_pe_v, nsa_cmp_k_w1, nsa_cmp_k_w2, nsa_cmp_v_w1, nsa_cmp_v_w2, ret_gn_g, ret_gn_b, w_out, norm_ffn_g, router_w, router_b, moe_w_gate_up, moe_b_gate_up, moe_w_down, moe_b_down, final_norm_g):
    B, S, _ = x.shape
    assert B == 1 and c.shape[0] == 1
    x2 = x.reshape(S, D_MODEL)
    mod = _ada_mod(c, ada_w, ada_b).reshape(DEPTH, 6, 1, D_MODEL)
    tabs = _rotary_tables(S)
    cover = _cover_matrix(S)
    final_g = final_norm_g.reshape(1, D_MODEL)
    for l in range(DEPTH):
        shift_a, scale_a, gate_a, shift_f, scale_f, gate_f = [mod[l, i] for i in range(6)]
        proj = _in_proj(x2, norm_mix_g[l].reshape(1, -1), scale_a, shift_a, _repack_w_in(w_in[l]), tabs)
        y_conv = _conv_group(proj, conv_dw_w[l], conv_dw_b[l], conv_ln_g[l], conv_ln_b[l])
        kc = _compress(_strips(proj, CB_KC), nsa_pe_k[l], nsa_cmp_k_w1[l], nsa_cmp_k_w2[l])
        vc = _compress(_strips(proj, CB_VC), nsa_pe_v[l], nsa_cmp_v_w1[l], nsa_cmp_v_w2[l])
        oc, selbias = _cmp_attention(proj, kc, vc, cover)
        ksa, vsb, kwb, vwb = _kv_prep(proj)
        y_nsa = _nsa_attention(proj, selbias, oc, ksa, vsb, kwb, vwb)
        y_ret = _retention(proj, ret_gn_g[l], ret_gn_b[l])
        x2 = _out_proj(x2, y_conv, y_nsa, y_ret, w_out[l].astype(BF16), gate_a)
        h2, route, counts = _router(x2, norm_ffn_g[l].reshape(1, -1), scale_f, shift_f, router_w[l], router_b[l])
        dest, slot_tok, n_used, item_e, item_row0, item_nsub = _moe_plan(route, counts)
        ys = _experts(h2, slot_tok, item_e, item_row0, item_nsub, moe_w_gate_up, moe_b_gate_up[l], moe_w_down,
                      moe_b_down[l], l)
        x2 = _combine(x2, route, gate_f, final_g, ys, dest, final_norm=(l == DEPTH - 1))
    return x2.reshape(B, S, D_MODEL)
```

```python
import functools
import math

import numpy as np
import jax
import jax.numpy as jnp
from jax import lax
from jax.experimental import pallas as pl
from jax.experimental.pallas import tpu as pltpu

F32 = jnp.float32
BF16 = jnp.bfloat16
HI = lax.Precision.HIGHEST

D_MODEL = 2048
DEPTH = 2
CONV_CH = 512
CONV_WIDTH = 31
HD = 128
NSA_HEADS = 8
NSA_KV_HEADS = 2
NSA_GROUP = 4
CMP_BLOCK = 32
CMP_STRIDE = 16
SEL_BLOCK = 64
SEL_TOPN = 16
WINDOW = 512
ROPE_THETA = 500000.0
ROPE_DIM = 32
RET_HEADS = 4
RET_CHUNK = 128
RET_THETA = 10000.0
N_EXPERTS = 32
TOP_K = 4
D_EXPERT = 2048
SWIGLU_LIMIT = 7.0
SWIGLU_ALPHA = 1.702
NORM_EPS = 1e-6
NEG = -1e30
BIG = 1e30
LOG2E = math.log2(math.e)

LANES = 128
VMEM_LIMIT = 56 * 1024 * 1024

CB_Q, CB_KC, CB_KS, CB_KW = 0, 8, 10, 12
CB_RQ, CB_RK = 14, 18
CB_VC, CB_CV, CB_CG, CB_VS, CB_VW, CB_GT, CB_RV, CB_RG = 22, 24, 28, 32, 34, 36, 37, 41
PROJ_BLOCKS = 46
PROJ_W = PROJ_BLOCKS * LANES
PROJ_TN = 256
NSA_ROT_TILES = CB_RQ * LANES // PROJ_TN
RET_ROT_TILES = CB_VC * LANES // PROJ_TN
_SRC = {'cv': (0, 512), 'cg': (512, 512), 'q': (1024, 1024), 'kc': (2048, 256), 'vc': (2304, 256),
        'ks': (2560, 256), 'vs': (2816, 256), 'kw': (3072, 256), 'vw': (3328, 256), 'gt': (3584, 24),
        'rq': (3608, 512), 'rk': (4120, 512), 'rv': (4632, 512), 'rg': (5144, 512)}
_ORDER = ['q', 'kc', 'ks', 'kw', 'rq', 'rk', 'vc', 'cv', 'cg', 'vs', 'vw', 'gt', 'rv', 'rg']

MOE_SUB = 256
MOE_SUBMAX = 5
MOE_TE = 512
MOE_NCT = D_EXPERT // MOE_TE


def _sigmoid(x):
    return 1.0 / (1.0 + jnp.exp(-x))


def _cparams(sem, vmem=VMEM_LIMIT):
    return pltpu.CompilerParams(dimension_semantics=sem, vmem_limit_bytes=vmem)


def _ada_kernel(c_ref, w_ref, b_ref, o_ref, sc_ref):
    @pl.when(pl.program_id(1) == 0)
    def _():
        cv = c_ref[...]
        sc_ref[...] = cv * _sigmoid(cv)

    tn = o_ref.shape[-1]

    def body(i, acc):
        k0 = pl.multiple_of(i * 64, 64)
        p = w_ref[0, pl.ds(k0, 64), :] * sc_ref[pl.ds(k0, 64), :]
        return acc + p.reshape(8, 8, tn).sum(axis=0)

    acc = lax.fori_loop(0, D_MODEL // 64, body, jnp.zeros((8, tn), F32))
    o_ref[0] = jnp.sum(acc, axis=0, keepdims=True) + b_ref[0]


def _ada_mod(c, ada_w, ada_b):
    tn = 1024
    n = 6 * D_MODEL
    return pl.pallas_call(
        _ada_kernel,
        out_shape=jax.ShapeDtypeStruct((DEPTH, 1, n), F32),
        grid=(DEPTH, n // tn),
        in_specs=[pl.BlockSpec((D_MODEL, 1), lambda l, j: (0, 0)),
                  pl.BlockSpec((1, D_MODEL, tn), lambda l, j: (l, 0, j)),
                  pl.BlockSpec((1, 1, tn), lambda l, j: (l, 0, j))],
        out_specs=pl.BlockSpec((1, 1, tn), lambda l, j: (l, 0, j)),
        scratch_shapes=[pltpu.VMEM((D_MODEL, 1), F32)],
        compiler_params=_cparams(("arbitrary", "arbitrary")),
        name="ada_mod",
    )(c.reshape(D_MODEL, 1), ada_w, ada_b.reshape(DEPTH, 1, n))


def _modulated_norm(x, g, scale, shift):
    ms = jnp.mean(x * x, axis=-1, keepdims=True)
    return x * lax.rsqrt(ms + NORM_EPS) * g * (1.0 + scale) + shift


def _inproj_kernel(x_ref, g_ref, sc_ref, sh_ref, w_ref, cn_ref, s1_ref, s2_ref, cr_ref, sr_ref,
                   o_ref, h_ref):
    j = pl.program_id(1)
    nsub = o_ref.shape[-1] // LANES

    @pl.when(j == 0)
    def _():
        h_ref[...] = _modulated_norm(x_ref[...], g_ref[...], sc_ref[...], sh_ref[...]).astype(BF16)

    def matmul():
        return jnp.dot(h_ref[...], w_ref[...], preferred_element_type=F32)

    @pl.when(j < NSA_ROT_TILES)
    def _():
        acc = matmul()
        for c in range(nsub):
            sub = acc[:, c * LANES:(c + 1) * LANES]
            o_ref[:, c * LANES:(c + 1) * LANES] = (
                sub * cn_ref[...] + pltpu.roll(sub, LANES - ROPE_DIM // 2, 1) * s1_ref[...]
                + pltpu.roll(sub, ROPE_DIM // 2, 1) * s2_ref[...])

    @pl.when((j >= NSA_ROT_TILES) & (j < RET_ROT_TILES))
    def _():
        acc = matmul()
        for c in range(nsub):
            sub = acc[:, c * LANES:(c + 1) * LANES]
            o_ref[:, c * LANES:(c + 1) * LANES] = sub * cr_ref[...] + pltpu.roll(sub, HD // 2, 1) * sr_ref[...]

    @pl.when(j >= RET_ROT_TILES)
    def _():
        o_ref[...] = matmul()


def _in_proj(x2, g, scale, shift, w_bf, tabs):
    S = x2.shape[0]
    tm, tn = 1024, PROJ_TN
    row = pl.BlockSpec((1, D_MODEL), lambda i, j: (0, 0))
    tab = pl.BlockSpec((tm, LANES), lambda i, j: (i, 0))
    return pl.pallas_call(
        _inproj_kernel,
        out_shape=jax.ShapeDtypeStruct((S, PROJ_W), F32),
        grid=(S // tm, PROJ_W // tn),
        in_specs=[pl.BlockSpec((tm, D_MODEL), lambda i, j: (i, 0)), row, row, row,
                  pl.BlockSpec((D_MODEL, tn), lambda i, j: (0, j)), tab, tab, tab, tab, tab],
        out_specs=pl.BlockSpec((tm, tn), lambda i, j: (i, j)),
        scratch_shapes=[pltpu.VMEM((tm, D_MODEL), BF16)],
        compiler_params=_cparams(("arbitrary", "arbitrary")),
        name="in_proj",
    )(x2, g, scale, shift, w_bf, *tabs)


def _rotary_tables(S):
    pos = jnp.arange(S, dtype=F32)[:, None]
    half = ROPE_DIM // 2
    inv = ROPE_THETA ** (-jnp.arange(half, dtype=F32) * 2.0 / ROPE_DIM)
    ang = pos * inv[None, :]
    cos, sin = jnp.cos(ang), jnp.sin(ang)
    ones = jnp.ones((S, LANES - ROPE_DIM), F32)
    zeros_r = jnp.zeros((S, LANES - ROPE_DIM), F32)
    zeros_h = jnp.zeros((S, half), F32)
    cn = jnp.concatenate([cos, cos, ones], axis=1)
    s1 = jnp.concatenate([-sin, zeros_h, zeros_r], axis=1)
    s2 = jnp.concatenate([zeros_h, sin, zeros_r], axis=1)
    halfr = HD // 2
    invr = RET_THETA ** (-jnp.arange(halfr, dtype=F32) * 2.0 / HD)
    angr = pos * invr[None, :]
    cosr, sinr = jnp.cos(angr), jnp.sin(angr)
    cr = jnp.concatenate([cosr, cosr], axis=1)
    sr = jnp.concatenate([-sinr, sinr], axis=1)
    return cn, s1, s2, cr, sr


def _conv_kernel(cv_ref, cg_ref, dw_ref, db_ref, lg_ref, lb_ref, o_ref, hb_ref, xs_ref):
    i = pl.program_id(0)
    ts = o_ref.shape[0]
    halo = 32

    @pl.when(i == 0)
    def _():
        hb_ref[0:halo, :] = jnp.zeros((halo, CONV_CH), F32)

    @pl.when(i > 0)
    def _():
        hb_ref[0:halo, :] = hb_ref[ts:ts + halo, :]

    hb_ref[halo:halo + ts, :] = cv_ref[...] * _sigmoid(cg_ref[...])
    acc = jnp.zeros((ts, CONV_CH), F32) + db_ref[...]
    first = halo - (CONV_WIDTH - 1)
    for sh in range(8):
        offs = [first + w for w in range(CONV_WIDTH) if (first + w) % 8 == sh]
        span = max(offs) - sh + ts
        xs_ref[sh, 0:span, :] = hb_ref[sh:sh + span, :]
        for off in offs:
            acc = acc + xs_ref[sh, off - sh:off - sh + ts, :] * dw_ref[off - first:off - first + 1, :]
    mu = jnp.mean(acc, axis=-1, keepdims=True)
    var = jnp.mean(jnp.square(acc - mu), axis=-1, keepdims=True)
    y = (acc - mu) * lax.rsqrt(var + NORM_EPS) * lg_ref[...] + lb_ref[...]
    o_ref[...] = (y * _sigmoid(y)).astype(o_ref.dtype)


def _conv_group(proj, dw_w, dw_b, ln_g, ln_b):
    S = proj.shape[0]
    ts = 256
    vec = pl.BlockSpec((1, CONV_CH), lambda i: (0, 0))
    return pl.pallas_call(
        _conv_kernel,
        out_shape=jax.ShapeDtypeStruct((S, CONV_CH), BF16),
        grid=(S // ts,),
        in_specs=[pl.BlockSpec((ts, CONV_CH), lambda i: (i, CB_CV * LANES // CONV_CH)),
                  pl.BlockSpec((ts, CONV_CH), lambda i: (i, CB_CG * LANES // CONV_CH)),
                  pl.BlockSpec((CONV_WIDTH, CONV_CH), lambda i: (0, 0)), vec, vec, vec],
        out_specs=pl.BlockSpec((ts, CONV_CH), lambda i: (i, 0)),
        scratch_shapes=[pltpu.VMEM((ts + 32, CONV_CH), F32), pltpu.VMEM((8, ts + 32, CONV_CH), F32)],
        compiler_params=_cparams(("arbitrary",)),
        name="conv_group",
    )(proj, proj, dw_w, dw_b.reshape(1, -1), ln_g.reshape(1, -1), ln_b.reshape(1, -1))


def _compress_kernel(x_ref, pe_ref, w1_ref, w2_ref, o_ref):
    X = x_ref[0]
    nrow = X.shape[0]
    half = CMP_STRIDE * HD
    A = jnp.dot(X, w1_ref[0:half, :], precision=HI, preferred_element_type=F32)
    B = jnp.dot(X, w1_ref[half:2 * half, :], precision=HI, preferred_element_type=F32)
    pe8 = jnp.broadcast_to(pe_ref[...], (8, 2 * half))
    cst = jnp.dot(pe8, w1_ref[...], precision=HI, preferred_element_type=F32)[0:1]
    pre = A + pltpu.roll(B, nrow - 1, 0) + cst
    act = pre * _sigmoid(pre)
    out = jnp.dot(act, w2_ref[...], precision=HI, preferred_element_type=F32)
    rows = lax.broadcasted_iota(jnp.int32, out.shape, 0)
    o_ref[0] = jnp.where(rows < nrow - 1, out, 0.0)


def _compress(strips, pe, w1, w2):
    nh, nrow, width = strips.shape
    return pl.pallas_call(
        _compress_kernel,
        out_shape=jax.ShapeDtypeStruct((nh, nrow, HD), F32),
        grid=(nh,),
        in_specs=[pl.BlockSpec((1, nrow, width), lambda h: (h, 0, 0)),
                  pl.BlockSpec((1, width * 2), lambda h: (0, 0)),
                  pl.BlockSpec((width * 2, HD), lambda h: (0, 0)),
                  pl.BlockSpec((HD, HD), lambda h: (0, 0))],
        out_specs=pl.BlockSpec((1, nrow, HD), lambda h: (h, 0, 0)),
        compiler_params=_cparams(("arbitrary",)),
        name="nsa_compress",
    )(strips, pe.reshape(1, -1), w1, w2)


def _lane_pick(vals, lane, idx):
    return jnp.sum(jnp.where(lane == idx, vals, 0.0), axis=-1, keepdims=True)


def _cmp_kernel(q_ref, gt_ref, kc_ref, vc_ref, covt_ref, oc_ref, sb_ref, *, n_sel):
    k = pl.program_id(0)
    i = pl.program_id(1)
    tq = q_ref.shape[0]
    ncp = kc_ref.shape[1]
    scale = HD ** -0.5
    nt = (((1,), (1,)), ((), ()))
    tpos = i * tq + lax.broadcasted_iota(jnp.int32, (1, tq), 1)
    nrow = lax.broadcasted_iota(jnp.int32, (ncp, 1), 0)
    mc = (nrow * CMP_STRIDE + (CMP_BLOCK - 1)) <= tpos
    anyv = tpos >= (CMP_BLOCK - 1)
    kc = kc_ref[0]
    kc_hi = kc.astype(BF16)
    kc_lo = (kc - kc_hi.astype(F32)).astype(BF16)
    kc_cat = jnp.concatenate([kc_hi, kc_lo, kc_hi], axis=1)
    covt = covt_ref[...].astype(BF16)
    vct = vc_ref[0].T.astype(BF16)
    lane = lax.broadcasted_iota(jnp.int32, (tq, LANES), 1)
    sig = _sigmoid(gt_ref[...])
    psum = jnp.zeros((ncp, tq), F32)
    for g in range(NSA_GROUP):
        qg = q_ref[:, g * HD:(g + 1) * HD] * scale
        q_hi = qg.astype(BF16)
        q_lo = (qg - q_hi.astype(F32)).astype(BF16)
        s = lax.dot_general(kc_cat, jnp.concatenate([q_hi, q_hi, q_lo], axis=1), nt,
                            preferred_element_type=F32)
        s = jnp.where(mc, s, NEG)
        m = jnp.max(s, axis=0, keepdims=True)
        e = jnp.where(mc, jnp.exp(s - m), 0.0)
        l = jnp.sum(e, axis=0, keepdims=True)
        p = e / jnp.where(anyv, l, 1.0)
        ot = jnp.dot(vct, p.astype(BF16), preferred_element_type=F32)
        gate = _lane_pick(sig, lane, k * NSA_GROUP + g)
        oc_ref[:, g * HD:(g + 1) * HD] = gate * ot.T
        psum = psum + p
    p_hi = psum.astype(BF16)
    p_lo = (psum - p_hi.astype(F32)).astype(BF16)
    imp = (jnp.dot(covt, p_hi, preferred_element_type=F32)
           + jnp.dot(covt, p_lo, preferred_element_type=F32))
    blk = lax.broadcasted_iota(jnp.int32, (LANES, tq), 0)
    valid = (blk * SEL_BLOCK <= tpos) & (blk < n_sel)
    forced = (blk == 0) | (blk == tpos // SEL_BLOCK)
    work = jnp.where(forced, BIG, jnp.where(valid, imp, NEG))
    blk_f = blk.astype(F32)
    chosen = jnp.zeros((LANES, tq), F32)
    for _ in range(min(SEL_TOPN, n_sel)):
        m = jnp.max(work, axis=0, keepdims=True)
        first = jnp.min(jnp.where(work == m, blk_f, float(LANES)), axis=0, keepdims=True)
        hit = blk_f == first
        chosen = jnp.where(hit, 1.0, chosen)
        work = jnp.where(hit, -jnp.inf, work)
    keep = (chosen > 0.5) & valid
    sb_ref[0] = jnp.where(keep, 0.0, NEG).T.astype(BF16)


def _cmp_attention(proj, kc, vc, cover):
    S = proj.shape[0]
    tq = 256
    ncp = kc.shape[1]
    return pl.pallas_call(
        functools.partial(_cmp_kernel, n_sel=S // SEL_BLOCK),
        out_shape=(jax.ShapeDtypeStruct((S, NSA_HEADS * HD), F32),
                   jax.ShapeDtypeStruct((NSA_KV_HEADS, S, LANES), BF16)),
        grid=(NSA_KV_HEADS, S // tq),
        in_specs=[pl.BlockSpec((tq, NSA_GROUP * HD), lambda k, i: (i, CB_Q // NSA_GROUP + k)),
                  pl.BlockSpec((tq, LANES), lambda k, i: (i, CB_GT)),
                  pl.BlockSpec((1, ncp, HD), lambda k, i: (k, 0, 0)),
                  pl.BlockSpec((1, ncp, HD), lambda k, i: (k, 0, 0)),
                  pl.BlockSpec((LANES, ncp), lambda k, i: (0, 0))],
        out_specs=(pl.BlockSpec((tq, NSA_GROUP * HD), lambda k, i: (i, k)),
                   pl.BlockSpec((1, tq, LANES), lambda k, i: (k, i, 0))),
        compiler_params=_cparams(("arbitrary", "arbitrary")),
        name="nsa_cmp_select",
    )(proj, proj, kc, vc, cover)


def _kvprep_kernel(ks_ref, vs_ref, kw_ref, vw_ref, ksa_ref, vsb_ref, kwb_ref, vwb_ref):
    i = pl.program_id(1)
    ts = ks_ref.shape[0]
    rows = i * ts + lax.broadcasted_iota(jnp.int32, (ts, LANES), 0)
    lane = lax.broadcasted_iota(jnp.int32, (ts, LANES), 1)
    ksa_ref[0, :, 0:HD] = ks_ref[...].astype(BF16)
    ksa_ref[0, :, HD:2 * HD] = jnp.where(lane == rows // SEL_BLOCK, 1.0, 0.0).astype(BF16)
    vsb_ref[0, 0] = vs_ref[...].T.astype(BF16)
    kwb_ref[0] = kw_ref[...].astype(BF16)
    for t in range(ts // LANES):
        vwb_ref[0, t] = vw_ref[t * LANES:(t + 1) * LANES, :].T.astype(BF16)


def _kv_prep(proj):
    S = proj.shape[0]
    ts = NSA_TK
    nw = ts // LANES

    def col(cb):
        return pl.BlockSpec((ts, HD), lambda k, i: (i, cb + k))

    return pl.pallas_call(
        _kvprep_kernel,
        out_shape=(jax.ShapeDtypeStruct((NSA_KV_HEADS, S, 2 * HD), BF16),
                   jax.ShapeDtypeStruct((NSA_KV_HEADS, S // ts, HD, ts), BF16),
                   jax.ShapeDtypeStruct((NSA_KV_HEADS, S, HD), BF16),
                   jax.ShapeDtypeStruct((NSA_KV_HEADS, S // LANES, HD, LANES), BF16)),
        grid=(NSA_KV_HEADS, S // ts),
        in_specs=[col(CB_KS), col(CB_VS), col(CB_KW), col(CB_VW)],
        out_specs=(pl.BlockSpec((1, ts, 2 * HD), lambda k, i: (k, i, 0)),
                   pl.BlockSpec((1, 1, HD, ts), lambda k, i: (k, i, 0, 0)),
                   pl.BlockSpec((1, ts, HD), lambda k, i: (k, i, 0)),
                   pl.BlockSpec((1, nw, HD, LANES), lambda k, i: (k, i, 0, 0))),
        compiler_params=_cparams(("arbitrary", "arbitrary")),
        name="nsa_kv_prep",
    )(proj, proj, proj, proj)


NSA_TQ = 256
NSA_TK = 512
NSA_WK = WINDOW + NSA_TQ
NSA_CH = 32
NSA_PW = 256


def _nsa_kernel(q_ref, sb_ref, oc_ref, gt_ref, ks_ref, vst_ref, kw_ref, vwt_ref, y_ref, *scratch):
    k = pl.program_id(0)
    i = pl.program_id(1)
    tq = NSA_TQ
    cols = NSA_GROUP * tq
    q0 = i * tq
    nt = (((1,), (1,)), ((), ()))
    npart = cols // NSA_PW
    s_refs, p_refs, acc_refs = scratch[0:npart], scratch[npart:2 * npart], scratch[2 * npart:3 * npart]
    gpp = NSA_PW // LANES
    nsubq = tq // LANES
    qs = jnp.concatenate([q_ref[sub * LANES:(sub + 1) * LANES, g * HD:(g + 1) * HD]
                          for sub in range(nsubq) for g in range(NSA_GROUP)], axis=0)
    qs = (qs * (HD ** -0.5 * LOG2E)).astype(BF16)
    sbs = jnp.concatenate([sb_ref[0, sub * LANES:(sub + 1) * LANES, :]
                           for sub in range(nsubq) for _ in range(NSA_GROUP)], axis=0)
    qaug = jnp.concatenate([qs, sbs], axis=1)
    lane_q = lax.broadcasted_iota(jnp.int32, (1, LANES), 1)

    def qpos(c):
        return q0 + (c // NSA_GROUP) * LANES + lane_q

    def softmax_tile(n_keys, k0, m_old, mask_fn, c_list):
        def scores(r0, c):
            s = s_refs[c // gpp][r0:r0 + NSA_CH, (c % gpp) * LANES:(c % gpp + 1) * LANES]
            if mask_fn is None:
                return s
            kpos = k0 + r0 + lax.broadcasted_iota(jnp.int32, (NSA_CH, 1), 0)
            return jnp.where(mask_fn(kpos, qpos(c)), s, NEG)

        def fold(x):
            return x.reshape(NSA_CH // 8, 8, LANES)

        m_new, sums = [], []
        for c in c_list:
            mx8 = jnp.max(fold(scores(0, c)), axis=0)
            for r0 in range(NSA_CH, n_keys, NSA_CH):
                mx8 = jnp.maximum(mx8, jnp.max(fold(scores(r0, c)), axis=0))
            mx = jnp.max(mx8, axis=0, keepdims=True)
            mc = mx if m_old is None else jnp.maximum(m_old[:, c * LANES:(c + 1) * LANES], mx)
            tot8 = jnp.zeros((8, LANES), F32)
            for r0 in range(0, n_keys, NSA_CH):
                p = jnp.exp2(scores(r0, c) - mc)
                tot8 = tot8 + jnp.sum(fold(p), axis=0)
                p_refs[c // gpp][r0:r0 + NSA_CH, (c % gpp) * LANES:(c % gpp + 1) * LANES] = p.astype(BF16)
            m_new.append(mc)
            sums.append(jnp.sum(tot8, axis=0, keepdims=True))
        return m_new, sums

    for a in acc_refs:
        a[...] = jnp.zeros((HD, NSA_PW), F32)

    def sel_tile(j, carry, causal):
        m_old, l_old = carry
        k0 = pl.multiple_of(j * NSA_TK, NSA_TK)
        kt = ks_ref[0, pl.ds(k0, NSA_TK), :]
        vt = vst_ref[0, j]
        mask_fn = (lambda kpos, qp: kpos <= qp) if causal else None
        halves = [slice(h * NSA_PW, (h + 1) * NSA_PW) for h in range(npart)]
        for h, hs in enumerate(halves):
            s_refs[h][0:NSA_TK, :] = lax.dot_general(kt, qaug[hs], nt, preferred_element_type=F32)
        m_out, l_out = [], []
        for h, hs in enumerate(halves):
            m_new, sums = softmax_tile(NSA_TK, k0, m_old, mask_fn, list(range(h * gpp, (h + 1) * gpp)))
            m_new, sums = jnp.concatenate(m_new, axis=1), jnp.concatenate(sums, axis=1)
            alpha = jnp.exp2(m_old[:, hs] - m_new)
            acc_refs[h][...] = acc_refs[h][...] * alpha + jnp.dot(vt, p_refs[h][0:NSA_TK, :],
                                                                  preferred_element_type=F32)
            m_out.append(m_new)
            l_out.append(alpha * l_old[:, hs] + sums)
        return jnp.concatenate(m_out, axis=1), jnp.concatenate(l_out, axis=1)

    n_full = (q0 + tq - 1) // NSA_TK
    init = (jnp.full((1, cols), NEG, F32), jnp.zeros((1, cols), F32))
    carry = lax.fori_loop(0, n_full, lambda j, c: sel_tile(j, c, False), init)
    _, l_s = sel_tile(n_full, carry, True)
    ot_s = jnp.concatenate([a[...] for a in acc_refs], axis=1) / l_s

    w0 = pl.multiple_of(jnp.maximum(q0 - WINDOW, 0), tq)
    kwt = kw_ref[0, pl.ds(w0, NSA_WK), :]
    for h in range(npart):
        s_refs[h][...] = lax.dot_general(kwt, qs[h * NSA_PW:(h + 1) * NSA_PW], nt, preferred_element_type=F32)
    _, l_w = softmax_tile(NSA_WK, w0, None, lambda kpos, qp: (kpos <= qp) & (kpos > qp - WINDOW),
                          list(range(cols // LANES)))
    l_w = jnp.concatenate(l_w, axis=1)
    ot_w = []
    for h in range(npart):
        o = jnp.zeros((HD, NSA_PW), F32)
        for t in range(NSA_WK // LANES):
            o = o + jnp.dot(vwt_ref[0, w0 // LANES + t], p_refs[h][t * LANES:(t + 1) * LANES, :],
                            preferred_element_type=F32)
        ot_w.append(o)
    ot_w = jnp.concatenate(ot_w, axis=1) / l_w

    lane = lax.broadcasted_iota(jnp.int32, (LANES, LANES), 1)
    for sub in range(nsubq):
        rs = slice(sub * LANES, (sub + 1) * LANES)
        sig = _sigmoid(gt_ref[rs, :])
        for g in range(NSA_GROUP):
            head = k * NSA_GROUP + g
            cs = slice((sub * NSA_GROUP + g) * LANES, (sub * NSA_GROUP + g + 1) * LANES)
            g_s = _lane_pick(sig, lane, NSA_HEADS + head)
            g_w = _lane_pick(sig, lane, 2 * NSA_HEADS + head)
            y = oc_ref[rs, g * HD:(g + 1) * HD] + g_s * ot_s[:, cs].T + g_w * ot_w[:, cs].T
            y_ref[rs, g * HD:(g + 1) * HD] = y.astype(y_ref.dtype)


def _nsa_attention(proj, selbias, oc, ksa, vst, kwb, vwt):
    S = proj.shape[0]
    tq = NSA_TQ
    cols = NSA_GROUP * tq
    npart = cols // NSA_PW
    assert S >= NSA_WK and tq % LANES == 0 and NSA_TK % tq == 0

    def full(a):
        return pl.BlockSpec((1,) + a.shape[1:], lambda k, i: (k,) + (0,) * (a.ndim - 1))

    return pl.pallas_call(
        _nsa_kernel,
        out_shape=jax.ShapeDtypeStruct((S, NSA_HEADS * HD), BF16),
        grid=(NSA_KV_HEADS, S // tq),
        in_specs=[pl.BlockSpec((tq, NSA_GROUP * HD), lambda k, i: (i, CB_Q // NSA_GROUP + k)),
                  pl.BlockSpec((1, tq, LANES), lambda k, i: (k, i, 0)),
                  pl.BlockSpec((tq, NSA_GROUP * HD), lambda k, i: (i, k)),
                  pl.BlockSpec((tq, LANES), lambda k, i: (i, CB_GT)),
                  full(ksa), full(vst), full(kwb), full(vwt)],
        out_specs=pl.BlockSpec((tq, NSA_GROUP * HD), lambda k, i: (i, k)),
        scratch_shapes=([pltpu.VMEM((NSA_WK, NSA_PW), F32)] * npart + [pltpu.VMEM((NSA_WK, NSA_PW), BF16)] * npart
                        + [pltpu.VMEM((HD, NSA_PW), F32)] * npart),
        compiler_params=_cparams(("arbitrary", "arbitrary")),
        name="nsa_sel_win",
    )(proj, selbias, oc, proj, ksa, vst, kwb, vwt)


def _ret_kernel(q_ref, k_ref, v_ref, g_ref, dm_ref, ze_ref, xi_ref, dc_ref, gg_ref, gb_ref, o_ref, st_ref):
    n = pl.program_id(1)
    C = RET_CHUNK
    nt = (((1,), (1,)), ((), ()))
    tn = (((0,), (0,)), ((), ()))

    @pl.when(n == 0)
    def _():
        st_ref[...] = jnp.zeros_like(st_ref)

    dmat = dm_ref[0]
    zeta = ze_ref[0]
    xi = xi_ref[0]
    decay = dc_ref[0]
    for c in range(q_ref.shape[0] // C):
        sl = slice(c * C, (c + 1) * C)
        q = q_ref[sl, :]
        kk = k_ref[sl, :] * (HD ** -0.5)
        v = v_ref[sl, :]
        qb, kb, vb = q.astype(BF16), kk.astype(BF16), v.astype(BF16)
        inner = lax.dot_general(qb, kb, nt, preferred_element_type=F32) * dmat
        o = jnp.dot(inner.astype(BF16), vb, preferred_element_type=F32)
        state = st_ref[...]
        o = o + jnp.dot(qb, state.astype(BF16), preferred_element_type=F32) * xi
        kv = lax.dot_general((kk * zeta).astype(BF16), vb, tn, preferred_element_type=F32)
        st_ref[...] = decay * state + kv
        mu = jnp.mean(o, axis=-1, keepdims=True)
        var = jnp.mean(jnp.square(o - mu), axis=-1, keepdims=True)
        y = (o - mu) * lax.rsqrt(var + NORM_EPS) * gg_ref[...] + gb_ref[...]
        gt = g_ref[sl, :]
        o_ref[sl, :] = (y * (gt * _sigmoid(gt))).astype(o_ref.dtype)


def _retention(proj, gn_g, gn_b):
    S = proj.shape[0]
    C = RET_CHUNK
    ts = 1024 if S % 1024 == 0 else C
    H = RET_HEADS
    log_g = jnp.log(1.0 - 2.0 ** (-5.0 - jnp.arange(H, dtype=F32)))
    i = jnp.arange(C, dtype=F32)
    diff = i[:, None] - i[None, :]
    dmat = jnp.where(diff >= 0, jnp.exp(log_g[:, None, None] * jnp.maximum(diff, 0.0)), 0.0)
    zeta = jnp.exp(log_g[:, None] * (C - 1.0 - i))[:, :, None]
    xi = jnp.exp(log_g[:, None] * (i + 1.0))[:, :, None]
    decay = jnp.broadcast_to(jnp.exp(log_g * C)[:, None, None], (H, 1, HD))

    def col(cb):
        return pl.BlockSpec((ts, HD), lambda h, n: (n, cb + h))

    return pl.pallas_call(
        _ret_kernel,
        out_shape=jax.ShapeDtypeStruct((S, H * HD), BF16),
        grid=(H, S // ts),
        in_specs=[col(CB_RQ), col(CB_RK), col(CB_RV), col(CB_RG),
                  pl.BlockSpec((1, C, C), lambda h, n: (h, 0, 0)),
                  pl.BlockSpec((1, C, 1), lambda h, n: (h, 0, 0)),
                  pl.BlockSpec((1, C, 1), lambda h, n: (h, 0, 0)),
                  pl.BlockSpec((1, 1, HD), lambda h, n: (h, 0, 0)),
                  pl.BlockSpec((1, HD), lambda h, n: (0, h)),
                  pl.BlockSpec((1, HD), lambda h, n: (0, h))],
        out_specs=pl.BlockSpec((ts, HD), lambda h, n: (n, h)),
        scratch_shapes=[pltpu.VMEM((HD, HD), F32)],
        compiler_params=_cparams(("arbitrary", "arbitrary")),
        name="retention",
    )(proj, proj, proj, proj, dmat, zeta, xi, decay, gn_g.reshape(1, -1), gn_b.reshape(1, -1))


def _outproj_kernel(x_ref, yc_ref, yn_ref, yr_ref, wc_ref, wn_ref, wr_ref, ga_ref, o_ref):
    y = jnp.dot(yc_ref[...], wc_ref[...], preferred_element_type=F32)
    y = y + jnp.dot(yn_ref[...], wn_ref[...], preferred_element_type=F32)
    y = y + jnp.dot(yr_ref[...], wr_ref[...], preferred_element_type=F32)
    o_ref[...] = x_ref[...] + ga_ref[...] * y


def _out_proj(x2, y_conv, y_nsa, y_ret, w_out_bf, gate_a):
    S = x2.shape[0]
    tm = 512
    wc, wn, wr = w_out_bf[:CONV_CH], w_out_bf[CONV_CH:CONV_CH + NSA_HEADS * HD], w_out_bf[CONV_CH + NSA_HEADS * HD:]

    def rows(w):
        return pl.BlockSpec((tm, w), lambda i: (i, 0))

    def whole(a):
        return pl.BlockSpec(a.shape, lambda i: (0, 0))

    return pl.pallas_call(
        _outproj_kernel,
        out_shape=jax.ShapeDtypeStruct((S, D_MODEL), F32),
        grid=(S // tm,),
        in_specs=[rows(D_MODEL), rows(y_conv.shape[1]), rows(y_nsa.shape[1]), rows(y_ret.shape[1]),
                  whole(wc), whole(wn), whole(wr), pl.BlockSpec((1, D_MODEL), lambda i: (0, 0))],
        out_specs=rows(D_MODEL),
        compiler_params=_cparams(("arbitrary",)),
        name="out_proj",
    )(x2, y_conv, y_nsa, y_ret, wc, wn, wr, gate_a)


def _router_kernel(x_ref, g_ref, sc_ref, sh_ref, rw_ref, rb_ref, h_ref, rt_ref, cnt_ref, carry_ref):
    i = pl.program_id(0)
    tm = x_ref.shape[0]

    @pl.when(i == 0)
    def _():
        carry_ref[...] = jnp.zeros_like(carry_ref)

    h = _modulated_norm(x_ref[...], g_ref[...], sc_ref[...], sh_ref[...])
    half = D_MODEL // 2
    hi = lax.bitcast_convert_type(h[:, :half].astype(BF16).astype(F32), jnp.uint32)
    lo = lax.bitcast_convert_type(h[:, half:].astype(BF16).astype(F32), jnp.uint32)
    h_ref[...] = hi | (lo >> 16)
    rw = rw_ref[...]
    rw_hi = rw.astype(BF16)
    rw_lo = (rw - rw_hi.astype(F32)).astype(BF16)
    h_hi = h.astype(BF16)
    h_lo = (h - h_hi.astype(F32)).astype(BF16)
    logits = (jnp.dot(h_hi, rw_hi, preferred_element_type=F32) + jnp.dot(h_hi, rw_lo, preferred_element_type=F32)
              + jnp.dot(h_lo, rw_hi, preferred_element_type=F32) + rb_ref[...])
    lane = lax.broadcasted_iota(jnp.int32, (tm, LANES), 1)
    lane_f = lane.astype(F32)
    work = jnp.where(lane < N_EXPERTS, logits, -jnp.inf)
    onehot = jnp.zeros((tm, LANES), F32)
    vals, idxs = [], []
    for _ in range(TOP_K):
        m = jnp.max(work, axis=-1, keepdims=True)
        first = jnp.min(jnp.where(work == m, lane_f, float(LANES)), axis=-1, keepdims=True)
        hit = lane_f == first
        onehot = jnp.where(hit, 1.0, onehot)
        work = jnp.where(hit, -jnp.inf, work)
        vals.append(m)
        idxs.append(first)
    ex = [jnp.exp(v - vals[0]) for v in vals]
    den = ex[0] + ex[1] + ex[2] + ex[3]
    r = lax.broadcasted_iota(jnp.int32, (tm, tm), 0)
    c = lax.broadcasted_iota(jnp.int32, (tm, tm), 1)
    tri = jnp.where(c < r, 1.0, 0.0).astype(BF16)
    cum = jnp.dot(tri, onehot.astype(BF16), preferred_element_type=F32) + carry_ref[...]
    out = jnp.zeros((tm, LANES), F32)
    for kk in range(TOP_K):
        rank = jnp.sum(jnp.where(lane_f == idxs[kk], cum, 0.0), axis=-1, keepdims=True)
        out = jnp.where(lane == kk, idxs[kk], out)
        out = jnp.where(lane == TOP_K + kk, ex[kk] / den, out)
        out = jnp.where(lane == 2 * TOP_K + kk, rank, out)
    rt_ref[...] = out
    carry_ref[...] = carry_ref[...] + jnp.sum(onehot, axis=0, keepdims=True)
    cnt_ref[...] = carry_ref[...]


def _router(x2, g, scale, shift, router_w, router_b):
    T = x2.shape[0]
    tm = 256
    rw = jnp.pad(router_w, ((0, 0), (0, LANES - N_EXPERTS)))
    rb = jnp.pad(router_b, (0, LANES - N_EXPERTS)).reshape(1, LANES)
    row = pl.BlockSpec((1, D_MODEL), lambda i: (0, 0))
    return pl.pallas_call(
        _router_kernel,
        out_shape=(jax.ShapeDtypeStruct((T, D_MODEL // 2), jnp.uint32),
                   jax.ShapeDtypeStruct((T, LANES), F32),
                   jax.ShapeDtypeStruct((1, LANES), F32)),
        grid=(T // tm,),
        in_specs=[pl.BlockSpec((tm, D_MODEL), lambda i: (i, 0)), row, row, row,
                  pl.BlockSpec((D_MODEL, LANES), lambda i: (0, 0)),
                  pl.BlockSpec((1, LANES), lambda i: (0, 0))],
        out_specs=(pl.BlockSpec((tm, D_MODEL // 2), lambda i: (i, 0)),
                   pl.BlockSpec((tm, LANES), lambda i: (i, 0)),
                   pl.BlockSpec((1, LANES), lambda i: (0, 0))),
        scratch_shapes=[pltpu.VMEM((1, LANES), F32)],
        compiler_params=_cparams(("arbitrary",)),
        name="moe_router",
    )(x2, g, scale, shift, rw, rb)


GATHER_UNROLL = 8


def _expert_kernel(ie_ref, ir_ref, ins_ref, tok_ref, hp_hbm, wgl_ref, wli_ref, bgl_ref, bli_ref, wdn_ref, bdn_ref,
                   ys_hbm, stage, xbf, acc, sem_g, sem_out):
    w = pl.program_id(0)
    ct = pl.program_id(1)
    n_w = pl.num_programs(0)
    n_ct = pl.num_programs(1)
    nsub = ins_ref[w]
    rows = stage.shape[0]
    per_step = rows // MOE_NCT
    half = D_MODEL // 2

    def gather_row(item, r):
        t = tok_ref[ir_ref[item] + r]
        pltpu.make_async_copy(hp_hbm.at[pl.ds(t, 1)], stage.at[pl.ds(r, 1)], sem_g).start()

    def y_copy(item, sb):
        row = pl.multiple_of(ir_ref[item] + sb * MOE_SUB, MOE_SUB)
        return pltpu.make_async_copy(acc.at[pl.ds(sb * MOE_SUB, MOE_SUB)],
                                     ys_hbm.at[pl.ds(row, MOE_SUB)], sem_out.at[sb])

    def for_subs(item, fn):
        n = ins_ref[item]
        for sb in range(MOE_SUBMAX):
            @pl.when(sb < n)
            def _():
                fn(item, sb)

    @pl.when(nsub > 0)
    def _():
        @pl.when((ct == 0) & (w == 0))
        def _():
            def body(c, carry):
                for u in range(GATHER_UNROLL):
                    gather_row(w, c * GATHER_UNROLL + u)
                return carry

            lax.fori_loop(0, rows // GATHER_UNROLL, body, 0)

        @pl.when(ct == 0)
        def _():
            for sb in range(MOE_SUBMAX):
                pltpu.make_async_copy(hp_hbm.at[pl.ds(0, MOE_SUB)], stage.at[pl.ds(sb * MOE_SUB, MOE_SUB)],
                                      sem_g).wait()
            for sb in range(MOE_SUBMAX):
                wv = stage[sb * MOE_SUB:(sb + 1) * MOE_SUB, :]
                xbf[sb * MOE_SUB:(sb + 1) * MOE_SUB, 0:half] = lax.bitcast_convert_type(
                    wv & jnp.uint32(0xFFFF0000), F32).astype(BF16)
                xbf[sb * MOE_SUB:(sb + 1) * MOE_SUB, half:D_MODEL] = lax.bitcast_convert_type(
                    wv << 16, F32).astype(BF16)

        nxt = jnp.minimum(w + 1, n_w - 1)

        for c in range(MOE_NCT):
            @pl.when((w + 1 < n_w) & (ins_ref[nxt] > 0) & (ct == c))
            def _():
                for u in range(per_step):
                    gather_row(nxt, c * per_step + u)

        @pl.when((ct == 0) & (w > 0))
        def _():
            for_subs(jnp.maximum(w - 1, 0), lambda it, sb: y_copy(it, sb).wait())

        def ffn_rows(r0, m_rows):
            x = xbf[pl.ds(r0, m_rows), :]
            glu = jnp.dot(x, wgl_ref[0, 0].astype(BF16), preferred_element_type=F32) + bgl_ref[0]
            lin = jnp.dot(x, wli_ref[0, 0].astype(BF16), preferred_element_type=F32) + bli_ref[0]
            glu = jnp.minimum(glu, SWIGLU_LIMIT)
            lin = jnp.clip(lin, -SWIGLU_LIMIT, SWIGLU_LIMIT)
            act = glu * _sigmoid(SWIGLU_ALPHA * glu) * (lin + 1.0)
            y = jnp.dot(act.astype(BF16), wdn_ref[0, 0].astype(BF16), preferred_element_type=F32)

            @pl.when(ct == 0)
            def _():
                acc[pl.ds(r0, m_rows), :] = y + bdn_ref[0]

            @pl.when(ct > 0)
            def _():
                acc[pl.ds(r0, m_rows), :] += y

        def quad_body(qd, carry):
            ffn_rows(pl.multiple_of(qd * (4 * MOE_SUB), 4 * MOE_SUB), 4 * MOE_SUB)
            return carry

        lax.fori_loop(0, nsub // 4, quad_body, 0)

        @pl.when(nsub % 4 >= 2)
        def _():
            ffn_rows(pl.multiple_of((nsub // 4) * (4 * MOE_SUB), 2 * MOE_SUB), 2 * MOE_SUB)

        @pl.when(nsub % 2 == 1)
        def _():
            ffn_rows(pl.multiple_of((nsub - 1) * MOE_SUB, MOE_SUB), MOE_SUB)

        @pl.when(ct == n_ct - 1)
        def _():
            for_subs(w, lambda it, sb: y_copy(it, sb).start())
            last = (w == n_w - 1) | (ins_ref[nxt] == 0)

            @pl.when(last)
            def _():
                for_subs(w, lambda it, sb: y_copy(it, sb).wait())


def _experts(hp, slot_tok, item_e, item_row0, item_nsub, w_gu, b_gu, w_dn, b_dn, layer):
    W = item_e.shape[0]
    te = MOE_TE
    n_ct = MOE_NCT
    rows = MOE_SUBMAX * MOE_SUB
    n_slots = slot_tok.shape[0] - rows
    assert rows % n_ct == 0

    def ct_eff(w, ct, ins):
        return jnp.where(ins[w] > 0, ct, n_ct - 1)

    return pl.pallas_call(
        _expert_kernel,
        out_shape=jax.ShapeDtypeStruct((n_slots, D_MODEL), F32),
        grid_spec=pltpu.PrefetchScalarGridSpec(
            num_scalar_prefetch=4,
            grid=(W, n_ct),
            in_specs=[pl.BlockSpec(memory_space=pl.ANY),
                      pl.BlockSpec((1, 1, D_MODEL, te),
                                   lambda w, ct, ie, ir, ins, tok: (layer, ie[w], 0, ct_eff(w, ct, ins))),
                      pl.BlockSpec((1, 1, D_MODEL, te),
                                   lambda w, ct, ie, ir, ins, tok: (layer, ie[w], 0, n_ct + ct_eff(w, ct, ins))),
                      pl.BlockSpec((1, 1, te), lambda w, ct, ie, ir, ins, tok: (ie[w], 0, ct_eff(w, ct, ins))),
                      pl.BlockSpec((1, 1, te), lambda w, ct, ie, ir, ins, tok: (ie[w], 0, n_ct + ct_eff(w, ct, ins))),
                      pl.BlockSpec((1, 1, te, D_MODEL),
                                   lambda w, ct, ie, ir, ins, tok: (layer, ie[w], ct_eff(w, ct, ins), 0)),
                      pl.BlockSpec((1, 1, D_MODEL), lambda w, ct, ie, ir, ins, tok: (ie[w], 0, 0))],
            out_specs=pl.BlockSpec(memory_space=pl.ANY),
            scratch_shapes=[pltpu.VMEM((rows, D_MODEL // 2), jnp.uint32), pltpu.VMEM((rows, D_MODEL), BF16),
                            pltpu.VMEM((rows, D_MODEL), F32),
                            pltpu.SemaphoreType.DMA(()), pltpu.SemaphoreType.DMA((MOE_SUBMAX,))]),
        compiler_params=_cparams(("arbitrary", "arbitrary"), vmem=60 * 1024 * 1024),
        name="moe_experts",
    )(item_e, item_row0, item_nsub, slot_tok, hp, w_gu, w_gu, b_gu.reshape(N_EXPERTS, 1, -1),
      b_gu.reshape(N_EXPERTS, 1, -1), w_dn, b_dn.reshape(N_EXPERTS, 1, -1))


def _combine_kernel(dest_ref, x_ref, rt_ref, gf_ref, fg_ref, ys_hbm, o_ref, buf, sem, *, final_norm):
    i = pl.program_id(0)
    n = pl.num_programs(0)
    tm = x_ref.shape[0]

    def issue(tile):
        slot = tile % 2

        base = tile * (tm * TOP_K)
        for r in range(tm):
            for kk in range(TOP_K):
                d = dest_ref[base + r * TOP_K + kk]
                pltpu.make_async_copy(ys_hbm.at[pl.ds(d, 1)], buf.at[slot, kk, pl.ds(r, 1)],
                                      sem.at[slot]).start()

    @pl.when(i == 0)
    def _():
        issue(i)

    @pl.when(i + 1 < n)
    def _():
        issue(i + 1)

    slot = i % 2
    for kk in range(TOP_K):
        pltpu.make_async_copy(ys_hbm.at[pl.ds(0, tm)], buf.at[slot, kk], sem.at[slot]).wait()
    rt = rt_ref[...]
    lane = lax.broadcasted_iota(jnp.int32, rt.shape, 1)
    moe = jnp.zeros((tm, D_MODEL), F32)
    for kk in range(TOP_K):
        moe = moe + buf[slot, kk] * _lane_pick(rt, lane, TOP_K + kk)
    out = x_ref[...] + gf_ref[...] * moe
    if final_norm:
        ms = jnp.mean(out * out, axis=-1, keepdims=True)
        out = out * lax.rsqrt(ms + NORM_EPS) * fg_ref[...]
    o_ref[...] = out


def _combine(x2, route, gate_f, final_g, ys, dest_flat, final_norm):
    T = x2.shape[0]
    tm = 128
    row = pl.BlockSpec((1, D_MODEL), lambda i, d: (0, 0))
    return pl.pallas_call(
        functools.partial(_combine_kernel, final_norm=final_norm),
        out_shape=jax.ShapeDtypeStruct((T, D_MODEL), F32),
        grid_spec=pltpu.PrefetchScalarGridSpec(
            num_scalar_prefetch=1,
            grid=(T // tm,),
            in_specs=[pl.BlockSpec((tm, D_MODEL), lambda i, d: (i, 0)),
                      pl.BlockSpec((tm, LANES), lambda i, d: (i, 0)), row, row,
                      pl.BlockSpec(memory_space=pl.ANY)],
            out_specs=pl.BlockSpec((tm, D_MODEL), lambda i, d: (i, 0)),
            scratch_shapes=[pltpu.VMEM((2, TOP_K, tm, D_MODEL), F32), pltpu.SemaphoreType.DMA((2,))]),
        compiler_params=_cparams(("arbitrary",)),
        name="moe_combine",
    )(dest_flat, x2, route, gate_f, final_g, ys)


def _moe_plan(route, counts_f):
    T = route.shape[0]
    e = route[:, 0:TOP_K].astype(jnp.int32)
    rank = route[:, 2 * TOP_K:3 * TOP_K].astype(jnp.int32)
    counts = counts_f[0, :N_EXPERTS].astype(jnp.int32)
    nsub = (counts + MOE_SUB - 1) // MOE_SUB
    padded = nsub * MOE_SUB
    pend = jnp.cumsum(padded)
    pstart = pend - padded
    dest = pstart[e] + rank
    n_slots = (T * TOP_K + MOE_SUB - 1) // MOE_SUB * MOE_SUB + N_EXPERTS * MOE_SUB
    tok = jnp.repeat(jnp.arange(T, dtype=jnp.int32), TOP_K)
    slot_tok = jnp.zeros((n_slots + MOE_SUBMAX * MOE_SUB,), jnp.int32).at[dest.reshape(-1)].set(tok)
    n_used = (pend[-1] // MOE_SUB).astype(jnp.int32).reshape(1)
    n_items_max = N_EXPERTS + (n_slots // MOE_SUB) // MOE_SUBMAX
    per_e = (nsub + MOE_SUBMAX - 1) // MOE_SUBMAX
    iend = jnp.cumsum(per_e)
    w = jnp.arange(n_items_max, dtype=jnp.int32)
    ew = jnp.minimum(jnp.searchsorted(iend, w, side='right'), N_EXPERTS - 1).astype(jnp.int32)
    local = w - (iend[ew] - per_e[ew])
    live = w < iend[-1]
    item_nsub = jnp.where(live, jnp.clip(nsub[ew] - local * MOE_SUBMAX, 0, MOE_SUBMAX), 0).astype(jnp.int32)
    item_row0 = jnp.where(live, pstart[ew] + local * (MOE_SUBMAX * MOE_SUB), 0).astype(jnp.int32)
    last_e = ew[jnp.maximum(iend[-1] - 1, 0)]
    item_e = jnp.where(live, ew, last_e).astype(jnp.int32)
    return dest.reshape(-1).astype(jnp.int32), slot_tok, n_used, item_e, item_row0, item_nsub


def _repack_w_in(w_in_l):
    parts, width = [], 0
    for name in _ORDER:
        off, size = _SRC[name]
        parts.append(w_in_l[:, off:off + size])
        width += size
        if size % LANES:
            parts.append(jnp.zeros((D_MODEL, LANES - size % LANES), w_in_l.dtype))
            width += LANES - size % LANES
    parts.append(jnp.zeros((D_MODEL, PROJ_W - width), w_in_l.dtype))
    return jnp.concatenate(parts, axis=1).astype(BF16)


def _cover_matrix(S):
    n_strip = S // CMP_STRIDE
    n_cmp = (S - CMP_BLOCK) // CMP_STRIDE + 1
    n = np.arange(n_strip)[:, None]
    j = np.arange(LANES)[None, :]
    start, end = n * CMP_STRIDE, n * CMP_STRIDE + CMP_BLOCK - 1
    cov = (start <= j * SEL_BLOCK + SEL_BLOCK - 1) & (end >= j * SEL_BLOCK) & (n < n_cmp) & (j < S // SEL_BLOCK)
    return jnp.asarray(cov.astype(np.float32).T)


def _strips(proj, cb):
    S = proj.shape[0]
    t = proj[:, cb * LANES:(cb + NSA_KV_HEADS) * LANES].reshape(S, NSA_KV_HEADS, HD)
    return t.transpose(1, 0, 2).reshape(NSA_KV_HEADS, S // CMP_STRIDE, CMP_STRIDE * HD)


def kernel(x, c, ada_w, ada_b, norm_mix_g, w_in, conv_dw_w, conv_dw_b, conv_ln_g, conv_ln_b, nsa_pe_k, nsa_pe_v, nsa_cmp_k_w1, nsa_cmp_k_w2, nsa_cmp_v_w1, nsa_cmp_v_w2, ret_gn_g, ret_gn_b, w_out, norm_ffn_g, router_w, router_b, moe_w_gate_up, moe_b_gate_up, moe_w_down, moe_b_down, final_norm_g):
    B, S, _ = x.shape
    assert B == 1 and c.shape[0] == 1
    x2 = x.reshape(S, D_MODEL)
    mod = _ada_mod(c, ada_w, ada_b).reshape(DEPTH, 6, 1, D_MODEL)
    tabs = _rotary_tables(S)
    cover = _cover_matrix(S)
    final_g = final_norm_g.reshape(1, D_MODEL)
    for l in range(DEPTH):
        shift_a, scale_a, gate_a, shift_f, scale_f, gate_f = [mod[l, i] for i in range(6)]
        proj = _in_proj(x2, norm_mix_g[l].reshape(1, -1), scale_a, shift_a, _repack_w_in(w_in[l]), tabs)
        y_conv = _conv_group(proj, conv_dw_w[l], conv_dw_b[l], conv_ln_g[l], conv_ln_b[l])
        kc = _compress(_strips(proj, CB_KC), nsa_pe_k[l], nsa_cmp_k_w1[l], nsa_cmp_k_w2[l])
        vc = _compress(_strips(proj, CB_VC), nsa_pe_v[l], nsa_cmp_v_w1[l], nsa_cmp_v_w2[l])
        oc, selbias = _cmp_attention(proj, kc, vc, cover)
        ksa, vsb, kwb, vwb = _kv_prep(proj)
        y_nsa = _nsa_attention(proj, selbias, oc, ksa, vsb, kwb, vwb)
        y_ret = _retention(proj, ret_gn_g[l], ret_gn_b[l])
        x2 = _out_proj(x2, y_conv, y_nsa, y_ret, w_out[l].astype(BF16), gate_a)
        h2, route, counts = _router(x2, norm_ffn_g[l].reshape(1, -1), scale_f, shift_f, router_w[l], router_b[l])
        dest, slot_tok, n_used, item_e, item_row0, item_nsub = _moe_plan(route, counts)
        ys = _experts(h2, slot_tok, item_e, item_row0, item_nsub, moe_w_gate_up, moe_b_gate_up[l], moe_w_down,
                      moe_b_down[l], l)
        x2 = _combine(x2, route, gate_f, final_g, ys, dest, final_norm=(l == DEPTH - 1))
    return x2.reshape(B, S, D_MODEL)
```

```python
import functools
import math

import numpy as np
import jax
import jax.numpy as jnp
from jax import lax
from jax.experimental import pallas as pl
from jax.experimental.pallas import tpu as pltpu

F32 = jnp.float32
BF16 = jnp.bfloat16
HI = lax.Precision.HIGHEST

D_MODEL = 2048
DEPTH = 2
CONV_CH = 512
CONV_WIDTH = 31
HD = 128
NSA_HEADS = 8
NSA_KV_HEADS = 2
NSA_GROUP = 4
CMP_BLOCK = 32
CMP_STRIDE = 16
SEL_BLOCK = 64
SEL_TOPN = 16
WINDOW = 512
ROPE_THETA = 500000.0
ROPE_DIM = 32
RET_HEADS = 4
RET_CHUNK = 128
RET_THETA = 10000.0
N_EXPERTS = 32
TOP_K = 4
D_EXPERT = 2048
SWIGLU_LIMIT = 7.0
SWIGLU_ALPHA = 1.702
NORM_EPS = 1e-6
NEG = -1e30
BIG = 1e30
LOG2E = math.log2(math.e)

LANES = 128
VMEM_LIMIT = 56 * 1024 * 1024

CB_Q, CB_KC, CB_KS, CB_KW = 0, 8, 10, 12
CB_RQ, CB_RK = 14, 18
CB_VC, CB_CV, CB_CG, CB_VS, CB_VW, CB_GT, CB_RV, CB_RG = 22, 24, 28, 32, 34, 36, 37, 41
PROJ_BLOCKS = 46
PROJ_W = PROJ_BLOCKS * LANES
PROJ_TN = 256
NSA_ROT_TILES = CB_RQ * LANES // PROJ_TN
RET_ROT_TILES = CB_VC * LANES // PROJ_TN
_SRC = {'cv': (0, 512), 'cg': (512, 512), 'q': (1024, 1024), 'kc': (2048, 256), 'vc': (2304, 256),
        'ks': (2560, 256), 'vs': (2816, 256), 'kw': (3072, 256), 'vw': (3328, 256), 'gt': (3584, 24),
        'rq': (3608, 512), 'rk': (4120, 512), 'rv': (4632, 512), 'rg': (5144, 512)}
_ORDER = ['q', 'kc', 'ks', 'kw', 'rq', 'rk', 'vc', 'cv', 'cg', 'vs', 'vw', 'gt', 'rv', 'rg']

MOE_SUB = 256
MOE_SUBMAX = 5
MOE_TE = 512
MOE_NCT = D_EXPERT // MOE_TE


def _sigmoid(x):
    return 1.0 / (1.0 + jnp.exp(-x))


def _cparams(sem, vmem=VMEM_LIMIT):
    return pltpu.CompilerParams(dimension_semantics=sem, vmem_limit_bytes=vmem)


def _ada_kernel(c_ref, w_ref, b_ref, o_ref, sc_ref):
    @pl.when(pl.program_id(1) == 0)
    def _():
        cv = c_ref[...]
        sc_ref[...] = cv * _sigmoid(cv)

    tn = o_ref.shape[-1]

    def body(i, acc):
        k0 = pl.multiple_of(i * 64, 64)
        p = w_ref[0, pl.ds(k0, 64), :] * sc_ref[pl.ds(k0, 64), :]
        return acc + p.reshape(8, 8, tn).sum(axis=0)

    acc = lax.fori_loop(0, D_MODEL // 64, body, jnp.zeros((8, tn), F32))
    o_ref[0] = jnp.sum(acc, axis=0, keepdims=True) + b_ref[0]


def _ada_mod(c, ada_w, ada_b):
    tn = 1024
    n = 6 * D_MODEL
    return pl.pallas_call(
        _ada_kernel,
        out_shape=jax.ShapeDtypeStruct((DEPTH, 1, n), F32),
        grid=(DEPTH, n // tn),
        in_specs=[pl.BlockSpec((D_MODEL, 1), lambda l, j: (0, 0)),
                  pl.BlockSpec((1, D_MODEL, tn), lambda l, j: (l, 0, j)),
                  pl.BlockSpec((1, 1, tn), lambda l, j: (l, 0, j))],
        out_specs=pl.BlockSpec((1, 1, tn), lambda l, j: (l, 0, j)),
        scratch_shapes=[pltpu.VMEM((D_MODEL, 1), F32)],
        compiler_params=_cparams(("arbitrary", "arbitrary")),
        name="ada_mod",
    )(c.reshape(D_MODEL, 1), ada_w, ada_b.reshape(DEPTH, 1, n))


def _modulated_norm(x, g, scale, shift):
    ms = jnp.mean(x * x, axis=-1, keepdims=True)
    return x * lax.rsqrt(ms + NORM_EPS) * g * (1.0 + scale) + shift


def _inproj_kernel(x_ref, g_ref, sc_ref, sh_ref, w_ref, cn_ref, s1_ref, s2_ref, cr_ref, sr_ref,
                   o_ref, h_ref):
    j = pl.program_id(1)
    nsub = o_ref.shape[-1] // LANES

    @pl.when(j == 0)
    def _():
        h_ref[...] = _modulated_norm(x_ref[...], g_ref[...], sc_ref[...], sh_ref[...]).astype(BF16)

    def matmul():
        return jnp.dot(h_ref[...], w_ref[...], preferred_element_type=F32)

    @pl.when(j < NSA_ROT_TILES)
    def _():
        acc = matmul()
        for c in range(nsub):
            sub = acc[:, c * LANES:(c + 1) * LANES]
            o_ref[:, c * LANES:(c + 1) * LANES] = (
                sub * cn_ref[...] + pltpu.roll(sub, LANES - ROPE_DIM // 2, 1) * s1_ref[...]
                + pltpu.roll(sub, ROPE_DIM // 2, 1) * s2_ref[...])

    @pl.when((j >= NSA_ROT_TILES) & (j < RET_ROT_TILES))
    def _():
        acc = matmul()
        for c in range(nsub):
            sub = acc[:, c * LANES:(c + 1) * LANES]
            o_ref[:, c * LANES:(c + 1) * LANES] = sub * cr_ref[...] + pltpu.roll(sub, HD // 2, 1) * sr_ref[...]

    @pl.when(j >= RET_ROT_TILES)
    def _():
        o_ref[...] = matmul()


def _in_proj(x2, g, scale, shift, w_bf, tabs):
    S = x2.shape[0]
    tm, tn = 1024, PROJ_TN
    row = pl.BlockSpec((1, D_MODEL), lambda i, j: (0, 0))
    tab = pl.BlockSpec((tm, LANES), lambda i, j: (i, 0))
    return pl.pallas_call(
        _inproj_kernel,
        out_shape=jax.ShapeDtypeStruct((S, PROJ_W), F32),
        grid=(S // tm, PROJ_W // tn),
        in_specs=[pl.BlockSpec((tm, D_MODEL), lambda i, j: (i, 0)), row, row, row,
                  pl.BlockSpec((D_MODEL, tn), lambda i, j: (0, j)), tab, tab, tab, tab, tab],
        out_specs=pl.BlockSpec((tm, tn), lambda i, j: (i, j)),
        scratch_shapes=[pltpu.VMEM((tm, D_MODEL), BF16)],
        compiler_params=_cparams(("arbitrary", "arbitrary")),
        name="in_proj",
    )(x2, g, scale, shift, w_bf, *tabs)


def _rotary_tables(S):
    pos = jnp.arange(S, dtype=F32)[:, None]
    half = ROPE_DIM // 2
    inv = ROPE_THETA ** (-jnp.arange(half, dtype=F32) * 2.0 / ROPE_DIM)
    ang = pos * inv[None, :]
    cos, sin = jnp.cos(ang), jnp.sin(ang)
    ones = jnp.ones((S, LANES - ROPE_DIM), F32)
    zeros_r = jnp.zeros((S, LANES - ROPE_DIM), F32)
    zeros_h = jnp.zeros((S, half), F32)
    cn = jnp.concatenate([cos, cos, ones], axis=1)
    s1 = jnp.concatenate([-sin, zeros_h, zeros_r], axis=1)
    s2 = jnp.concatenate([zeros_h, sin, zeros_r], axis=1)
    halfr = HD // 2
    invr = RET_THETA ** (-jnp.arange(halfr, dtype=F32) * 2.0 / HD)
    angr = pos * invr[None, :]
    cosr, sinr = jnp.cos(angr), jnp.sin(angr)
    cr = jnp.concatenate([cosr, cosr], axis=1)
    sr = jnp.concatenate([-sinr, sinr], axis=1)
    return cn, s1, s2, cr, sr


def _conv_kernel(cv_ref, cg_ref, dw_ref, db_ref, lg_ref, lb_ref, o_ref, hb_ref, xs_ref):
    i = pl.program_id(0)
    ts = o_ref.shape[0]
    halo = 32

    @pl.when(i == 0)
    def _():
        hb_ref[0:halo, :] = jnp.zeros((halo, CONV_CH), F32)

    @pl.when(i > 0)
    def _():
        hb_ref[0:halo, :] = hb_ref[ts:ts + halo, :]

    hb_ref[halo:halo + ts, :] = cv_ref[...] * _sigmoid(cg_ref[...])
    acc = jnp.zeros((ts, CONV_CH), F32) + db_ref[...]
    first = halo - (CONV_WIDTH - 1)
    for sh in range(8):
        offs = [first + w for w in range(CONV_WIDTH) if (first + w) % 8 == sh]
        span = max(offs) - sh + ts
        xs_ref[sh, 0:span, :] = hb_ref[sh:sh + span, :]
        for off in offs:
            acc = acc + xs_ref[sh, off - sh:off - sh + ts, :] * dw_ref[off - first:off - first + 1, :]
    mu = jnp.mean(acc, axis=-1, keepdims=True)
    var = jnp.mean(jnp.square(acc - mu), axis=-1, keepdims=True)
    y = (acc - mu) * lax.rsqrt(var + NORM_EPS) * lg_ref[...] + lb_ref[...]
    o_ref[...] = (y * _sigmoid(y)).astype(o_ref.dtype)


def _conv_group(proj, dw_w, dw_b, ln_g, ln_b):
    S = proj.shape[0]
    ts = 256
    vec = pl.BlockSpec((1, CONV_CH), lambda i: (0, 0))
    return pl.pallas_call(
        _conv_kernel,
        out_shape=jax.ShapeDtypeStruct((S, CONV_CH), BF16),
        grid=(S // ts,),
        in_specs=[pl.BlockSpec((ts, CONV_CH), lambda i: (i, CB_CV * LANES // CONV_CH)),
                  pl.BlockSpec((ts, CONV_CH), lambda i: (i, CB_CG * LANES // CONV_CH)),
                  pl.BlockSpec((CONV_WIDTH, CONV_CH), lambda i: (0, 0)), vec, vec, vec],
        out_specs=pl.BlockSpec((ts, CONV_CH), lambda i: (i, 0)),
        scratch_shapes=[pltpu.VMEM((ts + 32, CONV_CH), F32), pltpu.VMEM((8, ts + 32, CONV_CH), F32)],
        compiler_params=_cparams(("arbitrary",)),
        name="conv_group",
    )(proj, proj, dw_w, dw_b.reshape(1, -1), ln_g.reshape(1, -1), ln_b.reshape(1, -1))


def _compress_kernel(x_ref, pe_ref, w1_ref, w2_ref, o_ref):
    X = x_ref[0]
    nrow = X.shape[0]
    half = CMP_STRIDE * HD
    A = jnp.dot(X, w1_ref[0:half, :], precision=HI, preferred_element_type=F32)
    B = jnp.dot(X, w1_ref[half:2 * half, :], precision=HI, preferred_element_type=F32)
    pe8 = jnp.broadcast_to(pe_ref[...], (8, 2 * half))
    cst = jnp.dot(pe8, w1_ref[...], precision=HI, preferred_element_type=F32)[0:1]
    pre = A + pltpu.roll(B, nrow - 1, 0) + cst
    act = pre * _sigmoid(pre)
    out = jnp.dot(act, w2_ref[...], precision=HI, preferred_element_type=F32)
    rows = lax.broadcasted_iota(jnp.int32, out.shape, 0)
    o_ref[0] = jnp.where(rows < nrow - 1, out, 0.0)


def _compress(strips, pe, w1, w2):
    nh, nrow, width = strips.shape
    return pl.pallas_call(
        _compress_kernel,
        out_shape=jax.ShapeDtypeStruct((nh, nrow, HD), F32),
        grid=(nh,),
        in_specs=[pl.BlockSpec((1, nrow, width), lambda h: (h, 0, 0)),
                  pl.BlockSpec((1, width * 2), lambda h: (0, 0)),
                  pl.BlockSpec((width * 2, HD), lambda h: (0, 0)),
                  pl.BlockSpec((HD, HD), lambda h: (0, 0))],
        out_specs=pl.BlockSpec((1, nrow, HD), lambda h: (h, 0, 0)),
        compiler_params=_cparams(("arbitrary",)),
        name="nsa_compress",
    )(strips, pe.reshape(1, -1), w1, w2)


def _lane_pick(vals, lane, idx):
    return jnp.sum(jnp.where(lane == idx, vals, 0.0), axis=-1, keepdims=True)


def _cmp_kernel(q_ref, gt_ref, kc_ref, vc_ref, covt_ref, oc_ref, sb_ref, *, n_sel):
    k = pl.program_id(0)
    i = pl.program_id(1)
    tq = q_ref.shape[0]
    ncp = kc_ref.shape[1]
    scale = HD ** -0.5
    nt = (((1,), (1,)), ((), ()))
    tpos = i * tq + lax.broadcasted_iota(jnp.int32, (1, tq), 1)
    nrow = lax.broadcasted_iota(jnp.int32, (ncp, 1), 0)
    mc = (nrow * CMP_STRIDE + (CMP_BLOCK - 1)) <= tpos
    anyv = tpos >= (CMP_BLOCK - 1)
    kc = kc_ref[0]
    kc_hi = kc.astype(BF16)
    kc_lo = (kc - kc_hi.astype(F32)).astype(BF16)
    kc_cat = jnp.concatenate([kc_hi, kc_lo, kc_hi], axis=1)
    covt = covt_ref[...].astype(BF16)
    vct = vc_ref[0].T.astype(BF16)
    lane = lax.broadcasted_iota(jnp.int32, (tq, LANES), 1)
    sig = _sigmoid(gt_ref[...])
    psum = jnp.zeros((ncp, tq), F32)
    for g in range(NSA_GROUP):
        qg = q_ref[:, g * HD:(g + 1) * HD] * scale
        q_hi = qg.astype(BF16)
        q_lo = (qg - q_hi.astype(F32)).astype(BF16)
        s = lax.dot_general(kc_cat, jnp.concatenate([q_hi, q_hi, q_lo], axis=1), nt,
                            preferred_element_type=F32)
        s = jnp.where(mc, s, NEG)
        m = jnp.max(s, axis=0, keepdims=True)
        e = jnp.where(mc, jnp.exp(s - m), 0.0)
        l = jnp.sum(e, axis=0, keepdims=True)
        p = e / jnp.where(anyv, l, 1.0)
        ot = jnp.dot(vct, p.astype(BF16), preferred_element_type=F32)
        gate = _lane_pick(sig, lane, k * NSA_GROUP + g)
        oc_ref[:, g * HD:(g + 1) * HD] = gate * ot.T
        psum = psum + p
    p_hi = psum.astype(BF16)
    p_lo = (psum - p_hi.astype(F32)).astype(BF16)
    imp = (jnp.dot(covt, p_hi, preferred_element_type=F32)
           + jnp.dot(covt, p_lo, preferred_element_type=F32))
    blk = lax.broadcasted_iota(jnp.int32, (LANES, tq), 0)
    valid = (blk * SEL_BLOCK <= tpos) & (blk < n_sel)
    forced = (blk == 0) | (blk == tpos // SEL_BLOCK)
    work = jnp.where(forced, BIG, jnp.where(valid, imp, NEG))
    blk_f = blk.astype(F32)
    chosen = jnp.zeros((LANES, tq), F32)
    for _ in range(min(SEL_TOPN, n_sel)):
        m = jnp.max(work, axis=0, keepdims=True)
        first = jnp.min(jnp.where(work == m, blk_f, float(LANES)), axis=0, keepdims=True)
        hit = blk_f == first
        chosen = jnp.where(hit, 1.0, chosen)
        work = jnp.where(hit, -jnp.inf, work)
    keep = (chosen > 0.5) & valid
    sb_ref[0] = jnp.where(keep, 0.0, NEG).T.astype(BF16)


def _cmp_attention(proj, kc, vc, cover):
    S = proj.shape[0]
    tq = 256
    ncp = kc.shape[1]
    return pl.pallas_call(
        functools.partial(_cmp_kernel, n_sel=S // SEL_BLOCK),
        out_shape=(jax.ShapeDtypeStruct((S, NSA_HEADS * HD), F32),
                   jax.ShapeDtypeStruct((NSA_KV_HEADS, S, LANES), BF16)),
        grid=(NSA_KV_HEADS, S // tq),
        in_specs=[pl.BlockSpec((tq, NSA_GROUP * HD), lambda k, i: (i, CB_Q // NSA_GROUP + k)),
                  pl.BlockSpec((tq, LANES), lambda k, i: (i, CB_GT)),
                  pl.BlockSpec((1, ncp, HD), lambda k, i: (k, 0, 0)),
                  pl.BlockSpec((1, ncp, HD), lambda k, i: (k, 0, 0)),
                  pl.BlockSpec((LANES, ncp), lambda k, i: (0, 0))],
        out_specs=(pl.BlockSpec((tq, NSA_GROUP * HD), lambda k, i: (i, k)),
                   pl.BlockSpec((1, tq, LANES), lambda k, i: (k, i, 0))),
        compiler_params=_cparams(("arbitrary", "arbitrary")),
        name="nsa_cmp_select",
    )(proj, proj, kc, vc, cover)


def _kvprep_kernel(ks_ref, vs_ref, kw_ref, vw_ref, ksa_ref, vsb_ref, kwb_ref, vwb_ref):
    i = pl.program_id(1)
    ts = ks_ref.shape[0]
    rows = i * ts + lax.broadcasted_iota(jnp.int32, (ts, LANES), 0)
    lane = lax.broadcasted_iota(jnp.int32, (ts, LANES), 1)
    ksa_ref[0, :, 0:HD] = ks_ref[...].astype(BF16)
    ksa_ref[0, :, HD:2 * HD] = jnp.where(lane == rows // SEL_BLOCK, 1.0, 0.0).astype(BF16)
    vsb_ref[0, 0] = vs_ref[...].T.astype(BF16)
    kwb_ref[0] = kw_ref[...].astype(BF16)
    for t in range(ts // LANES):
        vwb_ref[0, t] = vw_ref[t * LANES:(t + 1) * LANES, :].T.astype(BF16)


def _kv_prep(proj):
    S = proj.shape[0]
    ts = NSA_TK
    nw = ts // LANES

    def col(cb):
        return pl.BlockSpec((ts, HD), lambda k, i: (i, cb + k))

    return pl.pallas_call(
        _kvprep_kernel,
        out_shape=(jax.ShapeDtypeStruct((NSA_KV_HEADS, S, 2 * HD), BF16),
                   jax.ShapeDtypeStruct((NSA_KV_HEADS, S // ts, HD, ts), BF16),
                   jax.ShapeDtypeStruct((NSA_KV_HEADS, S, HD), BF16),
                   jax.ShapeDtypeStruct((NSA_KV_HEADS, S // LANES, HD, LANES), BF16)),
        grid=(NSA_KV_HEADS, S // ts),
        in_specs=[col(CB_KS), col(CB_VS), col(CB_KW), col(CB_VW)],
        out_specs=(pl.BlockSpec((1, ts, 2 * HD), lambda k, i: (k, i, 0)),
                   pl.BlockSpec((1, 1, HD, ts), lambda k, i: (k, i, 0, 0)),
                   pl.BlockSpec((1, ts, HD), lambda k, i: (k, i, 0)),
                   pl.BlockSpec((1, nw, HD, LANES), lambda k, i: (k, i, 0, 0))),
        compiler_params=_cparams(("arbitrary", "arbitrary")),
        name="nsa_kv_prep",
    )(proj, proj, proj, proj)


NSA_TQ = 256
NSA_TK = 512
NSA_WK = WINDOW + NSA_TQ
NSA_CH = 32
NSA_PW = 256


def _nsa_kernel(q_ref, sb_ref, oc_ref, gt_ref, ks_ref, vst_ref, kw_ref, vwt_ref, y_ref, *scratch):
    k = pl.program_id(0)
    i = pl.program_id(1)
    tq = NSA_TQ
    cols = NSA_GROUP * tq
    q0 = i * tq
    nt = (((1,), (1,)), ((), ()))
    npart = cols // NSA_PW
    s_refs, p_refs, acc_refs = scratch[0:npart], scratch[npart:2 * npart], scratch[2 * npart:3 * npart]
    gpp = NSA_PW // LANES
    nsubq = tq // LANES
    qs = jnp.concatenate([q_ref[sub * LANES:(sub + 1) * LANES, g * HD:(g + 1) * HD]
                          for sub in range(nsubq) for g in range(NSA_GROUP)], axis=0)
    qs = (qs * (HD ** -0.5 * LOG2E)).astype(BF16)
    sbs = jnp.concatenate([sb_ref[0, sub * LANES:(sub + 1) * LANES, :]
                           for sub in range(nsubq) for _ in range(NSA_GROUP)], axis=0)
    qaug = jnp.concatenate([qs, sbs], axis=1)
    lane_q = lax.broadcasted_iota(jnp.int32, (1, LANES), 1)

    def qpos(c):
        return q0 + (c // NSA_GROUP) * LANES + lane_q

    def softmax_tile(n_keys, k0, m_old, mask_fn, c_list):
        def scores(r0, c):
            s = s_refs[c // gpp][r0:r0 + NSA_CH, (c % gpp) * LANES:(c % gpp + 1) * LANES]
            if mask_fn is None:
                return s
            kpos = k0 + r0 + lax.broadcasted_iota(jnp.int32, (NSA_CH, 1), 0)
            return jnp.where(mask_fn(kpos, qpos(c)), s, NEG)

        def fold(x):
            return x.reshape(NSA_CH // 8, 8, LANES)

        m_new, sums = [], []
        for c in c_list:
            mx8 = jnp.max(fold(scores(0, c)), axis=0)
            for r0 in range(NSA_CH, n_keys, NSA_CH):
                mx8 = jnp.maximum(mx8, jnp.max(fold(scores(r0, c)), axis=0))
            mx = jnp.max(mx8, axis=0, keepdims=True)
            mc = mx if m_old is None else jnp.maximum(m_old[:, c * LANES:(c + 1) * LANES], mx)
            tot8 = jnp.zeros((8, LANES), F32)
            for r0 in range(0, n_keys, NSA_CH):
                p = jnp.exp2(scores(r0, c) - mc)
                tot8 = tot8 + jnp.sum(fold(p), axis=0)
                p_refs[c // gpp][r0:r0 + NSA_CH, (c % gpp) * LANES:(c % gpp + 1) * LANES] = p.astype(BF16)
            m_new.append(mc)
            sums.append(jnp.sum(tot8, axis=0, keepdims=True))
        return m_new, sums

    for a in acc_refs:
        a[...] = jnp.zeros((HD, NSA_PW), F32)

    def sel_tile(j, carry, causal):
        m_old, l_old = carry
        k0 = pl.multiple_of(j * NSA_TK, NSA_TK)
        kt = ks_ref[0, pl.ds(k0, NSA_TK), :]
        vt = vst_ref[0, j]
        mask_fn = (lambda kpos, qp: kpos <= qp) if causal else None
        halves = [slice(h * NSA_PW, (h + 1) * NSA_PW) for h in range(npart)]
        for h, hs in enumerate(halves):
            s_refs[h][0:NSA_TK, :] = lax.dot_general(kt, qaug[hs], nt, preferred_element_type=F32)
        m_out, l_out = [], []
        for h, hs in enumerate(halves):
            m_new, sums = softmax_tile(NSA_TK, k0, m_old, mask_fn, list(range(h * gpp, (h + 1) * gpp)))
            m_new, sums = jnp.concatenate(m_new, axis=1), jnp.concatenate(sums, axis=1)
            alpha = jnp.exp2(m_old[:, hs] - m_new)
            acc_refs[h][...] = acc_refs[h][...] * alpha + jnp.dot(vt, p_refs[h][0:NSA_TK, :],
                                                                  preferred_element_type=F32)
            m_out.append(m_new)
            l_out.append(alpha * l_old[:, hs] + sums)
        return jnp.concatenate(m_out, axis=1), jnp.concatenate(l_out, axis=1)

    n_full = (q0 + tq - 1) // NSA_TK
    init = (jnp.full((1, cols), NEG, F32), jnp.zeros((1, cols), F32))
    carry = lax.fori_loop(0, n_full, lambda j, c: sel_tile(j, c, False), init)
    _, l_s = sel_tile(n_full, carry, True)
    ot_s = jnp.concatenate([a[...] for a in acc_refs], axis=1) / l_s

    w0 = pl.multiple_of(jnp.maximum(q0 - WINDOW, 0), tq)
    kwt = kw_ref[0, pl.ds(w0, NSA_WK), :]
    for h in range(npart):
        s_refs[h][...] = lax.dot_general(kwt, qs[h * NSA_PW:(h + 1) * NSA_PW], nt, preferred_element_type=F32)
    _, l_w = softmax_tile(NSA_WK, w0, None, lambda kpos, qp: (kpos <= qp) & (kpos > qp - WINDOW),
                          list(range(cols // LANES)))
    l_w = jnp.concatenate(l_w, axis=1)
    ot_w = []
    for h in range(npart):
        o = jnp.zeros((HD, NSA_PW), F32)
        for t in range(NSA_WK // LANES):
            o = o + jnp.dot(vwt_ref[0, w0 // LANES + t], p_refs[h][t * LANES:(t + 1) * LANES, :],
                            preferred_element_type=F32)
        ot_w.append(o)
    ot_w = jnp.concatenate(ot_w, axis=1) / l_w

    lane = lax.broadcasted_iota(jnp.int32, (LANES, LANES), 1)
    for sub in range(nsubq):
        rs = slice(sub * LANES, (sub + 1) * LANES)
        sig = _sigmoid(gt_ref[rs, :])
        for g in range(NSA_GROUP):
            head = k * NSA_GROUP + g
            cs = slice((sub * NSA_GROUP + g) * LANES, (sub * NSA_GROUP + g + 1) * LANES)
            g_s = _lane_pick(sig, lane, NSA_HEADS + head)
            g_w = _lane_pick(sig, lane, 2 * NSA_HEADS + head)
            y = oc_ref[rs, g * HD:(g + 1) * HD] + g_s * ot_s[:, cs].T + g_w * ot_w[:, cs].T
            y_ref[rs, g * HD:(g + 1) * HD] = y.astype(y_ref.dtype)


def _nsa_attention(proj, selbias, oc, ksa, vst, kwb, vwt):
    S = proj.shape[0]
    tq = NSA_TQ
    cols = NSA_GROUP * tq
    npart = cols // NSA_PW
    assert S >= NSA_WK and tq % LANES == 0 and NSA_TK % tq == 0

    def full(a):
        return pl.BlockSpec((1,) + a.shape[1:], lambda k, i: (k,) + (0,) * (a.ndim - 1))

    return pl.pallas_call(
        _nsa_kernel,
        out_shape=jax.ShapeDtypeStruct((S, NSA_HEADS * HD), BF16),
        grid=(NSA_KV_HEADS, S // tq),
        in_specs=[pl.BlockSpec((tq, NSA_GROUP * HD), lambda k, i: (i, CB_Q // NSA_GROUP + k)),
                  pl.BlockSpec((1, tq, LANES), lambda k, i: (k, i, 0)),
                  pl.BlockSpec((tq, NSA_GROUP * HD), lambda k, i: (i, k)),
                  pl.BlockSpec((tq, LANES), lambda k, i: (i, CB_GT)),
                  full(ksa), full(vst), full(kwb), full(vwt)],
        out_specs=pl.BlockSpec((tq, NSA_GROUP * HD), lambda k, i: (i, k)),
        scratch_shapes=([pltpu.VMEM((NSA_WK, NSA_PW), F32)] * npart + [pltpu.VMEM((NSA_WK, NSA_PW), BF16)] * npart
                        + [pltpu.VMEM((HD, NSA_PW), F32)] * npart),
        compiler_params=_cparams(("arbitrary", "arbitrary")),
        name="nsa_sel_win",
    )(proj, selbias, oc, proj, ksa, vst, kwb, vwt)


def _ret_kernel(q_ref, k_ref, v_ref, g_ref, dm_ref, ze_ref, xi_ref, dc_ref, gg_ref, gb_ref, o_ref, st_ref):
    n = pl.program_id(1)
    C = RET_CHUNK
    nt = (((1,), (1,)), ((), ()))
    tn = (((0,), (0,)), ((), ()))

    @pl.when(n == 0)
    def _():
        st_ref[...] = jnp.zeros_like(st_ref)

    dmat = dm_ref[0]
    zeta = ze_ref[0]
    xi = xi_ref[0]
    decay = dc_ref[0]
    for c in range(q_ref.shape[0] // C):
        sl = slice(c * C, (c + 1) * C)
        q = q_ref[sl, :]
        kk = k_ref[sl, :] * (HD ** -0.5)
        v = v_ref[sl, :]
        qb, kb, vb = q.astype(BF16), kk.astype(BF16), v.astype(BF16)
        inner = lax.dot_general(qb, kb, nt, preferred_element_type=F32) * dmat
        o = jnp.dot(inner.astype(BF16), vb, preferred_element_type=F32)
        state = st_ref[...]
        o = o + jnp.dot(qb, state.astype(BF16), preferred_element_type=F32) * xi
        kv = lax.dot_general((kk * zeta).astype(BF16), vb, tn, preferred_element_type=F32)
        st_ref[...] = decay * state + kv
        mu = jnp.mean(o, axis=-1, keepdims=True)
        var = jnp.mean(jnp.square(o - mu), axis=-1, keepdims=True)
        y = (o - mu) * lax.rsqrt(var + NORM_EPS) * gg_ref[...] + gb_ref[...]
        gt = g_ref[sl, :]
        o_ref[sl, :] = (y * (gt * _sigmoid(gt))).astype(o_ref.dtype)


def _retention(proj, gn_g, gn_b):
    S = proj.shape[0]
    C = RET_CHUNK
    ts = 1024 if S % 1024 == 0 else C
    H = RET_HEADS
    log_g = jnp.log(1.0 - 2.0 ** (-5.0 - jnp.arange(H, dtype=F32)))
    i = jnp.arange(C, dtype=F32)
    diff = i[:, None] - i[None, :]
    dmat = jnp.where(diff >= 0, jnp.exp(log_g[:, None, None] * jnp.maximum(diff, 0.0)), 0.0)
    zeta = jnp.exp(log_g[:, None] * (C - 1.0 - i))[:, :, None]
    xi = jnp.exp(log_g[:, None] * (i + 1.0))[:, :, None]
    decay = jnp.broadcast_to(jnp.exp(log_g * C)[:, None, None], (H, 1, HD))

    def col(cb):
        return pl.BlockSpec((ts, HD), lambda h, n: (n, cb + h))

    return pl.pallas_call(
        _ret_kernel,
        out_shape=jax.ShapeDtypeStruct((S, H * HD), BF16),
        grid=(H, S // ts),
        in_specs=[col(CB_RQ), col(CB_RK), col(CB_RV), col(CB_RG),
                  pl.BlockSpec((1, C, C), lambda h, n: (h, 0, 0)),
                  pl.BlockSpec((1, C, 1), lambda h, n: (h, 0, 0)),
                  pl.BlockSpec((1, C, 1), lambda h, n: (h, 0, 0)),
                  pl.BlockSpec((1, 1, HD), lambda h, n: (h, 0, 0)),
                  pl.BlockSpec((1, HD), lambda h, n: (0, h)),
                  pl.BlockSpec((1, HD), lambda h, n: (0, h))],
        out_specs=pl.BlockSpec((ts, HD), lambda h, n: (n, h)),
        scratch_shapes=[pltpu.VMEM((HD, HD), F32)],
        compiler_params=_cparams(("arbitrary", "arbitrary")),
        name="retention",
    )(proj, proj, proj, proj, dmat, zeta, xi, decay, gn_g.reshape(1, -1), gn_b.reshape(1, -1))


def _outproj_kernel(x_ref, yc_ref, yn_ref, yr_ref, wc_ref, wn_ref, wr_ref, ga_ref, o_ref):
    y = jnp.dot(yc_ref[...], wc_ref[...], preferred_element_type=F32)
    y = y + jnp.dot(yn_ref[...], wn_ref[...], preferred_element_type=F32)
    y = y + jnp.dot(yr_ref[...], wr_ref[...], preferred_element_type=F32)
    o_ref[...] = x_ref[...] + ga_ref[...] * y


def _out_proj(x2, y_conv, y_nsa, y_ret, w_out_bf, gate_a):
    S = x2.shape[0]
    tm = 512
    wc, wn, wr = w_out_bf[:CONV_CH], w_out_bf[CONV_CH:CONV_CH + NSA_HEADS * HD], w_out_bf[CONV_CH + NSA_HEADS * HD:]

    def rows(w):
        return pl.BlockSpec((tm, w), lambda i: (i, 0))

    def whole(a):
        return pl.BlockSpec(a.shape, lambda i: (0, 0))

    return pl.pallas_call(
        _outproj_kernel,
        out_shape=jax.ShapeDtypeStruct((S, D_MODEL), F32),
        grid=(S // tm,),
        in_specs=[rows(D_MODEL), rows(y_conv.shape[1]), rows(y_nsa.shape[1]), rows(y_ret.shape[1]),
                  whole(wc), whole(wn), whole(wr), pl.BlockSpec((1, D_MODEL), lambda i: (0, 0))],
        out_specs=rows(D_MODEL),
        compiler_params=_cparams(("arbitrary",)),
        name="out_proj",
    )(x2, y_conv, y_nsa, y_ret, wc, wn, wr, gate_a)


def _router_kernel(x_ref, g_ref, sc_ref, sh_ref, rw_ref, rb_ref, h_ref, rt_ref, cnt_ref, carry_ref):
    i = pl.program_id(0)
    tm = x_ref.shape[0]

    @pl.when(i == 0)
    def _():
        carry_ref[...] = jnp.zeros_like(carry_ref)

    h = _modulated_norm(x_ref[...], g_ref[...], sc_ref[...], sh_ref[...])
    half = D_MODEL // 2
    hi = lax.bitcast_convert_type(h[:, :half].astype(BF16).astype(F32), jnp.uint32)
    lo = lax.bitcast_convert_type(h[:, half:].astype(BF16).astype(F32), jnp.uint32)
    h_ref[...] = hi | (lo >> 16)
    rw = rw_ref[...]
    rw_hi = rw.astype(BF16)
    rw_lo = (rw - rw_hi.astype(F32)).astype(BF16)
    h_hi = h.astype(BF16)
    h_lo = (h - h_hi.astype(F32)).astype(BF16)
    logits = (jnp.dot(h_hi, rw_hi, preferred_element_type=F32) + jnp.dot(h_hi, rw_lo, preferred_element_type=F32)
              + jnp.dot(h_lo, rw_hi, preferred_element_type=F32) + rb_ref[...])
    lane = lax.broadcasted_iota(jnp.int32, (tm, LANES), 1)
    lane_f = lane.astype(F32)
    work = jnp.where(lane < N_EXPERTS, logits, -jnp.inf)
    onehot = jnp.zeros((tm, LANES), F32)
    vals, idxs = [], []
    for _ in range(TOP_K):
        m = jnp.max(work, axis=-1, keepdims=True)
        first = jnp.min(jnp.where(work == m, lane_f, float(LANES)), axis=-1, keepdims=True)
        hit = lane_f == first
        onehot = jnp.where(hit, 1.0, onehot)
        work = jnp.where(hit, -jnp.inf, work)
        vals.append(m)
        idxs.append(first)
    ex = [jnp.exp(v - vals[0]) for v in vals]
    den = ex[0] + ex[1] + ex[2] + ex[3]
    r = lax.broadcasted_iota(jnp.int32, (tm, tm), 0)
    c = lax.broadcasted_iota(jnp.int32, (tm, tm), 1)
    tri = jnp.where(c < r, 1.0, 0.0).astype(BF16)
    cum = jnp.dot(tri, onehot.astype(BF16), preferred_element_type=F32) + carry_ref[...]
    out = jnp.zeros((tm, LANES), F32)
    for kk in range(TOP_K):
        rank = jnp.sum(jnp.where(lane_f == idxs[kk], cum, 0.0), axis=-1, keepdims=True)
        out = jnp.where(lane == kk, idxs[kk], out)
        out = jnp.where(lane == TOP_K + kk, ex[kk] / den, out)
        out = jnp.where(lane == 2 * TOP_K + kk, rank, out)
    rt_ref[...] = out
    carry_ref[...] = carry_ref[...] + jnp.sum(onehot, axis=0, keepdims=True)
    cnt_ref[...] = carry_ref[...]


def _router(x2, g, scale, shift, router_w, router_b):
    T = x2.shape[0]
    tm = 256
    rw = jnp.pad(router_w, ((0, 0), (0, LANES - N_EXPERTS)))
    rb = jnp.pad(router_b, (0, LANES - N_EXPERTS)).reshape(1, LANES)
    row = pl.BlockSpec((1, D_MODEL), lambda i: (0, 0))
    return pl.pallas_call(
        _router_kernel,
        out_shape=(jax.ShapeDtypeStruct((T, D_MODEL // 2), jnp.uint32),
                   jax.ShapeDtypeStruct((T, LANES), F32),
                   jax.ShapeDtypeStruct((1, LANES), F32)),
        grid=(T // tm,),
        in_specs=[pl.BlockSpec((tm, D_MODEL), lambda i: (i, 0)), row, row, row,
                  pl.BlockSpec((D_MODEL, LANES), lambda i: (0, 0)),
                  pl.BlockSpec((1, LANES), lambda i: (0, 0))],
        out_specs=(pl.BlockSpec((tm, D_MODEL // 2), lambda i: (i, 0)),
                   pl.BlockSpec((tm, LANES), lambda i: (i, 0)),
                   pl.BlockSpec((1, LANES), lambda i: (0, 0))),
        scratch_shapes=[pltpu.VMEM((1, LANES), F32)],
        compiler_params=_cparams(("arbitrary",)),
        name="moe_router",
    )(x2, g, scale, shift, rw, rb)


GATHER_UNROLL = 8


def _expert_kernel(ie_ref, ir_ref, ins_ref, tok_ref, hp_hbm, wgl_ref, wli_ref, bgl_ref, bli_ref, wdn_ref, bdn_ref,
                   ys_hbm, stage, xbf, acc, sem_g, sem_out):
    w = pl.program_id(0)
    ct = pl.program_id(1)
    n_w = pl.num_programs(0)
    n_ct = pl.num_programs(1)
    nsub = ins_ref[w]
    rows = stage.shape[0]
    per_step = rows // MOE_NCT
    half = D_MODEL // 2

    def gather_row(item, r):
        t = tok_ref[ir_ref[item] + r]
        pltpu.make_async_copy(hp_hbm.at[pl.ds(t, 1)], stage.at[pl.ds(r, 1)], sem_g).start()

    def y_copy(item, sb):
        row = pl.multiple_of(ir_ref[item] + sb * MOE_SUB, MOE_SUB)
        return pltpu.make_async_copy(acc.at[pl.ds(sb * MOE_SUB, MOE_SUB)],
                                     ys_hbm.at[pl.ds(row, MOE_SUB)], sem_out.at[sb])

    def for_subs(item, fn):
        n = ins_ref[item]
        for sb in range(MOE_SUBMAX):
            @pl.when(sb < n)
            def _():
                fn(item, sb)

    @pl.when(nsub > 0)
    def _():
        @pl.when((ct == 0) & (w == 0))
        def _():
            def body(c, carry):
                for u in range(GATHER_UNROLL):
                    gather_row(w, c * GATHER_UNROLL + u)
                return carry

            lax.fori_loop(0, rows // GATHER_UNROLL, body, 0)

        @pl.when(ct == 0)
        def _():
            for sb in range(MOE_SUBMAX):
                pltpu.make_async_copy(hp_hbm.at[pl.ds(0, MOE_SUB)], stage.at[pl.ds(sb * MOE_SUB, MOE_SUB)],
                                      sem_g).wait()
            for sb in range(MOE_SUBMAX):
                wv = stage[sb * MOE_SUB:(sb + 1) * MOE_SUB, :]
                xbf[sb * MOE_SUB:(sb + 1) * MOE_SUB, 0:half] = lax.bitcast_convert_type(
                    wv & jnp.uint32(0xFFFF0000), F32).astype(BF16)
                xbf[sb * MOE_SUB:(sb + 1) * MOE_SUB, half:D_MODEL] = lax.bitcast_convert_type(
                    wv << 16, F32).astype(BF16)

        nxt = jnp.minimum(w + 1, n_w - 1)

        for c in range(MOE_NCT):
            @pl.when((w + 1 < n_w) & (ins_ref[nxt] > 0) & (ct == c))
            def _():
                for u in range(per_step):
                    gather_row(nxt, c * per_step + u)

        def ffn_rows(r0, m_rows, first):
            x = xbf[pl.ds(r0, m_rows), :]
            glu = jnp.dot(x, wgl_ref[0, 0].astype(BF16), preferred_element_type=F32) + bgl_ref[0]
            lin = jnp.dot(x, wli_ref[0, 0].astype(BF16), preferred_element_type=F32) + bli_ref[0]
            glu = jnp.minimum(glu, SWIGLU_LIMIT)
            lin = jnp.clip(lin, -SWIGLU_LIMIT, SWIGLU_LIMIT)
            act = glu * _sigmoid(SWIGLU_ALPHA * glu) * (lin + 1.0)
            y = jnp.dot(act.astype(BF16), wdn_ref[0, 0].astype(BF16), preferred_element_type=F32)

            @pl.when(ct == 0)
            def _():
                @pl.when(first & (w > 0))
                def _():
                    for_subs(jnp.maximum(w - 1, 0), lambda it, sb: y_copy(it, sb).wait())

                acc[pl.ds(r0, m_rows), :] = y + bdn_ref[0]

            @pl.when(ct > 0)
            def _():
                acc[pl.ds(r0, m_rows), :] += y

        def quad_body(qd, carry):
            ffn_rows(pl.multiple_of(qd * (4 * MOE_SUB), 4 * MOE_SUB), 4 * MOE_SUB, qd == 0)
            return carry

        lax.fori_loop(0, nsub // 4, quad_body, 0)

        @pl.when(nsub % 4 >= 2)
        def _():
            ffn_rows(pl.multiple_of((nsub // 4) * (4 * MOE_SUB), 2 * MOE_SUB), 2 * MOE_SUB, nsub < 4)

        @pl.when(nsub % 2 == 1)
        def _():
            ffn_rows(pl.multiple_of((nsub - 1) * MOE_SUB, MOE_SUB), MOE_SUB, nsub == 1)

        @pl.when(ct == n_ct - 1)
        def _():
            for_subs(w, lambda it, sb: y_copy(it, sb).start())
            last = (w == n_w - 1) | (ins_ref[nxt] == 0)

            @pl.when(last)
            def _():
                for_subs(w, lambda it, sb: y_copy(it, sb).wait())


def _experts(hp, slot_tok, item_e, item_row0, item_nsub, w_gu, b_gu, w_dn, b_dn, layer):
    W = item_e.shape[0]
    te = MOE_TE
    n_ct = MOE_NCT
    rows = MOE_SUBMAX * MOE_SUB
    n_slots = slot_tok.shape[0] - rows
    assert rows % n_ct == 0

    def ct_eff(w, ct, ins):
        return jnp.where(ins[w] > 0, ct, n_ct - 1)

    return pl.pallas_call(
        _expert_kernel,
        out_shape=jax.ShapeDtypeStruct((n_slots, D_MODEL), F32),
        grid_spec=pltpu.PrefetchScalarGridSpec(
            num_scalar_prefetch=4,
            grid=(W, n_ct),
            in_specs=[pl.BlockSpec(memory_space=pl.ANY),
                      pl.BlockSpec((1, 1, D_MODEL, te),
                                   lambda w, ct, ie, ir, ins, tok: (layer, ie[w], 0, ct_eff(w, ct, ins))),
                      pl.BlockSpec((1, 1, D_MODEL, te),
                                   lambda w, ct, ie, ir, ins, tok: (layer, ie[w], 0, n_ct + ct_eff(w, ct, ins))),
                      pl.BlockSpec((1, 1, te), lambda w, ct, ie, ir, ins, tok: (ie[w], 0, ct_eff(w, ct, ins))),
                      pl.BlockSpec((1, 1, te), lambda w, ct, ie, ir, ins, tok: (ie[w], 0, n_ct + ct_eff(w, ct, ins))),
                      pl.BlockSpec((1, 1, te, D_MODEL),
                                   lambda w, ct, ie, ir, ins, tok: (layer, ie[w], ct_eff(w, ct, ins), 0)),
                      pl.BlockSpec((1, 1, D_MODEL), lambda w, ct, ie, ir, ins, tok: (ie[w], 0, 0))],
            out_specs=pl.BlockSpec(memory_space=pl.ANY),
            scratch_shapes=[pltpu.VMEM((rows, D_MODEL // 2), jnp.uint32), pltpu.VMEM((rows, D_MODEL), BF16),
                            pltpu.VMEM((rows, D_MODEL), F32),
                            pltpu.SemaphoreType.DMA(()), pltpu.SemaphoreType.DMA((MOE_SUBMAX,))]),
        compiler_params=_cparams(("arbitrary", "arbitrary"), vmem=60 * 1024 * 1024),
        name="moe_experts",
    )(item_e, item_row0, item_nsub, slot_tok, hp, w_gu, w_gu, b_gu.reshape(N_EXPERTS, 1, -1),
      b_gu.reshape(N_EXPERTS, 1, -1), w_dn, b_dn.reshape(N_EXPERTS, 1, -1))


def _combine_kernel(dest_ref, x_ref, rt_ref, gf_ref, fg_ref, ys_hbm, o_ref, buf, sem, *, final_norm):
    i = pl.program_id(0)
    n = pl.num_programs(0)
    tm = x_ref.shape[0]

    def issue(tile):
        slot = tile % 2

        base = tile * (tm * TOP_K)
        for r in range(tm):
            for kk in range(TOP_K):
                d = dest_ref[base + r * TOP_K + kk]
                pltpu.make_async_copy(ys_hbm.at[pl.ds(d, 1)], buf.at[slot, kk, pl.ds(r, 1)],
                                      sem.at[slot]).start()

    @pl.when(i == 0)
    def _():
        issue(i)

    @pl.when(i + 1 < n)
    def _():
        issue(i + 1)

    slot = i % 2
    for kk in range(TOP_K):
        pltpu.make_async_copy(ys_hbm.at[pl.ds(0, tm)], buf.at[slot, kk], sem.at[slot]).wait()
    rt = rt_ref[...]
    lane = lax.broadcasted_iota(jnp.int32, rt.shape, 1)
    moe = jnp.zeros((tm, D_MODEL), F32)
    for kk in range(TOP_K):
        moe = moe + buf[slot, kk] * _lane_pick(rt, lane, TOP_K + kk)
    out = x_ref[...] + gf_ref[...] * moe
    if final_norm:
        ms = jnp.mean(out * out, axis=-1, keepdims=True)
        out = out * lax.rsqrt(ms + NORM_EPS) * fg_ref[...]
    o_ref[...] = out


def _combine(x2, route, gate_f, final_g, ys, dest_flat, final_norm):
    T = x2.shape[0]
    tm = 128
    row = pl.BlockSpec((1, D_MODEL), lambda i, d: (0, 0))
    return pl.pallas_call(
        functools.partial(_combine_kernel, final_norm=final_norm),
        out_shape=jax.ShapeDtypeStruct((T, D_MODEL), F32),
        grid_spec=pltpu.PrefetchScalarGridSpec(
            num_scalar_prefetch=1,
            grid=(T // tm,),
            in_specs=[pl.BlockSpec((tm, D_MODEL), lambda i, d: (i, 0)),
                      pl.BlockSpec((tm, LANES), lambda i, d: (i, 0)), row, row,
                      pl.BlockSpec(memory_space=pl.ANY)],
            out_specs=pl.BlockSpec((tm, D_MODEL), lambda i, d: (i, 0)),
            scratch_shapes=[pltpu.VMEM((2, TOP_K, tm, D_MODEL), F32), pltpu.SemaphoreType.DMA((2,))]),
        compiler_params=_cparams(("arbitrary",)),
        name="moe_combine",
    )(dest_flat, x2, route, gate_f, final_g, ys)


def _moe_plan(route, counts_f):
    T = route.shape[0]
    e = route[:, 0:TOP_K].astype(jnp.int32)
    rank = route[:, 2 * TOP_K:3 * TOP_K].astype(jnp.int32)
    counts = counts_f[0, :N_EXPERTS].astype(jnp.int32)
    nsub = (counts + MOE_SUB - 1) // MOE_SUB
    padded = nsub * MOE_SUB
    pend = jnp.cumsum(padded)
    pstart = pend - padded
    dest = pstart[e] + rank
    n_slots = (T * TOP_K + MOE_SUB - 1) // MOE_SUB * MOE_SUB + N_EXPERTS * MOE_SUB
    tok = jnp.repeat(jnp.arange(T, dtype=jnp.int32), TOP_K)
    slot_tok = jnp.zeros((n_slots + MOE_SUBMAX * MOE_SUB,), jnp.int32).at[dest.reshape(-1)].set(tok)
    n_used = (pend[-1] // MOE_SUB).astype(jnp.int32).reshape(1)
    n_items_max = N_EXPERTS + (n_slots // MOE_SUB) // MOE_SUBMAX
    per_e = (nsub + MOE_SUBMAX - 1) // MOE_SUBMAX
    iend = jnp.cumsum(per_e)
    w = jnp.arange(n_items_max, dtype=jnp.int32)
    ew = jnp.minimum(jnp.searchsorted(iend, w, side='right'), N_EXPERTS - 1).astype(jnp.int32)
    local = w - (iend[ew] - per_e[ew])
    live = w < iend[-1]
    item_nsub = jnp.where(live, jnp.clip(nsub[ew] - local * MOE_SUBMAX, 0, MOE_SUBMAX), 0).astype(jnp.int32)
    item_row0 = jnp.where(live, pstart[ew] + local * (MOE_SUBMAX * MOE_SUB), 0).astype(jnp.int32)
    last_e = ew[jnp.maximum(iend[-1] - 1, 0)]
    item_e = jnp.where(live, ew, last_e).astype(jnp.int32)
    return dest.reshape(-1).astype(jnp.int32), slot_tok, n_used, item_e, item_row0, item_nsub


def _repack_w_in(w_in_l):
    parts, width = [], 0
    for name in _ORDER:
        off, size = _SRC[name]
        parts.append(w_in_l[:, off:off + size])
        width += size
        if size % LANES:
            parts.append(jnp.zeros((D_MODEL, LANES - size % LANES), w_in_l.dtype))
            width += LANES - size % LANES
    parts.append(jnp.zeros((D_MODEL, PROJ_W - width), w_in_l.dtype))
    return jnp.concatenate(parts, axis=1).astype(BF16)


def _cover_matrix(S):
    n_strip = S // CMP_STRIDE
    n_cmp = (S - CMP_BLOCK) // CMP_STRIDE + 1
    n = np.arange(n_strip)[:, None]
    j = np.arange(LANES)[None, :]
    start, end = n * CMP_STRIDE, n * CMP_STRIDE + CMP_BLOCK - 1
    cov = (start <= j * SEL_BLOCK + SEL_BLOCK - 1) & (end >= j * SEL_BLOCK) & (n < n_cmp) & (j < S // SEL_BLOCK)
    return jnp.asarray(cov.astype(np.float32).T)


def _strips(proj, cb):
    S = proj.shape[0]
    t = proj[:, cb * LANES:(cb + NSA_KV_HEADS) * LANES].reshape(S, NSA_KV_HEADS, HD)
    return t.transpose(1, 0, 2).reshape(NSA_KV_HEADS, S // CMP_STRIDE, CMP_STRIDE * HD)


def kernel(x, c, ada_w, ada_b, norm_mix_g, w_in, conv_dw_w, conv_dw_b, conv_ln_g, conv_ln_b, nsa_pe_k, nsa_pe_v, nsa_cmp_k_w1, nsa_cmp_k_w2, nsa_cmp_v_w1, nsa_cmp_v_w2, ret_gn_g, ret_gn_b, w_out, norm_ffn_g, router_w, router_b, moe_w_gate_up, moe_b_gate_up, moe_w_down, moe_b_down, final_norm_g):
    B, S, _ = x.shape
    assert B == 1 and c.shape[0] == 1
    x2 = x.reshape(S, D_MODEL)
    mod = _ada_mod(c, ada_w, ada_b).reshape(DEPTH, 6, 1, D_MODEL)
    tabs = _rotary_tables(S)
    cover = _cover_matrix(S)
    final_g = final_norm_g.reshape(1, D_MODEL)
    for l in range(DEPTH):
        shift_a, scale_a, gate_a, shift_f, scale_f, gate_f = [mod[l, i] for i in range(6)]
        proj = _in_proj(x2, norm_mix_g[l].reshape(1, -1), scale_a, shift_a, _repack_w_in(w_in[l]), tabs)
        y_conv = _conv_group(proj, conv_dw_w[l], conv_dw_b[l], conv_ln_g[l], conv_ln_b[l])
        kc = _compress(_strips(proj, CB_KC), nsa_pe_k[l], nsa_cmp_k_w1[l], nsa_cmp_k_w2[l])
        vc = _compress(_strips(proj, CB_VC), nsa_pe_v[l], nsa_cmp_v_w1[l], nsa_cmp_v_w2[l])
        oc, selbias = _cmp_attention(proj, kc, vc, cover)
        ksa, vsb, kwb, vwb = _kv_prep(proj)
        y_nsa = _nsa_attention(proj, selbias, oc, ksa, vsb, kwb, vwb)
        y_ret = _retention(proj, ret_gn_g[l], ret_gn_b[l])
        x2 = _out_proj(x2, y_conv, y_nsa, y_ret, w_out[l].astype(BF16), gate_a)
        h2, route, counts = _router(x2, norm_ffn_g[l].reshape(1, -1), scale_f, shift_f, router_w[l], router_b[l])
        dest, slot_tok, n_used, item_e, item_row0, item_nsub = _moe_plan(route, counts)
        ys = _experts(h2, slot_tok, item_e, item_row0, item_nsub, moe_w_gate_up, moe_b_gate_up[l], moe_w_down,
                      moe_b_down[l], l)
        x2 = _combine(x2, route, gate_f, final_g, ys, dest, final_norm=(l == DEPTH - 1))
    return x2.reshape(B, S, D_MODEL)
```

```python
import functools
import math

import numpy as np
import jax
import jax.numpy as jnp
from jax import lax
from jax.experimental import pallas as pl
from jax.experimental.pallas import tpu as pltpu

F32 = jnp.float32
BF16 = jnp.bfloat16
HI = lax.Precision.HIGHEST

D_MODEL = 2048
DEPTH = 2
CONV_CH = 512
CONV_WIDTH = 31
HD = 128
NSA_HEADS = 8
NSA_KV_HEADS = 2
NSA_GROUP = 4
CMP_BLOCK = 32
CMP_STRIDE = 16
SEL_BLOCK = 64
SEL_TOPN = 16
WINDOW = 512
ROPE_THETA = 500000.0
ROPE_DIM = 32
RET_HEADS = 4
RET_CHUNK = 128
RET_THETA = 10000.0
N_EXPERTS = 32
TOP_K = 4
D_EXPERT = 2048
SWIGLU_LIMIT = 7.0
SWIGLU_ALPHA = 1.702
NORM_EPS = 1e-6
NEG = -1e30
BIG = 1e30
LOG2E = math.log2(math.e)

LANES = 128
VMEM_LIMIT = 56 * 1024 * 1024

CB_Q, CB_KC, CB_KS, CB_KW = 0, 8, 10, 12
CB_RQ, CB_RK = 14, 18
CB_VC, CB_CV, CB_CG, CB_VS, CB_VW, CB_GT, CB_RV, CB_RG = 22, 24, 28, 32, 34, 36, 37, 41
PROJ_BLOCKS = 46
PROJ_W = PROJ_BLOCKS * LANES
PROJ_TN = 256
NSA_ROT_TILES = CB_RQ * LANES // PROJ_TN
RET_ROT_TILES = CB_VC * LANES // PROJ_TN
_SRC = {'cv': (0, 512), 'cg': (512, 512), 'q': (1024, 1024), 'kc': (2048, 256), 'vc': (2304, 256),
        'ks': (2560, 256), 'vs': (2816, 256), 'kw': (3072, 256), 'vw': (3328, 256), 'gt': (3584, 24),
        'rq': (3608, 512), 'rk': (4120, 512), 'rv': (4632, 512), 'rg': (5144, 512)}
_ORDER = ['q', 'kc', 'ks', 'kw', 'rq', 'rk', 'vc', 'cv', 'cg', 'vs', 'vw', 'gt', 'rv', 'rg']

MOE_SUB = 256
MOE_SUBMAX = 5
MOE_TE = 512
MOE_NCT = D_EXPERT // MOE_TE


def _sigmoid(x):
    return 1.0 / (1.0 + jnp.exp(-x))


def _cparams(sem, vmem=VMEM_LIMIT):
    return pltpu.CompilerParams(dimension_semantics=sem, vmem_limit_bytes=vmem)


def _ada_kernel(c_ref, w_ref, b_ref, o_ref, sc_ref):
    @pl.when(pl.program_id(1) == 0)
    def _():
        cv = c_ref[...]
        sc_ref[...] = cv * _sigmoid(cv)

    tn = o_ref.shape[-1]

    def body(i, acc):
        k0 = pl.multiple_of(i * 64, 64)
        p = w_ref[0, pl.ds(k0, 64), :] * sc_ref[pl.ds(k0, 64), :]
        return acc + p.reshape(8, 8, tn).sum(axis=0)

    acc = lax.fori_loop(0, D_MODEL // 64, body, jnp.zeros((8, tn), F32))
    o_ref[0] = jnp.sum(acc, axis=0, keepdims=True) + b_ref[0]


def _ada_mod(c, ada_w, ada_b):
    tn = 1024
    n = 6 * D_MODEL
    return pl.pallas_call(
        _ada_kernel,
        out_shape=jax.ShapeDtypeStruct((DEPTH, 1, n), F32),
        grid=(DEPTH, n // tn),
        in_specs=[pl.BlockSpec((D_MODEL, 1), lambda l, j: (0, 0)),
                  pl.BlockSpec((1, D_MODEL, tn), lambda l, j: (l, 0, j)),
                  pl.BlockSpec((1, 1, tn), lambda l, j: (l, 0, j))],
        out_specs=pl.BlockSpec((1, 1, tn), lambda l, j: (l, 0, j)),
        scratch_shapes=[pltpu.VMEM((D_MODEL, 1), F32)],
        compiler_params=_cparams(("arbitrary", "arbitrary")),
        name="ada_mod",
    )(c.reshape(D_MODEL, 1), ada_w, ada_b.reshape(DEPTH, 1, n))


def _modulated_norm(x, g, scale, shift):
    ms = jnp.mean(x * x, axis=-1, keepdims=True)
    return x * lax.rsqrt(ms + NORM_EPS) * g * (1.0 + scale) + shift


def _inproj_kernel(x_ref, g_ref, sc_ref, sh_ref, w_ref, cn_ref, s1_ref, s2_ref, cr_ref, sr_ref,
                   o_ref, h_ref):
    j = pl.program_id(1)
    nsub = o_ref.shape[-1] // LANES

    @pl.when(j == 0)
    def _():
        h_ref[...] = _modulated_norm(x_ref[...], g_ref[...], sc_ref[...], sh_ref[...]).astype(BF16)

    def matmul():
        return jnp.dot(h_ref[...], w_ref[...], preferred_element_type=F32)

    @pl.when(j < NSA_ROT_TILES)
    def _():
        acc = matmul()
        for c in range(nsub):
            sub = acc[:, c * LANES:(c + 1) * LANES]
            o_ref[:, c * LANES:(c + 1) * LANES] = (
                sub * cn_ref[...] + pltpu.roll(sub, LANES - ROPE_DIM // 2, 1) * s1_ref[...]
                + pltpu.roll(sub, ROPE_DIM // 2, 1) * s2_ref[...])

    @pl.when((j >= NSA_ROT_TILES) & (j < RET_ROT_TILES))
    def _():
        acc = matmul()
        for c in range(nsub):
            sub = acc[:, c * LANES:(c + 1) * LANES]
            o_ref[:, c * LANES:(c + 1) * LANES] = sub * cr_ref[...] + pltpu.roll(sub, HD // 2, 1) * sr_ref[...]

    @pl.when(j >= RET_ROT_TILES)
    def _():
        o_ref[...] = matmul()


def _in_proj(x2, g, scale, shift, w_bf, tabs):
    S = x2.shape[0]
    tm, tn = 1024, PROJ_TN
    row = pl.BlockSpec((1, D_MODEL), lambda i, j: (0, 0))
    tab = pl.BlockSpec((tm, LANES), lambda i, j: (i, 0))
    return pl.pallas_call(
        _inproj_kernel,
        out_shape=jax.ShapeDtypeStruct((S, PROJ_W), F32),
        grid=(S // tm, PROJ_W // tn),
        in_specs=[pl.BlockSpec((tm, D_MODEL), lambda i, j: (i, 0)), row, row, row,
                  pl.BlockSpec((D_MODEL, tn), lambda i, j: (0, j)), tab, tab, tab, tab, tab],
        out_specs=pl.BlockSpec((tm, tn), lambda i, j: (i, j)),
        scratch_shapes=[pltpu.VMEM((tm, D_MODEL), BF16)],
        compiler_params=_cparams(("arbitrary", "arbitrary")),
        name="in_proj",
    )(x2, g, scale, shift, w_bf, *tabs)


def _rotary_tables(S):
    pos = jnp.arange(S, dtype=F32)[:, None]
    half = ROPE_DIM // 2
    inv = ROPE_THETA ** (-jnp.arange(half, dtype=F32) * 2.0 / ROPE_DIM)
    ang = pos * inv[None, :]
    cos, sin = jnp.cos(ang), jnp.sin(ang)
    ones = jnp.ones((S, LANES - ROPE_DIM), F32)
    zeros_r = jnp.zeros((S, LANES - ROPE_DIM), F32)
    zeros_h = jnp.zeros((S, half), F32)
    cn = jnp.concatenate([cos, cos, ones], axis=1)
    s1 = jnp.concatenate([-sin, zeros_h, zeros_r], axis=1)
    s2 = jnp.concatenate([zeros_h, sin, zeros_r], axis=1)
    halfr = HD // 2
    invr = RET_THETA ** (-jnp.arange(halfr, dtype=F32) * 2.0 / HD)
    angr = pos * invr[None, :]
    cosr, sinr = jnp.cos(angr), jnp.sin(angr)
    cr = jnp.concatenate([cosr, cosr], axis=1)
    sr = jnp.concatenate([-sinr, sinr], axis=1)
    return cn, s1, s2, cr, sr


def _conv_kernel(cv_ref, cg_ref, dw_ref, db_ref, lg_ref, lb_ref, o_ref, hb_ref, xs_ref):
    i = pl.program_id(0)
    ts = o_ref.shape[0]
    halo = 32

    @pl.when(i == 0)
    def _():
        hb_ref[0:halo, :] = jnp.zeros((halo, CONV_CH), F32)

    @pl.when(i > 0)
    def _():
        hb_ref[0:halo, :] = hb_ref[ts:ts + halo, :]

    hb_ref[halo:halo + ts, :] = cv_ref[...] * _sigmoid(cg_ref[...])
    acc = jnp.zeros((ts, CONV_CH), F32) + db_ref[...]
    first = halo - (CONV_WIDTH - 1)
    for sh in range(8):
        offs = [first + w for w in range(CONV_WIDTH) if (first + w) % 8 == sh]
        span = max(offs) - sh + ts
        xs_ref[sh, 0:span, :] = hb_ref[sh:sh + span, :]
        for off in offs:
            acc = acc + xs_ref[sh, off - sh:off - sh + ts, :] * dw_ref[off - first:off - first + 1, :]
    mu = jnp.mean(acc, axis=-1, keepdims=True)
    var = jnp.mean(jnp.square(acc - mu), axis=-1, keepdims=True)
    y = (acc - mu) * lax.rsqrt(var + NORM_EPS) * lg_ref[...] + lb_ref[...]
    o_ref[...] = (y * _sigmoid(y)).astype(o_ref.dtype)


def _conv_group(proj, dw_w, dw_b, ln_g, ln_b):
    S = proj.shape[0]
    ts = 256
    vec = pl.BlockSpec((1, CONV_CH), lambda i: (0, 0))
    return pl.pallas_call(
        _conv_kernel,
        out_shape=jax.ShapeDtypeStruct((S, CONV_CH), BF16),
        grid=(S // ts,),
        in_specs=[pl.BlockSpec((ts, CONV_CH), lambda i: (i, CB_CV * LANES // CONV_CH)),
                  pl.BlockSpec((ts, CONV_CH), lambda i: (i, CB_CG * LANES // CONV_CH)),
                  pl.BlockSpec((CONV_WIDTH, CONV_CH), lambda i: (0, 0)), vec, vec, vec],
        out_specs=pl.BlockSpec((ts, CONV_CH), lambda i: (i, 0)),
        scratch_shapes=[pltpu.VMEM((ts + 32, CONV_CH), F32), pltpu.VMEM((8, ts + 32, CONV_CH), F32)],
        compiler_params=_cparams(("arbitrary",)),
        name="conv_group",
    )(proj, proj, dw_w, dw_b.reshape(1, -1), ln_g.reshape(1, -1), ln_b.reshape(1, -1))


def _compress_kernel(x_ref, pe_ref, w1_ref, w2_ref, o_ref):
    X = x_ref[0]
    nrow = X.shape[0]
    half = CMP_STRIDE * HD
    A = jnp.dot(X, w1_ref[0:half, :], precision=HI, preferred_element_type=F32)
    B = jnp.dot(X, w1_ref[half:2 * half, :], precision=HI, preferred_element_type=F32)
    pe8 = jnp.broadcast_to(pe_ref[...], (8, 2 * half))
    cst = jnp.dot(pe8, w1_ref[...], precision=HI, preferred_element_type=F32)[0:1]
    pre = A + pltpu.roll(B, nrow - 1, 0) + cst
    act = pre * _sigmoid(pre)
    out = jnp.dot(act, w2_ref[...], precision=HI, preferred_element_type=F32)
    rows = lax.broadcasted_iota(jnp.int32, out.shape, 0)
    o_ref[0] = jnp.where(rows < nrow - 1, out, 0.0)


def _compress(strips, pe, w1, w2):
    nh, nrow, width = strips.shape
    return pl.pallas_call(
        _compress_kernel,
        out_shape=jax.ShapeDtypeStruct((nh, nrow, HD), F32),
        grid=(nh,),
        in_specs=[pl.BlockSpec((1, nrow, width), lambda h: (h, 0, 0)),
                  pl.BlockSpec((1, width * 2), lambda h: (0, 0)),
                  pl.BlockSpec((width * 2, HD), lambda h: (0, 0)),
                  pl.BlockSpec((HD, HD), lambda h: (0, 0))],
        out_specs=pl.BlockSpec((1, nrow, HD), lambda h: (h, 0, 0)),
        compiler_params=_cparams(("arbitrary",)),
        name="nsa_compress",
    )(strips, pe.reshape(1, -1), w1, w2)


def _lane_pick(vals, lane, idx):
    return jnp.sum(jnp.where(lane == idx, vals, 0.0), axis=-1, keepdims=True)


def _cmp_kernel(q_ref, gt_ref, kc_ref, vc_ref, covt_ref, oc_ref, sb_ref, *, n_sel):
    k = pl.program_id(0)
    i = pl.program_id(1)
    tq = q_ref.shape[0]
    ncp = kc_ref.shape[1]
    scale = HD ** -0.5
    nt = (((1,), (1,)), ((), ()))
    tpos = i * tq + lax.broadcasted_iota(jnp.int32, (1, tq), 1)
    nrow = lax.broadcasted_iota(jnp.int32, (ncp, 1), 0)
    mc = (nrow * CMP_STRIDE + (CMP_BLOCK - 1)) <= tpos
    anyv = tpos >= (CMP_BLOCK - 1)
    kc = kc_ref[0]
    kc_hi = kc.astype(BF16)
    kc_lo = (kc - kc_hi.astype(F32)).astype(BF16)
    kc_cat = jnp.concatenate([kc_hi, kc_lo, kc_hi], axis=1)
    covt = covt_ref[...].astype(BF16)
    vct = vc_ref[0].T.astype(BF16)
    lane = lax.broadcasted_iota(jnp.int32, (tq, LANES), 1)
    sig = _sigmoid(gt_ref[...])
    psum = jnp.zeros((ncp, tq), F32)
    for g in range(NSA_GROUP):
        qg = q_ref[:, g * HD:(g + 1) * HD] * scale
        q_hi = qg.astype(BF16)
        q_lo = (qg - q_hi.astype(F32)).astype(BF16)
        s = lax.dot_general(kc_cat, jnp.concatenate([q_hi, q_hi, q_lo], axis=1), nt,
                            preferred_element_type=F32)
        s = jnp.where(mc, s, NEG)
        m = jnp.max(s, axis=0, keepdims=True)
        e = jnp.where(mc, jnp.exp(s - m), 0.0)
        l = jnp.sum(e, axis=0, keepdims=True)
        p = e / jnp.where(anyv, l, 1.0)
        ot = jnp.dot(vct, p.astype(BF16), preferred_element_type=F32)
        gate = _lane_pick(sig, lane, k * NSA_GROUP + g)
        oc_ref[:, g * HD:(g + 1) * HD] = gate * ot.T
        psum = psum + p
    p_hi = psum.astype(BF16)
    p_lo = (psum - p_hi.astype(F32)).astype(BF16)
    imp = (jnp.dot(covt, p_hi, preferred_element_type=F32)
           + jnp.dot(covt, p_lo, preferred_element_type=F32))
    blk = lax.broadcasted_iota(jnp.int32, (LANES, tq), 0)
    valid = (blk * SEL_BLOCK <= tpos) & (blk < n_sel)
    forced = (blk == 0) | (blk == tpos // SEL_BLOCK)
    work = jnp.where(forced, BIG, jnp.where(valid, imp, NEG))
    blk_f = blk.astype(F32)
    chosen = jnp.zeros((LANES, tq), F32)
    for _ in range(min(SEL_TOPN, n_sel)):
        m = jnp.max(work, axis=0, keepdims=True)
        first = jnp.min(jnp.where(work == m, blk_f, float(LANES)), axis=0, keepdims=True)
        hit = blk_f == first
        chosen = jnp.where(hit, 1.0, chosen)
        work = jnp.where(hit, -jnp.inf, work)
    keep = (chosen > 0.5) & valid
    sb_ref[0] = jnp.where(keep, 0.0, NEG).T.astype(BF16)


def _cmp_attention(proj, kc, vc, cover):
    S = proj.shape[0]
    tq = 256
    ncp = kc.shape[1]
    return pl.pallas_call(
        functools.partial(_cmp_kernel, n_sel=S // SEL_BLOCK),
        out_shape=(jax.ShapeDtypeStruct((S, NSA_HEADS * HD), F32),
                   jax.ShapeDtypeStruct((NSA_KV_HEADS, S, LANES), BF16)),
        grid=(NSA_KV_HEADS, S // tq),
        in_specs=[pl.BlockSpec((tq, NSA_GROUP * HD), lambda k, i: (i, CB_Q // NSA_GROUP + k)),
                  pl.BlockSpec((tq, LANES), lambda k, i: (i, CB_GT)),
                  pl.BlockSpec((1, ncp, HD), lambda k, i: (k, 0, 0)),
                  pl.BlockSpec((1, ncp, HD), lambda k, i: (k, 0, 0)),
                  pl.BlockSpec((LANES, ncp), lambda k, i: (0, 0))],
        out_specs=(pl.BlockSpec((tq, NSA_GROUP * HD), lambda k, i: (i, k)),
                   pl.BlockSpec((1, tq, LANES), lambda k, i: (k, i, 0))),
        compiler_params=_cparams(("arbitrary", "arbitrary")),
        name="nsa_cmp_select",
    )(proj, proj, kc, vc, cover)


def _kvprep_kernel(ks_ref, vs_ref, kw_ref, vw_ref, ksa_ref, vsb_ref, kwb_ref, vwb_ref):
    i = pl.program_id(1)
    ts = ks_ref.shape[0]
    rows = i * ts + lax.broadcasted_iota(jnp.int32, (ts, LANES), 0)
    lane = lax.broadcasted_iota(jnp.int32, (ts, LANES), 1)
    ksa_ref[0, :, 0:HD] = ks_ref[...].astype(BF16)
    ksa_ref[0, :, HD:2 * HD] = jnp.where(lane == rows // SEL_BLOCK, 1.0, 0.0).astype(BF16)
    vsb_ref[0, 0] = vs_ref[...].T.astype(BF16)
    kwb_ref[0] = kw_ref[...].astype(BF16)
    for t in range(ts // LANES):
        vwb_ref[0, t] = vw_ref[t * LANES:(t + 1) * LANES, :].T.astype(BF16)


def _kv_prep(proj):
    S = proj.shape[0]
    ts = NSA_TK
    nw = ts // LANES

    def col(cb):
        return pl.BlockSpec((ts, HD), lambda k, i: (i, cb + k))

    return pl.pallas_call(
        _kvprep_kernel,
        out_shape=(jax.ShapeDtypeStruct((NSA_KV_HEADS, S, 2 * HD), BF16),
                   jax.ShapeDtypeStruct((NSA_KV_HEADS, S // ts, HD, ts), BF16),
                   jax.ShapeDtypeStruct((NSA_KV_HEADS, S, HD), BF16),
                   jax.ShapeDtypeStruct((NSA_KV_HEADS, S // LANES, HD, LANES), BF16)),
        grid=(NSA_KV_HEADS, S // ts),
        in_specs=[col(CB_KS), col(CB_VS), col(CB_KW), col(CB_VW)],
        out_specs=(pl.BlockSpec((1, ts, 2 * HD), lambda k, i: (k, i, 0)),
                   pl.BlockSpec((1, 1, HD, ts), lambda k, i: (k, i, 0, 0)),
                   pl.BlockSpec((1, ts, HD), lambda k, i: (k, i, 0)),
                   pl.BlockSpec((1, nw, HD, LANES), lambda k, i: (k, i, 0, 0))),
        compiler_params=_cparams(("arbitrary", "arbitrary")),
        name="nsa_kv_prep",
    )(proj, proj, proj, proj)


NSA_TQ = 256
NSA_TK = 512
NSA_WK = WINDOW + NSA_TQ
NSA_CH = 32
NSA_PW = 256


def _nsa_kernel(q_ref, sb_ref, oc_ref, gt_ref, ks_ref, vst_ref, kw_ref, vwt_ref, y_ref, *scratch):
    k = pl.program_id(0)
    i = pl.program_id(1)
    tq = NSA_TQ
    cols = NSA_GROUP * tq
    q0 = i * tq
    nt = (((1,), (1,)), ((), ()))
    npart = cols // NSA_PW
    s_refs, p_refs, acc_refs = scratch[0:npart], scratch[npart:2 * npart], scratch[2 * npart:3 * npart]
    gpp = NSA_PW // LANES
    nsubq = tq // LANES
    qs = jnp.concatenate([q_ref[sub * LANES:(sub + 1) * LANES, g * HD:(g + 1) * HD]
                          for sub in range(nsubq) for g in range(NSA_GROUP)], axis=0)
    qs = (qs * (HD ** -0.5 * LOG2E)).astype(BF16)
    sbs = jnp.concatenate([sb_ref[0, sub * LANES:(sub + 1) * LANES, :]
                           for sub in range(nsubq) for _ in range(NSA_GROUP)], axis=0)
    qaug = jnp.concatenate([qs, sbs], axis=1)
    lane_q = lax.broadcasted_iota(jnp.int32, (1, LANES), 1)

    def qpos(c):
        return q0 + (c // NSA_GROUP) * LANES + lane_q

    def softmax_tile(n_keys, k0, m_old, mask_fn, c_list):
        def scores(r0, c):
            s = s_refs[c // gpp][r0:r0 + NSA_CH, (c % gpp) * LANES:(c % gpp + 1) * LANES]
            if mask_fn is None:
                return s
            kpos = k0 + r0 + lax.broadcasted_iota(jnp.int32, (NSA_CH, 1), 0)
            return jnp.where(mask_fn(kpos, qpos(c)), s, NEG)

        def fold(x):
            return x.reshape(NSA_CH // 8, 8, LANES)

        m_new, sums = [], []
        for c in c_list:
            mx8 = jnp.max(fold(scores(0, c)), axis=0)
            for r0 in range(NSA_CH, n_keys, NSA_CH):
                mx8 = jnp.maximum(mx8, jnp.max(fold(scores(r0, c)), axis=0))
            mx = jnp.max(mx8, axis=0, keepdims=True)
            mc = mx if m_old is None else jnp.maximum(m_old[:, c * LANES:(c + 1) * LANES], mx)
            tot8 = jnp.zeros((8, LANES), F32)
            for r0 in range(0, n_keys, NSA_CH):
                p = jnp.exp2(scores(r0, c) - mc)
                tot8 = tot8 + jnp.sum(fold(p), axis=0)
                p_refs[c // gpp][r0:r0 + NSA_CH, (c % gpp) * LANES:(c % gpp + 1) * LANES] = p.astype(BF16)
            m_new.append(mc)
            sums.append(jnp.sum(tot8, axis=0, keepdims=True))
        return m_new, sums

    for a in acc_refs:
        a[...] = jnp.zeros((HD, NSA_PW), F32)

    def sel_tile(j, carry, causal):
        m_old, l_old = carry
        k0 = pl.multiple_of(j * NSA_TK, NSA_TK)
        kt = ks_ref[0, pl.ds(k0, NSA_TK), :]
        vt = vst_ref[0, j]
        mask_fn = (lambda kpos, qp: kpos <= qp) if causal else None
        halves = [slice(h * NSA_PW, (h + 1) * NSA_PW) for h in range(npart)]
        for h, hs in enumerate(halves):
            s_refs[h][0:NSA_TK, :] = lax.dot_general(kt, qaug[hs], nt, preferred_element_type=F32)
        m_out, l_out = [], []
        for h, hs in enumerate(halves):
            m_new, sums = softmax_tile(NSA_TK, k0, m_old, mask_fn, list(range(h * gpp, (h + 1) * gpp)))
            m_new, sums = jnp.concatenate(m_new, axis=1), jnp.concatenate(sums, axis=1)
            alpha = jnp.exp2(m_old[:, hs] - m_new)
            acc_refs[h][...] = acc_refs[h][...] * alpha + jnp.dot(vt, p_refs[h][0:NSA_TK, :],
                                                                  preferred_element_type=F32)
            m_out.append(m_new)
            l_out.append(alpha * l_old[:, hs] + sums)
        return jnp.concatenate(m_out, axis=1), jnp.concatenate(l_out, axis=1)

    n_full = (q0 + tq - 1) // NSA_TK
    init = (jnp.full((1, cols), NEG, F32), jnp.zeros((1, cols), F32))
    carry = lax.fori_loop(0, n_full, lambda j, c: sel_tile(j, c, False), init)
    _, l_s = sel_tile(n_full, carry, True)
    ot_s = jnp.concatenate([a[...] for a in acc_refs], axis=1) / l_s

    w0 = pl.multiple_of(jnp.maximum(q0 - WINDOW, 0), tq)
    kwt = kw_ref[0, pl.ds(w0, NSA_WK), :]
    for h in range(npart):
        s_refs[h][...] = lax.dot_general(kwt, qs[h * NSA_PW:(h + 1) * NSA_PW], nt, preferred_element_type=F32)
    _, l_w = softmax_tile(NSA_WK, w0, None, lambda kpos, qp: (kpos <= qp) & (kpos > qp - WINDOW),
                          list(range(cols // LANES)))
    l_w = jnp.concatenate(l_w, axis=1)
    ot_w = []
    for h in range(npart):
        o = jnp.zeros((HD, NSA_PW), F32)
        for t in range(NSA_WK // LANES):
            o = o + jnp.dot(vwt_ref[0, w0 // LANES + t], p_refs[h][t * LANES:(t + 1) * LANES, :],
                            preferred_element_type=F32)
        ot_w.append(o)
    ot_w = jnp.concatenate(ot_w, axis=1) / l_w

    lane = lax.broadcasted_iota(jnp.int32, (LANES, LANES), 1)
    for sub in range(nsubq):
        rs = slice(sub * LANES, (sub + 1) * LANES)
        sig = _sigmoid(gt_ref[rs, :])
        for g in range(NSA_GROUP):
            head = k * NSA_GROUP + g
            cs = slice((sub * NSA_GROUP + g) * LANES, (sub * NSA_GROUP + g + 1) * LANES)
            g_s = _lane_pick(sig, lane, NSA_HEADS + head)
            g_w = _lane_pick(sig, lane, 2 * NSA_HEADS + head)
            y = oc_ref[rs, g * HD:(g + 1) * HD] + g_s * ot_s[:, cs].T + g_w * ot_w[:, cs].T
            y_ref[rs, g * HD:(g + 1) * HD] = y.astype(y_ref.dtype)


def _nsa_attention(proj, selbias, oc, ksa, vst, kwb, vwt):
    S = proj.shape[0]
    tq = NSA_TQ
    cols = NSA_GROUP * tq
    npart = cols // NSA_PW
    assert S >= NSA_WK and tq % LANES == 0 and NSA_TK % tq == 0

    def full(a):
        return pl.BlockSpec((1,) + a.shape[1:], lambda k, i: (k,) + (0,) * (a.ndim - 1))

    return pl.pallas_call(
        _nsa_kernel,
        out_shape=jax.ShapeDtypeStruct((S, NSA_HEADS * HD), BF16),
        grid=(NSA_KV_HEADS, S // tq),
        in_specs=[pl.BlockSpec((tq, NSA_GROUP * HD), lambda k, i: (i, CB_Q // NSA_GROUP + k)),
                  pl.BlockSpec((1, tq, LANES), lambda k, i: (k, i, 0)),
                  pl.BlockSpec((tq, NSA_GROUP * HD), lambda k, i: (i, k)),
                  pl.BlockSpec((tq, LANES), lambda k, i: (i, CB_GT)),
                  full(ksa), full(vst), full(kwb), full(vwt)],
        out_specs=pl.BlockSpec((tq, NSA_GROUP * HD), lambda k, i: (i, k)),
        scratch_shapes=([pltpu.VMEM((NSA_WK, NSA_PW), F32)] * npart + [pltpu.VMEM((NSA_WK, NSA_PW), BF16)] * npart
                        + [pltpu.VMEM((HD, NSA_PW), F32)] * npart),
        compiler_params=_cparams(("arbitrary", "arbitrary")),
        name="nsa_sel_win",
    )(proj, selbias, oc, proj, ksa, vst, kwb, vwt)


def _ret_kernel(q_ref, k_ref, v_ref, g_ref, dm_ref, ze_ref, xi_ref, dc_ref, gg_ref, gb_ref, o_ref, st_ref):
    n = pl.program_id(1)
    C = RET_CHUNK
    nt = (((1,), (1,)), ((), ()))
    tn = (((0,), (0,)), ((), ()))

    @pl.when(n == 0)
    def _():
        st_ref[...] = jnp.zeros_like(st_ref)

    dmat = dm_ref[0]
    zeta = ze_ref[0]
    xi = xi_ref[0]
    decay = dc_ref[0]
    for c in range(q_ref.shape[0] // C):
        sl = slice(c * C, (c + 1) * C)
        q = q_ref[sl, :]
        kk = k_ref[sl, :] * (HD ** -0.5)
        v = v_ref[sl, :]
        qb, kb, vb = q.astype(BF16), kk.astype(BF16), v.astype(BF16)
        inner = lax.dot_general(qb, kb, nt, preferred_element_type=F32) * dmat
        o = jnp.dot(inner.astype(BF16), vb, preferred_element_type=F32)
        state = st_ref[...]
        o = o + jnp.dot(qb, state.astype(BF16), preferred_element_type=F32) * xi
        kv = lax.dot_general((kk * zeta).astype(BF16), vb, tn, preferred_element_type=F32)
        st_ref[...] = decay * state + kv
        mu = jnp.mean(o, axis=-1, keepdims=True)
        var = jnp.mean(jnp.square(o - mu), axis=-1, keepdims=True)
        y = (o - mu) * lax.rsqrt(var + NORM_EPS) * gg_ref[...] + gb_ref[...]
        gt = g_ref[sl, :]
        o_ref[sl, :] = (y * (gt * _sigmoid(gt))).astype(o_ref.dtype)


def _retention(proj, gn_g, gn_b):
    S = proj.shape[0]
    C = RET_CHUNK
    ts = 1024 if S % 1024 == 0 else C
    H = RET_HEADS
    log_g = jnp.log(1.0 - 2.0 ** (-5.0 - jnp.arange(H, dtype=F32)))
    i = jnp.arange(C, dtype=F32)
    diff = i[:, None] - i[None, :]
    dmat = jnp.where(diff >= 0, jnp.exp(log_g[:, None, None] * jnp.maximum(diff, 0.0)), 0.0)
    zeta = jnp.exp(log_g[:, None] * (C - 1.0 - i))[:, :, None]
    xi = jnp.exp(log_g[:, None] * (i + 1.0))[:, :, None]
    decay = jnp.broadcast_to(jnp.exp(log_g * C)[:, None, None], (H, 1, HD))

    def col(cb):
        return pl.BlockSpec((ts, HD), lambda h, n: (n, cb + h))

    return pl.pallas_call(
        _ret_kernel,
        out_shape=jax.ShapeDtypeStruct((S, H * HD), BF16),
        grid=(H, S // ts),
        in_specs=[col(CB_RQ), col(CB_RK), col(CB_RV), col(CB_RG),
                  pl.BlockSpec((1, C, C), lambda h, n: (h, 0, 0)),
                  pl.BlockSpec((1, C, 1), lambda h, n: (h, 0, 0)),
                  pl.BlockSpec((1, C, 1), lambda h, n: (h, 0, 0)),
                  pl.BlockSpec((1, 1, HD), lambda h, n: (h, 0, 0)),
                  pl.BlockSpec((1, HD), lambda h, n: (0, h)),
                  pl.BlockSpec((1, HD), lambda h, n: (0, h))],
        out_specs=pl.BlockSpec((ts, HD), lambda h, n: (n, h)),
        scratch_shapes=[pltpu.VMEM((HD, HD), F32)],
        compiler_params=_cparams(("arbitrary", "arbitrary")),
        name="retention",
    )(proj, proj, proj, proj, dmat, zeta, xi, decay, gn_g.reshape(1, -1), gn_b.reshape(1, -1))


def _outproj_kernel(x_ref, yc_ref, yn_ref, yr_ref, wc_ref, wn_ref, wr_ref, ga_ref, o_ref):
    y = jnp.dot(yc_ref[...], wc_ref[...], preferred_element_type=F32)
    y = y + jnp.dot(yn_ref[...], wn_ref[...], preferred_element_type=F32)
    y = y + jnp.dot(yr_ref[...], wr_ref[...], preferred_element_type=F32)
    o_ref[...] = x_ref[...] + ga_ref[...] * y


def _out_proj(x2, y_conv, y_nsa, y_ret, w_out_bf, gate_a):
    S = x2.shape[0]
    tm = 512
    wc, wn, wr = w_out_bf[:CONV_CH], w_out_bf[CONV_CH:CONV_CH + NSA_HEADS * HD], w_out_bf[CONV_CH + NSA_HEADS * HD:]

    def rows(w):
        return pl.BlockSpec((tm, w), lambda i: (i, 0))

    def whole(a):
        return pl.BlockSpec(a.shape, lambda i: (0, 0))

    return pl.pallas_call(
        _outproj_kernel,
        out_shape=jax.ShapeDtypeStruct((S, D_MODEL), F32),
        grid=(S // tm,),
        in_specs=[rows(D_MODEL), rows(y_conv.shape[1]), rows(y_nsa.shape[1]), rows(y_ret.shape[1]),
                  whole(wc), whole(wn), whole(wr), pl.BlockSpec((1, D_MODEL), lambda i: (0, 0))],
        out_specs=rows(D_MODEL),
        compiler_params=_cparams(("arbitrary",)),
        name="out_proj",
    )(x2, y_conv, y_nsa, y_ret, wc, wn, wr, gate_a)


def _router_kernel(x_ref, g_ref, sc_ref, sh_ref, rw_ref, rb_ref, h_ref, rt_ref, cnt_ref, carry_ref):
    i = pl.program_id(0)
    tm = x_ref.shape[0]

    @pl.when(i == 0)
    def _():
        carry_ref[...] = jnp.zeros_like(carry_ref)

    h = _modulated_norm(x_ref[...], g_ref[...], sc_ref[...], sh_ref[...])
    half = D_MODEL // 2
    hi = lax.bitcast_convert_type(h[:, :half].astype(BF16).astype(F32), jnp.uint32)
    lo = lax.bitcast_convert_type(h[:, half:].astype(BF16).astype(F32), jnp.uint32)
    h_ref[...] = hi | (lo >> 16)
    rw = rw_ref[...]
    rw_hi = rw.astype(BF16)
    rw_lo = (rw - rw_hi.astype(F32)).astype(BF16)
    h_hi = h.astype(BF16)
    h_lo = (h - h_hi.astype(F32)).astype(BF16)
    logits = (jnp.dot(h_hi, rw_hi, preferred_element_type=F32) + jnp.dot(h_hi, rw_lo, preferred_element_type=F32)
              + jnp.dot(h_lo, rw_hi, preferred_element_type=F32) + rb_ref[...])
    lane = lax.broadcasted_iota(jnp.int32, (tm, LANES), 1)
    lane_f = lane.astype(F32)
    work = jnp.where(lane < N_EXPERTS, logits, -jnp.inf)
    onehot = jnp.zeros((tm, LANES), F32)
    vals, idxs = [], []
    for _ in range(TOP_K):
        m = jnp.max(work, axis=-1, keepdims=True)
        first = jnp.min(jnp.where(work == m, lane_f, float(LANES)), axis=-1, keepdims=True)
        hit = lane_f == first
        onehot = jnp.where(hit, 1.0, onehot)
        work = jnp.where(hit, -jnp.inf, work)
        vals.append(m)
        idxs.append(first)
    ex = [jnp.exp(v - vals[0]) for v in vals]
    den = ex[0] + ex[1] + ex[2] + ex[3]
    r = lax.broadcasted_iota(jnp.int32, (tm, tm), 0)
    c = lax.broadcasted_iota(jnp.int32, (tm, tm), 1)
    tri = jnp.where(c < r, 1.0, 0.0).astype(BF16)
    cum = jnp.dot(tri, onehot.astype(BF16), preferred_element_type=F32) + carry_ref[...]
    out = jnp.zeros((tm, LANES), F32)
    for kk in range(TOP_K):
        rank = jnp.sum(jnp.where(lane_f == idxs[kk], cum, 0.0), axis=-1, keepdims=True)
        out = jnp.where(lane == kk, idxs[kk], out)
        out = jnp.where(lane == TOP_K + kk, ex[kk] / den, out)
        out = jnp.where(lane == 2 * TOP_K + kk, rank, out)
    rt_ref[...] = out
    carry_ref[...] = carry_ref[...] + jnp.sum(onehot, axis=0, keepdims=True)
    cnt_ref[...] = carry_ref[...]


def _router(x2, g, scale, shift, router_w, router_b):
    T = x2.shape[0]
    tm = 256
    rw = jnp.pad(router_w, ((0, 0), (0, LANES - N_EXPERTS)))
    rb = jnp.pad(router_b, (0, LANES - N_EXPERTS)).reshape(1, LANES)
    row = pl.BlockSpec((1, D_MODEL), lambda i: (0, 0))
    return pl.pallas_call(
        _router_kernel,
        out_shape=(jax.ShapeDtypeStruct((T, D_MODEL // 2), jnp.uint32),
                   jax.ShapeDtypeStruct((T, LANES), F32),
                   jax.ShapeDtypeStruct((1, LANES), F32)),
        grid=(T // tm,),
        in_specs=[pl.BlockSpec((tm, D_MODEL), lambda i: (i, 0)), row, row, row,
                  pl.BlockSpec((D_MODEL, LANES), lambda i: (0, 0)),
                  pl.BlockSpec((1, LANES), lambda i: (0, 0))],
        out_specs=(pl.BlockSpec((tm, D_MODEL // 2), lambda i: (i, 0)),
                   pl.BlockSpec((tm, LANES), lambda i: (i, 0)),
                   pl.BlockSpec((1, LANES), lambda i: (0, 0))),
        scratch_shapes=[pltpu.VMEM((1, LANES), F32)],
        compiler_params=_cparams(("arbitrary",)),
        name="moe_router",
    )(x2, g, scale, shift, rw, rb)


GATHER_UNROLL = 8


def _expert_kernel(ie_ref, ir_ref, ins_ref, tok_ref, hp_hbm, wgl_ref, wli_ref, bgl_ref, bli_ref, wdn_ref, bdn_ref,
                   ys_hbm, stage, xbf, acc, sem_g, sem_out):
    w = pl.program_id(0)
    ct = pl.program_id(1)
    n_w = pl.num_programs(0)
    n_ct = pl.num_programs(1)
    nsub = ins_ref[w]
    rows = stage.shape[0]
    per_step = rows // MOE_NCT
    half = D_MODEL // 2

    def gather_row(item, r):
        t = tok_ref[ir_ref[item] + r]
        pltpu.make_async_copy(hp_hbm.at[pl.ds(t, 1)], stage.at[pl.ds(r, 1)], sem_g).start()

    def y_copy(item, sb):
        row = pl.multiple_of(ir_ref[item] + sb * MOE_SUB, MOE_SUB)
        return pltpu.make_async_copy(acc.at[pl.ds(sb * MOE_SUB, MOE_SUB)],
                                     ys_hbm.at[pl.ds(row, MOE_SUB)], sem_out.at[sb])

    def for_subs(item, fn):
        n = ins_ref[item]
        for sb in range(MOE_SUBMAX):
            @pl.when(sb < n)
            def _():
                fn(item, sb)

    @pl.when(nsub > 0)
    def _():
        @pl.when((ct == 0) & (w == 0))
        def _():
            def body(c, carry):
                for u in range(GATHER_UNROLL):
                    gather_row(w, c * GATHER_UNROLL + u)
                return carry

            lax.fori_loop(0, rows // GATHER_UNROLL, body, 0)

        @pl.when(ct == 0)
        def _():
            for sb in range(MOE_SUBMAX):
                pltpu.make_async_copy(hp_hbm.at[pl.ds(0, MOE_SUB)], stage.at[pl.ds(sb * MOE_SUB, MOE_SUB)],
                                      sem_g).wait()
            for sb in range(MOE_SUBMAX):
                wv = stage[sb * MOE_SUB:(sb + 1) * MOE_SUB, :]
                xbf[sb * MOE_SUB:(sb + 1) * MOE_SUB, 0:half] = lax.bitcast_convert_type(
                    wv & jnp.uint32(0xFFFF0000), F32).astype(BF16)
                xbf[sb * MOE_SUB:(sb + 1) * MOE_SUB, half:D_MODEL] = lax.bitcast_convert_type(
                    wv << 16, F32).astype(BF16)

        nxt = jnp.minimum(w + 1, n_w - 1)

        for c in range(MOE_NCT):
            @pl.when((w + 1 < n_w) & (ins_ref[nxt] > 0) & (ct == c))
            def _():
                for u in range(per_step):
                    gather_row(nxt, c * per_step + u)

        def ffn_rows(r0, m_rows, first):
            x = xbf[pl.ds(r0, m_rows), :]
            glu = jnp.dot(x, wgl_ref[0, 0].astype(BF16), preferred_element_type=F32) + bgl_ref[0]
            lin = jnp.dot(x, wli_ref[0, 0].astype(BF16), preferred_element_type=F32) + bli_ref[0]
            glu = jnp.minimum(glu, SWIGLU_LIMIT)
            lin = jnp.clip(lin, -SWIGLU_LIMIT, SWIGLU_LIMIT)
            act = glu * _sigmoid(SWIGLU_ALPHA * glu) * (lin + 1.0)
            y = jnp.dot(act.astype(BF16), wdn_ref[0, 0].astype(BF16), preferred_element_type=F32)

            @pl.when(ct == 0)
            def _():
                @pl.when(first & (w > 0))
                def _():
                    for_subs(jnp.maximum(w - 1, 0), lambda it, sb: y_copy(it, sb).wait())

                acc[pl.ds(r0, m_rows), :] = y + bdn_ref[0]

            @pl.when(ct > 0)
            def _():
                acc[pl.ds(r0, m_rows), :] += y

        def quad_body(qd, carry):
            ffn_rows(pl.multiple_of(qd * (4 * MOE_SUB), 4 * MOE_SUB), 4 * MOE_SUB, qd == 0)
            return carry

        lax.fori_loop(0, nsub // 4, quad_body, 0)

        @pl.when(nsub % 4 >= 2)
        def _():
            ffn_rows(pl.multiple_of((nsub // 4) * (4 * MOE_SUB), 2 * MOE_SUB), 2 * MOE_SUB, nsub < 4)

        @pl.when(nsub % 2 == 1)
        def _():
            ffn_rows(pl.multiple_of((nsub - 1) * MOE_SUB, MOE_SUB), MOE_SUB, nsub == 1)

        @pl.when(ct == n_ct - 1)
        def _():
            for_subs(w, lambda it, sb: y_copy(it, sb).start())
            last = (w == n_w - 1) | (ins_ref[nxt] == 0)

            @pl.when(last)
            def _():
                for_subs(w, lambda it, sb: y_copy(it, sb).wait())


def _experts(hp, slot_tok, item_e, item_row0, item_nsub, w_gu, b_gu, w_dn, b_dn, layer):
    W = item_e.shape[0]
    te = MOE_TE
    n_ct = MOE_NCT
    rows = MOE_SUBMAX * MOE_SUB
    n_slots = slot_tok.shape[0] - rows
    assert rows % n_ct == 0

    def ct_eff(w, ct, ins):
        return jnp.where(ins[w] > 0, ct, n_ct - 1)

    return pl.pallas_call(
        _expert_kernel,
        out_shape=jax.ShapeDtypeStruct((n_slots, D_MODEL), F32),
        grid_spec=pltpu.PrefetchScalarGridSpec(
            num_scalar_prefetch=4,
            grid=(W, n_ct),
            in_specs=[pl.BlockSpec(memory_space=pl.ANY),
                      pl.BlockSpec((1, 1, D_MODEL, te),
                                   lambda w, ct, ie, ir, ins, tok: (layer, ie[w], 0, ct_eff(w, ct, ins))),
                      pl.BlockSpec((1, 1, D_MODEL, te),
                                   lambda w, ct, ie, ir, ins, tok: (layer, ie[w], 0, n_ct + ct_eff(w, ct, ins))),
                      pl.BlockSpec((1, 1, te), lambda w, ct, ie, ir, ins, tok: (ie[w], 0, ct_eff(w, ct, ins))),
                      pl.BlockSpec((1, 1, te), lambda w, ct, ie, ir, ins, tok: (ie[w], 0, n_ct + ct_eff(w, ct, ins))),
                      pl.BlockSpec((1, 1, te, D_MODEL),
                                   lambda w, ct, ie, ir, ins, tok: (layer, ie[w], ct_eff(w, ct, ins), 0)),
                      pl.BlockSpec((1, 1, D_MODEL), lambda w, ct, ie, ir, ins, tok: (ie[w], 0, 0))],
            out_specs=pl.BlockSpec(memory_space=pl.ANY),
            scratch_shapes=[pltpu.VMEM((rows, D_MODEL // 2), jnp.uint32), pltpu.VMEM((rows, D_MODEL), BF16),
                            pltpu.VMEM((rows, D_MODEL), F32),
                            pltpu.SemaphoreType.DMA(()), pltpu.SemaphoreType.DMA((MOE_SUBMAX,))]),
        compiler_params=_cparams(("arbitrary", "arbitrary"), vmem=60 * 1024 * 1024),
        name="moe_experts",
    )(item_e, item_row0, item_nsub, slot_tok, hp, w_gu, w_gu, b_gu.reshape(N_EXPERTS, 1, -1),
      b_gu.reshape(N_EXPERTS, 1, -1), w_dn, b_dn.reshape(N_EXPERTS, 1, -1))


def _combine_kernel(dest_ref, x_ref, rt_ref, gf_ref, fg_ref, ys_hbm, o_ref, buf, sem, *, final_norm):
    i = pl.program_id(0)
    n = pl.num_programs(0)
    tm = x_ref.shape[0]

    def issue(tile):
        slot = tile % 2

        base = tile * (tm * TOP_K)
        for r in range(tm):
            for kk in range(TOP_K):
                d = dest_ref[base + r * TOP_K + kk]
                pltpu.make_async_copy(ys_hbm.at[pl.ds(d, 1)], buf.at[slot, kk, pl.ds(r, 1)],
                                      sem.at[slot]).start()

    @pl.when(i == 0)
    def _():
        issue(i)

    @pl.when(i + 1 < n)
    def _():
        issue(i + 1)

    slot = i % 2
    for kk in range(TOP_K):
        pltpu.make_async_copy(ys_hbm.at[pl.ds(0, tm)], buf.at[slot, kk], sem.at[slot]).wait()
    rt = rt_ref[...]
    lane = lax.broadcasted_iota(jnp.int32, rt.shape, 1)
    moe = jnp.zeros((tm, D_MODEL), F32)
    for kk in range(TOP_K):
        moe = moe + buf[slot, kk] * _lane_pick(rt, lane, TOP_K + kk)
    out = x_ref[...] + gf_ref[...] * moe
    if final_norm:
        ms = jnp.mean(out * out, axis=-1, keepdims=True)
        out = out * lax.rsqrt(ms + NORM_EPS) * fg_ref[...]
    o_ref[...] = out


def _combine(x2, route, gate_f, final_g, ys, dest_flat, final_norm):
    T = x2.shape[0]
    tm = 128
    row = pl.BlockSpec((1, D_MODEL), lambda i, d: (0, 0))
    return pl.pallas_call(
        functools.partial(_combine_kernel, final_norm=final_norm),
        out_shape=jax.ShapeDtypeStruct((T, D_MODEL), F32),
        grid_spec=pltpu.PrefetchScalarGridSpec(
            num_scalar_prefetch=1,
            grid=(T // tm,),
            in_specs=[pl.BlockSpec((tm, D_MODEL), lambda i, d: (i, 0)),
                      pl.BlockSpec((tm, LANES), lambda i, d: (i, 0)), row, row,
                      pl.BlockSpec(memory_space=pl.ANY)],
            out_specs=pl.BlockSpec((tm, D_MODEL), lambda i, d: (i, 0)),
            scratch_shapes=[pltpu.VMEM((2, TOP_K, tm, D_MODEL), F32), pltpu.SemaphoreType.DMA((2,))]),
        compiler_params=_cparams(("arbitrary",)),
        name="moe_combine",
    )(dest_flat, x2, route, gate_f, final_g, ys)


def _moe_plan(route, counts_f):
    T = route.shape[0]
    e = route[:, 0:TOP_K].astype(jnp.int32)
    rank = route[:, 2 * TOP_K:3 * TOP_K].astype(jnp.int32)
    counts = counts_f[0, :N_EXPERTS].astype(jnp.int32)
    nsub = (counts + MOE_SUB - 1) // MOE_SUB
    padded = nsub * MOE_SUB
    pend = jnp.cumsum(padded)
    pstart = pend - padded
    dest = pstart[e] + rank
    n_slots = (T * TOP_K + MOE_SUB - 1) // MOE_SUB * MOE_SUB + N_EXPERTS * MOE_SUB
    tok = jnp.repeat(jnp.arange(T, dtype=jnp.int32), TOP_K)
    slot_tok = jnp.zeros((n_slots + MOE_SUBMAX * MOE_SUB,), jnp.int32).at[dest.reshape(-1)].set(
        tok, unique_indices=True, mode='promise_in_bounds')
    n_used = (pend[-1] // MOE_SUB).astype(jnp.int32).reshape(1)
    n_items_max = N_EXPERTS + (n_slots // MOE_SUB) // MOE_SUBMAX
    per_e = (nsub + MOE_SUBMAX - 1) // MOE_SUBMAX
    iend = jnp.cumsum(per_e)
    w = jnp.arange(n_items_max, dtype=jnp.int32)
    ew = jnp.minimum(jnp.searchsorted(iend, w, side='right'), N_EXPERTS - 1).astype(jnp.int32)
    local = w - (iend[ew] - per_e[ew])
    live = w < iend[-1]
    item_nsub = jnp.where(live, jnp.clip(nsub[ew] - local * MOE_SUBMAX, 0, MOE_SUBMAX), 0).astype(jnp.int32)
    item_row0 = jnp.where(live, pstart[ew] + local * (MOE_SUBMAX * MOE_SUB), 0).astype(jnp.int32)
    last_e = ew[jnp.maximum(iend[-1] - 1, 0)]
    item_e = jnp.where(live, ew, last_e).astype(jnp.int32)
    return dest.reshape(-1).astype(jnp.int32), slot_tok, n_used, item_e, item_row0, item_nsub


def _repack_w_in(w_in_l):
    parts, width = [], 0
    for name in _ORDER:
        off, size = _SRC[name]
        parts.append(w_in_l[:, off:off + size])
        width += size
        if size % LANES:
            parts.append(jnp.zeros((D_MODEL, LANES - size % LANES), w_in_l.dtype))
            width += LANES - size % LANES
    parts.append(jnp.zeros((D_MODEL, PROJ_W - width), w_in_l.dtype))
    return jnp.concatenate(parts, axis=1).astype(BF16)


def _cover_matrix(S):
    n_strip = S // CMP_STRIDE
    n_cmp = (S - CMP_BLOCK) // CMP_STRIDE + 1
    n = np.arange(n_strip)[:, None]
    j = np.arange(LANES)[None, :]
    start, end = n * CMP_STRIDE, n * CMP_STRIDE + CMP_BLOCK - 1
    cov = (start <= j * SEL_BLOCK + SEL_BLOCK - 1) & (end >= j * SEL_BLOCK) & (n < n_cmp) & (j < S // SEL_BLOCK)
    return jnp.asarray(cov.astype(np.float32).T)


def _strips(proj, cb):
    S = proj.shape[0]
    t = proj[:, cb * LANES:(cb + NSA_KV_HEADS) * LANES].reshape(S, NSA_KV_HEADS, HD)
    return t.transpose(1, 0, 2).reshape(NSA_KV_HEADS, S // CMP_STRIDE, CMP_STRIDE * HD)


def kernel(x, c, ada_w, ada_b, norm_mix_g, w_in, conv_dw_w, conv_dw_b, conv_ln_g, conv_ln_b, nsa_pe_k, nsa_pe_v, nsa_cmp_k_w1, nsa_cmp_k_w2, nsa_cmp_v_w1, nsa_cmp_v_w2, ret_gn_g, ret_gn_b, w_out, norm_ffn_g, router_w, router_b, moe_w_gate_up, moe_b_gate_up, moe_w_down, moe_b_down, final_norm_g):
    B, S, _ = x.shape
    assert B == 1 and c.shape[0] == 1
    x2 = x.reshape(S, D_MODEL)
    mod = _ada_mod(c, ada_w, ada_b).reshape(DEPTH, 6, 1, D_MODEL)
    tabs = _rotary_tables(S)
    cover = _cover_matrix(S)
    final_g = final_norm_g.reshape(1, D_MODEL)
    for l in range(DEPTH):
        shift_a, scale_a, gate_a, shift_f, scale_f, gate_f = [mod[l, i] for i in range(6)]
        proj = _in_proj(x2, norm_mix_g[l].reshape(1, -1), scale_a, shift_a, _repack_w_in(w_in[l]), tabs)
        y_conv = _conv_group(proj, conv_dw_w[l], conv_dw_b[l], conv_ln_g[l], conv_ln_b[l])
        kc = _compress(_strips(proj, CB_KC), nsa_pe_k[l], nsa_cmp_k_w1[l], nsa_cmp_k_w2[l])
        vc = _compress(_strips(proj, CB_VC), nsa_pe_v[l], nsa_cmp_v_w1[l], nsa_cmp_v_w2[l])
        oc, selbias = _cmp_attention(proj, kc, vc, cover)
        ksa, vsb, kwb, vwb = _kv_prep(proj)
        y_nsa = _nsa_attention(proj, selbias, oc, ksa, vsb, kwb, vwb)
        y_ret = _retention(proj, ret_gn_g[l], ret_gn_b[l])
        x2 = _out_proj(x2, y_conv, y_nsa, y_ret, w_out[l].astype(BF16), gate_a)
        h2, route, counts = _router(x2, norm_ffn_g[l].reshape(1, -1), scale_f, shift_f, router_w[l], router_b[l])
        dest, slot_tok, n_used, item_e, item_row0, item_nsub = _moe_plan(route, counts)
        ys = _experts(h2, slot_tok, item_e, item_row0, item_nsub, moe_w_gate_up, moe_b_gate_up[l], moe_w_down,
                      moe_b_down[l], l)
        x2 = _combine(x2, route, gate_f, final_g, ys, dest, final_norm=(l == DEPTH - 1))
    return x2.reshape(B, S, D_MODEL)
```

```python
import functools
import math

import numpy as np
import jax
import jax.numpy as jnp
from jax import lax
from jax.experimental import pallas as pl
from jax.experimental.pallas import tpu as pltpu

F32 = jnp.float32
BF16 = jnp.bfloat16
HI = lax.Precision.HIGHEST

D_MODEL = 2048
DEPTH = 2
CONV_CH = 512
CONV_WIDTH = 31
HD = 128
NSA_HEADS = 8
NSA_KV_HEADS = 2
NSA_GROUP = 4
CMP_BLOCK = 32
CMP_STRIDE = 16
SEL_BLOCK = 64
SEL_TOPN = 16
WINDOW = 512
ROPE_THETA = 500000.0
ROPE_DIM = 32
RET_HEADS = 4
RET_CHUNK = 128
RET_THETA = 10000.0
N_EXPERTS = 32
TOP_K = 4
D_EXPERT = 2048
SWIGLU_LIMIT = 7.0
SWIGLU_ALPHA = 1.702
NORM_EPS = 1e-6
NEG = -1e30
BIG = 1e30
LOG2E = math.log2(math.e)

LANES = 128
VMEM_LIMIT = 56 * 1024 * 1024

CB_Q, CB_KC, CB_KS, CB_KW = 0, 8, 10, 12
CB_RQ, CB_RK = 14, 18
CB_VC, CB_CV, CB_CG, CB_VS, CB_VW, CB_GT, CB_RV, CB_RG = 22, 24, 28, 32, 34, 36, 37, 41
PROJ_BLOCKS = 46
PROJ_W = PROJ_BLOCKS * LANES
PROJ_TN = 256
NSA_ROT_TILES = CB_RQ * LANES // PROJ_TN
RET_ROT_TILES = CB_VC * LANES // PROJ_TN
_SRC = {'cv': (0, 512), 'cg': (512, 512), 'q': (1024, 1024), 'kc': (2048, 256), 'vc': (2304, 256),
        'ks': (2560, 256), 'vs': (2816, 256), 'kw': (3072, 256), 'vw': (3328, 256), 'gt': (3584, 24),
        'rq': (3608, 512), 'rk': (4120, 512), 'rv': (4632, 512), 'rg': (5144, 512)}
_ORDER = ['q', 'kc', 'ks', 'kw', 'rq', 'rk', 'vc', 'cv', 'cg', 'vs', 'vw', 'gt', 'rv', 'rg']

MOE_SUB = 256
MOE_SUBMAX = 5
MOE_TE = 512
MOE_NCT = D_EXPERT // MOE_TE


def _sigmoid(x):
    return 1.0 / (1.0 + jnp.exp(-x))


def _cparams(sem, vmem=VMEM_LIMIT):
    return pltpu.CompilerParams(dimension_semantics=sem, vmem_limit_bytes=vmem)


def _ada_kernel(c_ref, w_ref, b_ref, o_ref, sc_ref):
    @pl.when(pl.program_id(1) == 0)
    def _():
        cv = c_ref[...]
        sc_ref[...] = cv * _sigmoid(cv)

    tn = o_ref.shape[-1]

    def body(i, acc):
        k0 = pl.multiple_of(i * 64, 64)
        p = w_ref[0, pl.ds(k0, 64), :] * sc_ref[pl.ds(k0, 64), :]
        return acc + p.reshape(8, 8, tn).sum(axis=0)

    acc = lax.fori_loop(0, D_MODEL // 64, body, jnp.zeros((8, tn), F32))
    o_ref[0] = jnp.sum(acc, axis=0, keepdims=True) + b_ref[0]


def _ada_mod(c, ada_w, ada_b):
    tn = 1024
    n = 6 * D_MODEL
    return pl.pallas_call(
        _ada_kernel,
        out_shape=jax.ShapeDtypeStruct((DEPTH, 1, n), F32),
        grid=(DEPTH, n // tn),
        in_specs=[pl.BlockSpec((D_MODEL, 1), lambda l, j: (0, 0)),
                  pl.BlockSpec((1, D_MODEL, tn), lambda l, j: (l, 0, j)),
                  pl.BlockSpec((1, 1, tn), lambda l, j: (l, 0, j))],
        out_specs=pl.BlockSpec((1, 1, tn), lambda l, j: (l, 0, j)),
        scratch_shapes=[pltpu.VMEM((D_MODEL, 1), F32)],
        compiler_params=_cparams(("arbitrary", "arbitrary")),
        name="ada_mod",
    )(c.reshape(D_MODEL, 1), ada_w, ada_b.reshape(DEPTH, 1, n))


def _modulated_norm(x, g, scale, shift):
    ms = jnp.mean(x * x, axis=-1, keepdims=True)
    return x * lax.rsqrt(ms + NORM_EPS) * g * (1.0 + scale) + shift


def _inproj_kernel(x_ref, g_ref, sc_ref, sh_ref, w_ref, cn_ref, s1_ref, s2_ref, cr_ref, sr_ref,
                   o_ref, h_ref):
    j = pl.program_id(1)
    nsub = o_ref.shape[-1] // LANES

    @pl.when(j == 0)
    def _():
        h_ref[...] = _modulated_norm(x_ref[...], g_ref[...], sc_ref[...], sh_ref[...]).astype(BF16)

    def matmul():
        return jnp.dot(h_ref[...], w_ref[...], preferred_element_type=F32)

    @pl.when(j < NSA_ROT_TILES)
    def _():
        acc = matmul()
        for c in range(nsub):
            sub = acc[:, c * LANES:(c + 1) * LANES]
            o_ref[:, c * LANES:(c + 1) * LANES] = (
                sub * cn_ref[...] + pltpu.roll(sub, LANES - ROPE_DIM // 2, 1) * s1_ref[...]
                + pltpu.roll(sub, ROPE_DIM // 2, 1) * s2_ref[...])

    @pl.when((j >= NSA_ROT_TILES) & (j < RET_ROT_TILES))
    def _():
        acc = matmul()
        for c in range(nsub):
            sub = acc[:, c * LANES:(c + 1) * LANES]
            o_ref[:, c * LANES:(c + 1) * LANES] = sub * cr_ref[...] + pltpu.roll(sub, HD // 2, 1) * sr_ref[...]

    @pl.when(j >= RET_ROT_TILES)
    def _():
        o_ref[...] = matmul()


def _in_proj(x2, g, scale, shift, w_bf, tabs):
    S = x2.shape[0]
    tm, tn = 1024, PROJ_TN
    row = pl.BlockSpec((1, D_MODEL), lambda i, j: (0, 0))
    tab = pl.BlockSpec((tm, LANES), lambda i, j: (i, 0))
    return pl.pallas_call(
        _inproj_kernel,
        out_shape=jax.ShapeDtypeStruct((S, PROJ_W), F32),
        grid=(S // tm, PROJ_W // tn),
        in_specs=[pl.BlockSpec((tm, D_MODEL), lambda i, j: (i, 0)), row, row, row,
                  pl.BlockSpec((D_MODEL, tn), lambda i, j: (0, j)), tab, tab, tab, tab, tab],
        out_specs=pl.BlockSpec((tm, tn), lambda i, j: (i, j)),
        scratch_shapes=[pltpu.VMEM((tm, D_MODEL), BF16)],
        compiler_params=_cparams(("arbitrary", "arbitrary")),
        name="in_proj",
    )(x2, g, scale, shift, w_bf, *tabs)


def _rotary_tables(S):
    pos = jnp.arange(S, dtype=F32)[:, None]
    half = ROPE_DIM // 2
    inv = ROPE_THETA ** (-jnp.arange(half, dtype=F32) * 2.0 / ROPE_DIM)
    ang = pos * inv[None, :]
    cos, sin = jnp.cos(ang), jnp.sin(ang)
    ones = jnp.ones((S, LANES - ROPE_DIM), F32)
    zeros_r = jnp.zeros((S, LANES - ROPE_DIM), F32)
    zeros_h = jnp.zeros((S, half), F32)
    cn = jnp.concatenate([cos, cos, ones], axis=1)
    s1 = jnp.concatenate([-sin, zeros_h, zeros_r], axis=1)
    s2 = jnp.concatenate([zeros_h, sin, zeros_r], axis=1)
    halfr = HD // 2
    invr = RET_THETA ** (-jnp.arange(halfr, dtype=F32) * 2.0 / HD)
    angr = pos * invr[None, :]
    cosr, sinr = jnp.cos(angr), jnp.sin(angr)
    cr = jnp.concatenate([cosr, cosr], axis=1)
    sr = jnp.concatenate([-sinr, sinr], axis=1)
    return cn, s1, s2, cr, sr


def _conv_kernel(cv_ref, cg_ref, dw_ref, db_ref, lg_ref, lb_ref, o_ref, hb_ref, xs_ref):
    i = pl.program_id(0)
    ts = o_ref.shape[0]
    halo = 32

    @pl.when(i == 0)
    def _():
        hb_ref[0:halo, :] = jnp.zeros((halo, CONV_CH), F32)

    @pl.when(i > 0)
    def _():
        hb_ref[0:halo, :] = hb_ref[ts:ts + halo, :]

    hb_ref[halo:halo + ts, :] = cv_ref[...] * _sigmoid(cg_ref[...])
    acc = jnp.zeros((ts, CONV_CH), F32) + db_ref[...]
    first = halo - (CONV_WIDTH - 1)
    for sh in range(8):
        offs = [first + w for w in range(CONV_WIDTH) if (first + w) % 8 == sh]
        span = max(offs) - sh + ts
        xs_ref[sh, 0:span, :] = hb_ref[sh:sh + span, :]
        for off in offs:
            acc = acc + xs_ref[sh, off - sh:off - sh + ts, :] * dw_ref[off - first:off - first + 1, :]
    mu = jnp.mean(acc, axis=-1, keepdims=True)
    var = jnp.mean(jnp.square(acc - mu), axis=-1, keepdims=True)
    y = (acc - mu) * lax.rsqrt(var + NORM_EPS) * lg_ref[...] + lb_ref[...]
    o_ref[...] = (y * _sigmoid(y)).astype(o_ref.dtype)


def _conv_group(proj, dw_w, dw_b, ln_g, ln_b):
    S = proj.shape[0]
    ts = 256
    vec = pl.BlockSpec((1, CONV_CH), lambda i: (0, 0))
    return pl.pallas_call(
        _conv_kernel,
        out_shape=jax.ShapeDtypeStruct((S, CONV_CH), BF16),
        grid=(S // ts,),
        in_specs=[pl.BlockSpec((ts, CONV_CH), lambda i: (i, CB_CV * LANES // CONV_CH)),
                  pl.BlockSpec((ts, CONV_CH), lambda i: (i, CB_CG * LANES // CONV_CH)),
                  pl.BlockSpec((CONV_WIDTH, CONV_CH), lambda i: (0, 0)), vec, vec, vec],
        out_specs=pl.BlockSpec((ts, CONV_CH), lambda i: (i, 0)),
        scratch_shapes=[pltpu.VMEM((ts + 32, CONV_CH), F32), pltpu.VMEM((8, ts + 32, CONV_CH), F32)],
        compiler_params=_cparams(("arbitrary",)),
        name="conv_group",
    )(proj, proj, dw_w, dw_b.reshape(1, -1), ln_g.reshape(1, -1), ln_b.reshape(1, -1))


def _compress_kernel(x_ref, pe_ref, w1_ref, w2_ref, o_ref):
    X = x_ref[0]
    nrow = X.shape[0]
    half = CMP_STRIDE * HD
    A = jnp.dot(X, w1_ref[0:half, :], precision=HI, preferred_element_type=F32)
    B = jnp.dot(X, w1_ref[half:2 * half, :], precision=HI, preferred_element_type=F32)
    pe8 = jnp.broadcast_to(pe_ref[...], (8, 2 * half))
    cst = jnp.dot(pe8, w1_ref[...], precision=HI, preferred_element_type=F32)[0:1]
    pre = A + pltpu.roll(B, nrow - 1, 0) + cst
    act = pre * _sigmoid(pre)
    out = jnp.dot(act, w2_ref[...], precision=HI, preferred_element_type=F32)
    rows = lax.broadcasted_iota(jnp.int32, out.shape, 0)
    o_ref[0] = jnp.where(rows < nrow - 1, out, 0.0)


def _compress(strips, pe, w1, w2):
    nh, nrow, width = strips.shape
    return pl.pallas_call(
        _compress_kernel,
        out_shape=jax.ShapeDtypeStruct((nh, nrow, HD), F32),
        grid=(nh,),
        in_specs=[pl.BlockSpec((1, nrow, width), lambda h: (h, 0, 0)),
                  pl.BlockSpec((1, width * 2), lambda h: (0, 0)),
                  pl.BlockSpec((width * 2, HD), lambda h: (0, 0)),
                  pl.BlockSpec((HD, HD), lambda h: (0, 0))],
        out_specs=pl.BlockSpec((1, nrow, HD), lambda h: (h, 0, 0)),
        compiler_params=_cparams(("arbitrary",)),
        name="nsa_compress",
    )(strips, pe.reshape(1, -1), w1, w2)


def _lane_pick(vals, lane, idx):
    return jnp.sum(jnp.where(lane == idx, vals, 0.0), axis=-1, keepdims=True)


def _cmp_kernel(q_ref, gt_ref, kc_ref, vc_ref, covt_ref, oc_ref, sb_ref, *, n_sel):
    k = pl.program_id(0)
    i = pl.program_id(1)
    tq = q_ref.shape[0]
    ncp = kc_ref.shape[1]
    scale = HD ** -0.5
    nt = (((1,), (1,)), ((), ()))
    tpos = i * tq + lax.broadcasted_iota(jnp.int32, (1, tq), 1)
    nrow = lax.broadcasted_iota(jnp.int32, (ncp, 1), 0)
    mc = (nrow * CMP_STRIDE + (CMP_BLOCK - 1)) <= tpos
    anyv = tpos >= (CMP_BLOCK - 1)
    kc = kc_ref[0]
    kc_hi = kc.astype(BF16)
    kc_lo = (kc - kc_hi.astype(F32)).astype(BF16)
    kc_cat = jnp.concatenate([kc_hi, kc_lo, kc_hi], axis=1)
    covt = covt_ref[...].astype(BF16)
    vct = vc_ref[0].T.astype(BF16)
    lane = lax.broadcasted_iota(jnp.int32, (tq, LANES), 1)
    sig = _sigmoid(gt_ref[...])
    psum = jnp.zeros((ncp, tq), F32)
    for g in range(NSA_GROUP):
        qg = q_ref[:, g * HD:(g + 1) * HD] * scale
        q_hi = qg.astype(BF16)
        q_lo = (qg - q_hi.astype(F32)).astype(BF16)
        s = lax.dot_general(kc_cat, jnp.concatenate([q_hi, q_hi, q_lo], axis=1), nt,
                            preferred_element_type=F32)
        s = jnp.where(mc, s, NEG)
        m = jnp.max(s, axis=0, keepdims=True)
        e = jnp.where(mc, jnp.exp(s - m), 0.0)
        l = jnp.sum(e, axis=0, keepdims=True)
        p = e / jnp.where(anyv, l, 1.0)
        ot = jnp.dot(vct, p.astype(BF16), preferred_element_type=F32)
        gate = _lane_pick(sig, lane, k * NSA_GROUP + g)
        oc_ref[:, g * HD:(g + 1) * HD] = gate * ot.T
        psum = psum + p
    p_hi = psum.astype(BF16)
    p_lo = (psum - p_hi.astype(F32)).astype(BF16)
    imp = (jnp.dot(covt, p_hi, preferred_element_type=F32)
           + jnp.dot(covt, p_lo, preferred_element_type=F32))
    blk = lax.broadcasted_iota(jnp.int32, (LANES, tq), 0)
    valid = (blk * SEL_BLOCK <= tpos) & (blk < n_sel)
    forced = (blk == 0) | (blk == tpos // SEL_BLOCK)
    work = jnp.where(forced, BIG, jnp.where(valid, imp, NEG))
    blk_f = blk.astype(F32)
    chosen = jnp.zeros((LANES, tq), F32)
    for _ in range(min(SEL_TOPN, n_sel)):
        m = jnp.max(work, axis=0, keepdims=True)
        first = jnp.min(jnp.where(work == m, blk_f, float(LANES)), axis=0, keepdims=True)
        hit = blk_f == first
        chosen = jnp.where(hit, 1.0, chosen)
        work = jnp.where(hit, -jnp.inf, work)
    keep = (chosen > 0.5) & valid
    sb_ref[0] = jnp.where(keep, 0.0, NEG).T.astype(BF16)


def _cmp_attention(proj, kc, vc, cover):
    S = proj.shape[0]
    tq = 256
    ncp = kc.shape[1]
    return pl.pallas_call(
        functools.partial(_cmp_kernel, n_sel=S // SEL_BLOCK),
        out_shape=(jax.ShapeDtypeStruct((S, NSA_HEADS * HD), F32),
                   jax.ShapeDtypeStruct((NSA_KV_HEADS, S, LANES), BF16)),
        grid=(NSA_KV_HEADS, S // tq),
        in_specs=[pl.BlockSpec((tq, NSA_GROUP * HD), lambda k, i: (i, CB_Q // NSA_GROUP + k)),
                  pl.BlockSpec((tq, LANES), lambda k, i: (i, CB_GT)),
                  pl.BlockSpec((1, ncp, HD), lambda k, i: (k, 0, 0)),
                  pl.BlockSpec((1, ncp, HD), lambda k, i: (k, 0, 0)),
                  pl.BlockSpec((LANES, ncp), lambda k, i: (0, 0))],
        out_specs=(pl.BlockSpec((tq, NSA_GROUP * HD), lambda k, i: (i, k)),
                   pl.BlockSpec((1, tq, LANES), lambda k, i: (k, i, 0))),
        compiler_params=_cparams(("arbitrary", "arbitrary")),
        name="nsa_cmp_select",
    )(proj, proj, kc, vc, cover)


def _kvprep_kernel(ks_ref, vs_ref, kw_ref, vw_ref, ksa_ref, vsb_ref, kwb_ref, vwb_ref):
    i = pl.program_id(1)
    ts = ks_ref.shape[0]
    rows = i * ts + lax.broadcasted_iota(jnp.int32, (ts, LANES), 0)
    lane = lax.broadcasted_iota(jnp.int32, (ts, LANES), 1)
    ksa_ref[0, :, 0:HD] = ks_ref[...].astype(BF16)
    ksa_ref[0, :, HD:2 * HD] = jnp.where(lane == rows // SEL_BLOCK, 1.0, 0.0).astype(BF16)
    vsb_ref[0, 0] = vs_ref[...].T.astype(BF16)
    kwb_ref[0] = kw_ref[...].astype(BF16)
    for t in range(ts // LANES):
        vwb_ref[0, t] = vw_ref[t * LANES:(t + 1) * LANES, :].T.astype(BF16)


def _kv_prep(proj):
    S = proj.shape[0]
    ts = NSA_TK
    nw = ts // LANES

    def col(cb):
        return pl.BlockSpec((ts, HD), lambda k, i: (i, cb + k))

    return pl.pallas_call(
        _kvprep_kernel,
        out_shape=(jax.ShapeDtypeStruct((NSA_KV_HEADS, S, 2 * HD), BF16),
                   jax.ShapeDtypeStruct((NSA_KV_HEADS, S // ts, HD, ts), BF16),
                   jax.ShapeDtypeStruct((NSA_KV_HEADS, S, HD), BF16),
                   jax.ShapeDtypeStruct((NSA_KV_HEADS, S // LANES, HD, LANES), BF16)),
        grid=(NSA_KV_HEADS, S // ts),
        in_specs=[col(CB_KS), col(CB_VS), col(CB_KW), col(CB_VW)],
        out_specs=(pl.BlockSpec((1, ts, 2 * HD), lambda k, i: (k, i, 0)),
                   pl.BlockSpec((1, 1, HD, ts), lambda k, i: (k, i, 0, 0)),
                   pl.BlockSpec((1, ts, HD), lambda k, i: (k, i, 0)),
                   pl.BlockSpec((1, nw, HD, LANES), lambda k, i: (k, i, 0, 0))),
        compiler_params=_cparams(("arbitrary", "arbitrary")),
        name="nsa_kv_prep",
    )(proj, proj, proj, proj)


NSA_TQ = 256
NSA_TK = 512
NSA_WK = WINDOW + NSA_TQ
NSA_CH = 32
NSA_PW = 256


def _nsa_kernel(q_ref, sb_ref, oc_ref, gt_ref, ks_ref, vst_ref, kw_ref, vwt_ref, y_ref, *scratch):
    k = pl.program_id(0)
    i = pl.program_id(1)
    tq = NSA_TQ
    cols = NSA_GROUP * tq
    q0 = i * tq
    nt = (((1,), (1,)), ((), ()))
    npart = cols // NSA_PW
    s_refs, p_refs, acc_refs = scratch[0:npart], scratch[npart:2 * npart], scratch[2 * npart:3 * npart]
    gpp = NSA_PW // LANES
    nsubq = tq // LANES
    qs = jnp.concatenate([q_ref[sub * LANES:(sub + 1) * LANES, g * HD:(g + 1) * HD]
                          for sub in range(nsubq) for g in range(NSA_GROUP)], axis=0)
    qs = (qs * (HD ** -0.5 * LOG2E)).astype(BF16)
    sbs = jnp.concatenate([sb_ref[0, sub * LANES:(sub + 1) * LANES, :]
                           for sub in range(nsubq) for _ in range(NSA_GROUP)], axis=0)
    qaug = jnp.concatenate([qs, sbs], axis=1)
    lane_q = lax.broadcasted_iota(jnp.int32, (1, LANES), 1)

    def qpos(c):
        return q0 + (c // NSA_GROUP) * LANES + lane_q

    def softmax_tile(n_keys, k0, m_old, mask_fn, c_list):
        def scores(r0, c):
            s = s_refs[c // gpp][r0:r0 + NSA_CH, (c % gpp) * LANES:(c % gpp + 1) * LANES]
            if mask_fn is None:
                return s
            kpos = k0 + r0 + lax.broadcasted_iota(jnp.int32, (NSA_CH, 1), 0)
            return jnp.where(mask_fn(kpos, qpos(c)), s, NEG)

        def fold(x):
            return x.reshape(NSA_CH // 8, 8, LANES)

        m_new, sums = [], []
        for c in c_list:
            mx8 = jnp.max(fold(scores(0, c)), axis=0)
            for r0 in range(NSA_CH, n_keys, NSA_CH):
                mx8 = jnp.maximum(mx8, jnp.max(fold(scores(r0, c)), axis=0))
            mx = jnp.max(mx8, axis=0, keepdims=True)
            mc = mx if m_old is None else jnp.maximum(m_old[:, c * LANES:(c + 1) * LANES], mx)
            tot8 = jnp.zeros((8, LANES), F32)
            for r0 in range(0, n_keys, NSA_CH):
                p = jnp.exp2(scores(r0, c) - mc)
                tot8 = tot8 + jnp.sum(fold(p), axis=0)
                p_refs[c // gpp][r0:r0 + NSA_CH, (c % gpp) * LANES:(c % gpp + 1) * LANES] = p.astype(BF16)
            m_new.append(mc)
            sums.append(jnp.sum(tot8, axis=0, keepdims=True))
        return m_new, sums

    for a in acc_refs:
        a[...] = jnp.zeros((HD, NSA_PW), F32)

    def sel_tile(j, carry, causal):
        m_old, l_old = carry
        k0 = pl.multiple_of(j * NSA_TK, NSA_TK)
        kt = ks_ref[0, pl.ds(k0, NSA_TK), :]
        vt = vst_ref[0, j]
        mask_fn = (lambda kpos, qp: kpos <= qp) if causal else None
        halves = [slice(h * NSA_PW, (h + 1) * NSA_PW) for h in range(npart)]
        for h, hs in enumerate(halves):
            s_refs[h][0:NSA_TK, :] = lax.dot_general(kt, qaug[hs], nt, preferred_element_type=F32)
        m_out, l_out = [], []
        for h, hs in enumerate(halves):
            m_new, sums = softmax_tile(NSA_TK, k0, m_old, mask_fn, list(range(h * gpp, (h + 1) * gpp)))
            m_new, sums = jnp.concatenate(m_new, axis=1), jnp.concatenate(sums, axis=1)
            alpha = jnp.exp2(m_old[:, hs] - m_new)
            acc_refs[h][...] = acc_refs[h][...] * alpha + jnp.dot(vt, p_refs[h][0:NSA_TK, :],
                                                                  preferred_element_type=F32)
            m_out.append(m_new)
            l_out.append(alpha * l_old[:, hs] + sums)
        return jnp.concatenate(m_out, axis=1), jnp.concatenate(l_out, axis=1)

    n_full = (q0 + tq - 1) // NSA_TK
    init = (jnp.full((1, cols), NEG, F32), jnp.zeros((1, cols), F32))
    carry = lax.fori_loop(0, n_full, lambda j, c: sel_tile(j, c, False), init)
    _, l_s = sel_tile(n_full, carry, True)
    ot_s = jnp.concatenate([a[...] for a in acc_refs], axis=1) / l_s

    w0 = pl.multiple_of(jnp.maximum(q0 - WINDOW, 0), tq)
    kwt = kw_ref[0, pl.ds(w0, NSA_WK), :]
    for h in range(npart):
        s_refs[h][...] = lax.dot_general(kwt, qs[h * NSA_PW:(h + 1) * NSA_PW], nt, preferred_element_type=F32)
    _, l_w = softmax_tile(NSA_WK, w0, None, lambda kpos, qp: (kpos <= qp) & (kpos > qp - WINDOW),
                          list(range(cols // LANES)))
    l_w = jnp.concatenate(l_w, axis=1)
    ot_w = []
    for h in range(npart):
        o = jnp.zeros((HD, NSA_PW), F32)
        for t in range(NSA_WK // LANES):
            o = o + jnp.dot(vwt_ref[0, w0 // LANES + t], p_refs[h][t * LANES:(t + 1) * LANES, :],
                            preferred_element_type=F32)
        ot_w.append(o)
    ot_w = jnp.concatenate(ot_w, axis=1) / l_w

    lane = lax.broadcasted_iota(jnp.int32, (LANES, LANES), 1)
    for sub in range(nsubq):
        rs = slice(sub * LANES, (sub + 1) * LANES)
        sig = _sigmoid(gt_ref[rs, :])
        for g in range(NSA_GROUP):
            head = k * NSA_GROUP + g
            cs = slice((sub * NSA_GROUP + g) * LANES, (sub * NSA_GROUP + g + 1) * LANES)
            g_s = _lane_pick(sig, lane, NSA_HEADS + head)
            g_w = _lane_pick(sig, lane, 2 * NSA_HEADS + head)
            y = oc_ref[rs, g * HD:(g + 1) * HD] + g_s * ot_s[:, cs].T + g_w * ot_w[:, cs].T
            y_ref[rs, g * HD:(g + 1) * HD] = y.astype(y_ref.dtype)


def _nsa_attention(proj, selbias, oc, ksa, vst, kwb, vwt):
    S = proj.shape[0]
    tq = NSA_TQ
    cols = NSA_GROUP * tq
    npart = cols // NSA_PW
    assert S >= NSA_WK and tq % LANES == 0 and NSA_TK % tq == 0

    def full(a):
        return pl.BlockSpec((1,) + a.shape[1:], lambda k, i: (k,) + (0,) * (a.ndim - 1))

    return pl.pallas_call(
        _nsa_kernel,
        out_shape=jax.ShapeDtypeStruct((S, NSA_HEADS * HD), BF16),
        grid=(NSA_KV_HEADS, S // tq),
        in_specs=[pl.BlockSpec((tq, NSA_GROUP * HD), lambda k, i: (i, CB_Q // NSA_GROUP + k)),
                  pl.BlockSpec((1, tq, LANES), lambda k, i: (k, i, 0)),
                  pl.BlockSpec((tq, NSA_GROUP * HD), lambda k, i: (i, k)),
                  pl.BlockSpec((tq, LANES), lambda k, i: (i, CB_GT)),
                  full(ksa), full(vst), full(kwb), full(vwt)],
        out_specs=pl.BlockSpec((tq, NSA_GROUP * HD), lambda k, i: (i, k)),
        scratch_shapes=([pltpu.VMEM((NSA_WK, NSA_PW), F32)] * npart + [pltpu.VMEM((NSA_WK, NSA_PW), BF16)] * npart
                        + [pltpu.VMEM((HD, NSA_PW), F32)] * npart),
        compiler_params=_cparams(("arbitrary", "arbitrary")),
        name="nsa_sel_win",
    )(proj, selbias, oc, proj, ksa, vst, kwb, vwt)


def _ret_kernel(q_ref, k_ref, v_ref, g_ref, dm_ref, ze_ref, xi_ref, dc_ref, gg_ref, gb_ref, o_ref, st_ref):
    n = pl.program_id(1)
    C = RET_CHUNK
    nt = (((1,), (1,)), ((), ()))
    tn = (((0,), (0,)), ((), ()))

    @pl.when(n == 0)
    def _():
        st_ref[...] = jnp.zeros_like(st_ref)

    dmat = dm_ref[0]
    zeta = ze_ref[0]
    xi = xi_ref[0]
    decay = dc_ref[0]
    for c in range(q_ref.shape[0] // C):
        sl = slice(c * C, (c + 1) * C)
        q = q_ref[sl, :]
        kk = k_ref[sl, :] * (HD ** -0.5)
        v = v_ref[sl, :]
        qb, kb, vb = q.astype(BF16), kk.astype(BF16), v.astype(BF16)
        inner = lax.dot_general(qb, kb, nt, preferred_element_type=F32) * dmat
        o = jnp.dot(inner.astype(BF16), vb, preferred_element_type=F32)
        state = st_ref[...]
        o = o + jnp.dot(qb, state.astype(BF16), preferred_element_type=F32) * xi
        kv = lax.dot_general((kk * zeta).astype(BF16), vb, tn, preferred_element_type=F32)
        st_ref[...] = decay * state + kv
        mu = jnp.mean(o, axis=-1, keepdims=True)
        var = jnp.mean(jnp.square(o - mu), axis=-1, keepdims=True)
        y = (o - mu) * lax.rsqrt(var + NORM_EPS) * gg_ref[...] + gb_ref[...]
        gt = g_ref[sl, :]
        o_ref[sl, :] = (y * (gt * _sigmoid(gt))).astype(o_ref.dtype)


def _retention(proj, gn_g, gn_b):
    S = proj.shape[0]
    C = RET_CHUNK
    ts = 1024 if S % 1024 == 0 else C
    H = RET_HEADS
    log_g = jnp.log(1.0 - 2.0 ** (-5.0 - jnp.arange(H, dtype=F32)))
    i = jnp.arange(C, dtype=F32)
    diff = i[:, None] - i[None, :]
    dmat = jnp.where(diff >= 0, jnp.exp(log_g[:, None, None] * jnp.maximum(diff, 0.0)), 0.0)
    zeta = jnp.exp(log_g[:, None] * (C - 1.0 - i))[:, :, None]
    xi = jnp.exp(log_g[:, None] * (i + 1.0))[:, :, None]
    decay = jnp.broadcast_to(jnp.exp(log_g * C)[:, None, None], (H, 1, HD))

    def col(cb):
        return pl.BlockSpec((ts, HD), lambda h, n: (n, cb + h))

    return pl.pallas_call(
        _ret_kernel,
        out_shape=jax.ShapeDtypeStruct((S, H * HD), BF16),
        grid=(H, S // ts),
        in_specs=[col(CB_RQ), col(CB_RK), col(CB_RV), col(CB_RG),
                  pl.BlockSpec((1, C, C), lambda h, n: (h, 0, 0)),
                  pl.BlockSpec((1, C, 1), lambda h, n: (h, 0, 0)),
                  pl.BlockSpec((1, C, 1), lambda h, n: (h, 0, 0)),
                  pl.BlockSpec((1, 1, HD), lambda h, n: (h, 0, 0)),
                  pl.BlockSpec((1, HD), lambda h, n: (0, h)),
                  pl.BlockSpec((1, HD), lambda h, n: (0, h))],
        out_specs=pl.BlockSpec((ts, HD), lambda h, n: (n, h)),
        scratch_shapes=[pltpu.VMEM((HD, HD), F32)],
        compiler_params=_cparams(("arbitrary", "arbitrary")),
        name="retention",
    )(proj, proj, proj, proj, dmat, zeta, xi, decay, gn_g.reshape(1, -1), gn_b.reshape(1, -1))


def _outproj_kernel(x_ref, yc_ref, yn_ref, yr_ref, wc_ref, wn_ref, wr_ref, ga_ref, o_ref):
    y = jnp.dot(yc_ref[...], wc_ref[...], preferred_element_type=F32)
    y = y + jnp.dot(yn_ref[...], wn_ref[...], preferred_element_type=F32)
    y = y + jnp.dot(yr_ref[...], wr_ref[...], preferred_element_type=F32)
    o_ref[...] = x_ref[...] + ga_ref[...] * y


def _out_proj(x2, y_conv, y_nsa, y_ret, w_out_bf, gate_a):
    S = x2.shape[0]
    tm = 512
    wc, wn, wr = w_out_bf[:CONV_CH], w_out_bf[CONV_CH:CONV_CH + NSA_HEADS * HD], w_out_bf[CONV_CH + NSA_HEADS * HD:]

    def rows(w):
        return pl.BlockSpec((tm, w), lambda i: (i, 0))

    def whole(a):
        return pl.BlockSpec(a.shape, lambda i: (0, 0))

    return pl.pallas_call(
        _outproj_kernel,
        out_shape=jax.ShapeDtypeStruct((S, D_MODEL), F32),
        grid=(S // tm,),
        in_specs=[rows(D_MODEL), rows(y_conv.shape[1]), rows(y_nsa.shape[1]), rows(y_ret.shape[1]),
                  whole(wc), whole(wn), whole(wr), pl.BlockSpec((1, D_MODEL), lambda i: (0, 0))],
        out_specs=rows(D_MODEL),
        compiler_params=_cparams(("arbitrary",)),
        name="out_proj",
    )(x2, y_conv, y_nsa, y_ret, wc, wn, wr, gate_a)


def _router_kernel(x_ref, g_ref, sc_ref, sh_ref, rw_ref, rb_ref, h_ref, rt_ref, cnt_ref, carry_ref):
    i = pl.program_id(0)
    tm = x_ref.shape[0]

    @pl.when(i == 0)
    def _():
        carry_ref[...] = jnp.zeros_like(carry_ref)

    h = _modulated_norm(x_ref[...], g_ref[...], sc_ref[...], sh_ref[...])
    half = D_MODEL // 2
    hi = lax.bitcast_convert_type(h[:, :half].astype(BF16).astype(F32), jnp.uint32)
    lo = lax.bitcast_convert_type(h[:, half:].astype(BF16).astype(F32), jnp.uint32)
    h_ref[...] = hi | (lo >> 16)
    rw = rw_ref[...]
    rw_hi = rw.astype(BF16)
    rw_lo = (rw - rw_hi.astype(F32)).astype(BF16)
    h_hi = h.astype(BF16)
    h_lo = (h - h_hi.astype(F32)).astype(BF16)
    logits = (jnp.dot(h_hi, rw_hi, preferred_element_type=F32) + jnp.dot(h_hi, rw_lo, preferred_element_type=F32)
              + jnp.dot(h_lo, rw_hi, preferred_element_type=F32) + rb_ref[...])
    lane = lax.broadcasted_iota(jnp.int32, (tm, LANES), 1)
    lane_f = lane.astype(F32)
    work = jnp.where(lane < N_EXPERTS, logits, -jnp.inf)
    onehot = jnp.zeros((tm, LANES), F32)
    vals, idxs = [], []
    for _ in range(TOP_K):
        m = jnp.max(work, axis=-1, keepdims=True)
        first = jnp.min(jnp.where(work == m, lane_f, float(LANES)), axis=-1, keepdims=True)
        hit = lane_f == first
        onehot = jnp.where(hit, 1.0, onehot)
        work = jnp.where(hit, -jnp.inf, work)
        vals.append(m)
        idxs.append(first)
    ex = [jnp.exp(v - vals[0]) for v in vals]
    den = ex[0] + ex[1] + ex[2] + ex[3]
    r = lax.broadcasted_iota(jnp.int32, (tm, tm), 0)
    c = lax.broadcasted_iota(jnp.int32, (tm, tm), 1)
    tri = jnp.where(c < r, 1.0, 0.0).astype(BF16)
    cum = jnp.dot(tri, onehot.astype(BF16), preferred_element_type=F32) + carry_ref[...]
    out = jnp.zeros((tm, LANES), F32)
    for kk in range(TOP_K):
        rank = jnp.sum(jnp.where(lane_f == idxs[kk], cum, 0.0), axis=-1, keepdims=True)
        out = jnp.where(lane == kk, idxs[kk], out)
        out = jnp.where(lane == TOP_K + kk, ex[kk] / den, out)
        out = jnp.where(lane == 2 * TOP_K + kk, rank, out)
    rt_ref[...] = out
    carry_ref[...] = carry_ref[...] + jnp.sum(onehot, axis=0, keepdims=True)
    cnt_ref[...] = carry_ref[...]


def _router(x2, g, scale, shift, router_w, router_b):
    T = x2.shape[0]
    tm = 256
    rw = jnp.pad(router_w, ((0, 0), (0, LANES - N_EXPERTS)))
    rb = jnp.pad(router_b, (0, LANES - N_EXPERTS)).reshape(1, LANES)
    row = pl.BlockSpec((1, D_MODEL), lambda i: (0, 0))
    return pl.pallas_call(
        _router_kernel,
        out_shape=(jax.ShapeDtypeStruct((T, D_MODEL // 2), jnp.uint32),
                   jax.ShapeDtypeStruct((T, LANES), F32),
                   jax.ShapeDtypeStruct((1, LANES), F32)),
        grid=(T // tm,),
        in_specs=[pl.BlockSpec((tm, D_MODEL), lambda i: (i, 0)), row, row, row,
                  pl.BlockSpec((D_MODEL, LANES), lambda i: (0, 0)),
                  pl.BlockSpec((1, LANES), lambda i: (0, 0))],
        out_specs=(pl.BlockSpec((tm, D_MODEL // 2), lambda i: (i, 0)),
                   pl.BlockSpec((tm, LANES), lambda i: (i, 0)),
                   pl.BlockSpec((1, LANES), lambda i: (0, 0))),
        scratch_shapes=[pltpu.VMEM((1, LANES), F32)],
        compiler_params=_cparams(("arbitrary",)),
        name="moe_router",
    )(x2, g, scale, shift, rw, rb)


GATHER_UNROLL = 8


def _expert_kernel(ie_ref, ir_ref, ins_ref, tok_ref, hp_hbm, wgl_ref, wli_ref, bgl_ref, bli_ref, wdn_ref, bdn_ref,
                   ys_hbm, stage, xbf, acc, sem_g, sem_out):
    w = pl.program_id(0)
    ct = pl.program_id(1)
    n_w = pl.num_programs(0)
    n_ct = pl.num_programs(1)
    nsub = ins_ref[w]
    rows = stage.shape[0]
    per_step = rows // MOE_NCT
    half = D_MODEL // 2

    def gather_row(item, r):
        t = tok_ref[ir_ref[item] + r]
        pltpu.make_async_copy(hp_hbm.at[pl.ds(t, 1)], stage.at[pl.ds(r, 1)], sem_g).start()

    def y_copy(item, sb):
        row = pl.multiple_of(ir_ref[item] + sb * MOE_SUB, MOE_SUB)
        return pltpu.make_async_copy(acc.at[pl.ds(sb * MOE_SUB, MOE_SUB)],
                                     ys_hbm.at[pl.ds(row, MOE_SUB)], sem_out.at[sb])

    def for_subs(item, fn):
        n = ins_ref[item]
        for sb in range(MOE_SUBMAX):
            @pl.when(sb < n)
            def _():
                fn(item, sb)

    @pl.when(nsub > 0)
    def _():
        @pl.when((ct == 0) & (w == 0))
        def _():
            def body(c, carry):
                for u in range(GATHER_UNROLL):
                    gather_row(w, c * GATHER_UNROLL + u)
                return carry

            lax.fori_loop(0, rows // GATHER_UNROLL, body, 0)

        @pl.when(ct == 0)
        def _():
            for sb in range(MOE_SUBMAX):
                pltpu.make_async_copy(hp_hbm.at[pl.ds(0, MOE_SUB)], stage.at[pl.ds(sb * MOE_SUB, MOE_SUB)],
                                      sem_g).wait()
            for sb in range(MOE_SUBMAX):
                wv = stage[sb * MOE_SUB:(sb + 1) * MOE_SUB, :]
                xbf[sb * MOE_SUB:(sb + 1) * MOE_SUB, 0:half] = lax.bitcast_convert_type(
                    wv & jnp.uint32(0xFFFF0000), F32).astype(BF16)
                xbf[sb * MOE_SUB:(sb + 1) * MOE_SUB, half:D_MODEL] = lax.bitcast_convert_type(
                    wv << 16, F32).astype(BF16)

        nxt = jnp.minimum(w + 1, n_w - 1)

        for c in range(MOE_NCT):
            @pl.when((w + 1 < n_w) & (ins_ref[nxt] > 0) & (ct == c))
            def _():
                for u in range(per_step):
                    gather_row(nxt, c * per_step + u)

        def ffn_rows(r0, m_rows, first):
            x = xbf[pl.ds(r0, m_rows), :]
            glu = jnp.dot(x, wgl_ref[0, 0].astype(BF16), preferred_element_type=F32) + bgl_ref[0]
            lin = jnp.dot(x, wli_ref[0, 0].astype(BF16), preferred_element_type=F32) + bli_ref[0]
            glu = jnp.minimum(glu, SWIGLU_LIMIT)
            lin = jnp.clip(lin, -SWIGLU_LIMIT, SWIGLU_LIMIT)
            act = glu * _sigmoid(SWIGLU_ALPHA * glu) * (lin + 1.0)
            y = jnp.dot(act.astype(BF16), wdn_ref[0, 0].astype(BF16), preferred_element_type=F32)

            @pl.when(ct == 0)
            def _():
                @pl.when(first & (w > 0))
                def _():
                    for_subs(jnp.maximum(w - 1, 0), lambda it, sb: y_copy(it, sb).wait())

                acc[pl.ds(r0, m_rows), :] = y + bdn_ref[0]

            @pl.when(ct > 0)
            def _():
                acc[pl.ds(r0, m_rows), :] += y

        def quad_body(qd, carry):
            ffn_rows(pl.multiple_of(qd * (4 * MOE_SUB), 4 * MOE_SUB), 4 * MOE_SUB, qd == 0)
            return carry

        lax.fori_loop(0, nsub // 4, quad_body, 0)

        @pl.when(nsub % 4 >= 2)
        def _():
            ffn_rows(pl.multiple_of((nsub // 4) * (4 * MOE_SUB), 2 * MOE_SUB), 2 * MOE_SUB, nsub < 4)

        @pl.when(nsub % 2 == 1)
        def _():
            ffn_rows(pl.multiple_of((nsub - 1) * MOE_SUB, MOE_SUB), MOE_SUB, nsub == 1)

        @pl.when(ct == n_ct - 1)
        def _():
            for_subs(w, lambda it, sb: y_copy(it, sb).start())
            last = (w == n_w - 1) | (ins_ref[nxt] == 0)

            @pl.when(last)
            def _():
                for_subs(w, lambda it, sb: y_copy(it, sb).wait())


def _experts(hp, slot_tok, item_e, item_row0, item_nsub, w_gu, b_gu, w_dn, b_dn, layer):
    W = item_e.shape[0]
    te = MOE_TE
    n_ct = MOE_NCT
    rows = MOE_SUBMAX * MOE_SUB
    n_slots = slot_tok.shape[0] - rows
    assert rows % n_ct == 0

    def ct_eff(w, ct, ins):
        return jnp.where(ins[w] > 0, ct, n_ct - 1)

    return pl.pallas_call(
        _expert_kernel,
        out_shape=jax.ShapeDtypeStruct((n_slots, D_MODEL), F32),
        grid_spec=pltpu.PrefetchScalarGridSpec(
            num_scalar_prefetch=4,
            grid=(W, n_ct),
            in_specs=[pl.BlockSpec(memory_space=pl.ANY),
                      pl.BlockSpec((1, 1, D_MODEL, te),
                                   lambda w, ct, ie, ir, ins, tok: (layer, ie[w], 0, ct_eff(w, ct, ins))),
                      pl.BlockSpec((1, 1, D_MODEL, te),
                                   lambda w, ct, ie, ir, ins, tok: (layer, ie[w], 0, n_ct + ct_eff(w, ct, ins))),
                      pl.BlockSpec((1, 1, te), lambda w, ct, ie, ir, ins, tok: (ie[w], 0, ct_eff(w, ct, ins))),
                      pl.BlockSpec((1, 1, te), lambda w, ct, ie, ir, ins, tok: (ie[w], 0, n_ct + ct_eff(w, ct, ins))),
                      pl.BlockSpec((1, 1, te, D_MODEL),
                                   lambda w, ct, ie, ir, ins, tok: (layer, ie[w], ct_eff(w, ct, ins), 0)),
                      pl.BlockSpec((1, 1, D_MODEL), lambda w, ct, ie, ir, ins, tok: (ie[w], 0, 0))],
            out_specs=pl.BlockSpec(memory_space=pl.ANY),
            scratch_shapes=[pltpu.VMEM((rows, D_MODEL // 2), jnp.uint32), pltpu.VMEM((rows, D_MODEL), BF16),
                            pltpu.VMEM((rows, D_MODEL), F32),
                            pltpu.SemaphoreType.DMA(()), pltpu.SemaphoreType.DMA((MOE_SUBMAX,))]),
        compiler_params=_cparams(("arbitrary", "arbitrary"), vmem=60 * 1024 * 1024),
        name="moe_experts",
    )(item_e, item_row0, item_nsub, slot_tok, hp, w_gu, w_gu, b_gu.reshape(N_EXPERTS, 1, -1),
      b_gu.reshape(N_EXPERTS, 1, -1), w_dn, b_dn.reshape(N_EXPERTS, 1, -1))


def _combine_kernel(dest_ref, x_ref, rt_ref, gf_ref, fg_ref, ys_hbm, o_ref, buf, sem, *, final_norm):
    i = pl.program_id(0)
    n = pl.num_programs(0)
    tm = x_ref.shape[0]

    def issue(tile):
        slot = tile % 2

        base = tile * (tm * TOP_K)
        for r in range(tm):
            for kk in range(TOP_K):
                d = dest_ref[base + r * TOP_K + kk]
                pltpu.make_async_copy(ys_hbm.at[pl.ds(d, 1)], buf.at[slot, kk, pl.ds(r, 1)],
                                      sem.at[slot]).start(priority=(r * TOP_K + kk) % 2)

    @pl.when(i == 0)
    def _():
        issue(i)

    @pl.when(i + 1 < n)
    def _():
        issue(i + 1)

    slot = i % 2
    for kk in range(TOP_K):
        pltpu.make_async_copy(ys_hbm.at[pl.ds(0, tm)], buf.at[slot, kk], sem.at[slot]).wait()
    rt = rt_ref[...]
    lane = lax.broadcasted_iota(jnp.int32, rt.shape, 1)
    moe = jnp.zeros((tm, D_MODEL), F32)
    for kk in range(TOP_K):
        moe = moe + buf[slot, kk] * _lane_pick(rt, lane, TOP_K + kk)
    out = x_ref[...] + gf_ref[...] * moe
    if final_norm:
        ms = jnp.mean(out * out, axis=-1, keepdims=True)
        out = out * lax.rsqrt(ms + NORM_EPS) * fg_ref[...]
    o_ref[...] = out


def _combine(x2, route, gate_f, final_g, ys, dest_flat, final_norm):
    T = x2.shape[0]
    tm = 128
    row = pl.BlockSpec((1, D_MODEL), lambda i, d: (0, 0))
    return pl.pallas_call(
        functools.partial(_combine_kernel, final_norm=final_norm),
        out_shape=jax.ShapeDtypeStruct((T, D_MODEL), F32),
        grid_spec=pltpu.PrefetchScalarGridSpec(
            num_scalar_prefetch=1,
            grid=(T // tm,),
            in_specs=[pl.BlockSpec((tm, D_MODEL), lambda i, d: (i, 0)),
                      pl.BlockSpec((tm, LANES), lambda i, d: (i, 0)), row, row,
                      pl.BlockSpec(memory_space=pl.ANY)],
            out_specs=pl.BlockSpec((tm, D_MODEL), lambda i, d: (i, 0)),
            scratch_shapes=[pltpu.VMEM((2, TOP_K, tm, D_MODEL), F32), pltpu.SemaphoreType.DMA((2,))]),
        compiler_params=_cparams(("arbitrary",)),
        name="moe_combine",
    )(dest_flat, x2, route, gate_f, final_g, ys)


def _moe_plan(route, counts_f):
    T = route.shape[0]
    e = route[:, 0:TOP_K].astype(jnp.int32)
    rank = route[:, 2 * TOP_K:3 * TOP_K].astype(jnp.int32)
    counts = counts_f[0, :N_EXPERTS].astype(jnp.int32)
    nsub = (counts + MOE_SUB - 1) // MOE_SUB
    padded = nsub * MOE_SUB
    pend = jnp.cumsum(padded)
    pstart = pend - padded
    dest = pstart[e] + rank
    n_slots = (T * TOP_K + MOE_SUB - 1) // MOE_SUB * MOE_SUB + N_EXPERTS * MOE_SUB
    tok = jnp.repeat(jnp.arange(T, dtype=jnp.int32), TOP_K)
    slot_tok = jnp.zeros((n_slots + MOE_SUBMAX * MOE_SUB,), jnp.int32).at[dest.reshape(-1)].set(tok)
    n_used = (pend[-1] // MOE_SUB).astype(jnp.int32).reshape(1)
    n_items_max = N_EXPERTS + (n_slots // MOE_SUB) // MOE_SUBMAX
    per_e = (nsub + MOE_SUBMAX - 1) // MOE_SUBMAX
    iend = jnp.cumsum(per_e)
    w = jnp.arange(n_items_max, dtype=jnp.int32)
    ew = jnp.minimum(jnp.searchsorted(iend, w, side='right'), N_EXPERTS - 1).astype(jnp.int32)
    local = w - (iend[ew] - per_e[ew])
    live = w < iend[-1]
    item_nsub = jnp.where(live, jnp.clip(nsub[ew] - local * MOE_SUBMAX, 0, MOE_SUBMAX), 0).astype(jnp.int32)
    item_row0 = jnp.where(live, pstart[ew] + local * (MOE_SUBMAX * MOE_SUB), 0).astype(jnp.int32)
    last_e = ew[jnp.maximum(iend[-1] - 1, 0)]
    item_e = jnp.where(live, ew, last_e).astype(jnp.int32)
    return dest.reshape(-1).astype(jnp.int32), slot_tok, n_used, item_e, item_row0, item_nsub


def _repack_w_in(w_in_l):
    parts, width = [], 0
    for name in _ORDER:
        off, size = _SRC[name]
        parts.append(w_in_l[:, off:off + size])
        width += size
        if size % LANES:
            parts.append(jnp.zeros((D_MODEL, LANES - size % LANES), w_in_l.dtype))
            width += LANES - size % LANES
    parts.append(jnp.zeros((D_MODEL, PROJ_W - width), w_in_l.dtype))
    return jnp.concatenate(parts, axis=1).astype(BF16)


def _cover_matrix(S):
    n_strip = S // CMP_STRIDE
    n_cmp = (S - CMP_BLOCK) // CMP_STRIDE + 1
    n = np.arange(n_strip)[:, None]
    j = np.arange(LANES)[None, :]
    start, end = n * CMP_STRIDE, n * CMP_STRIDE + CMP_BLOCK - 1
    cov = (start <= j * SEL_BLOCK + SEL_BLOCK - 1) & (end >= j * SEL_BLOCK) & (n < n_cmp) & (j < S // SEL_BLOCK)
    return jnp.asarray(cov.astype(np.float32).T)


def _strips(proj, cb):
    S = proj.shape[0]
    t = proj[:, cb * LANES:(cb + NSA_KV_HEADS) * LANES].reshape(S, NSA_KV_HEADS, HD)
    return t.transpose(1, 0, 2).reshape(NSA_KV_HEADS, S // CMP_STRIDE, CMP_STRIDE * HD)


def kernel(x, c, ada_w, ada_b, norm_mix_g, w_in, conv_dw_w, conv_dw_b, conv_ln_g, conv_ln_b, nsa_pe_k, nsa_pe_v, nsa_cmp_k_w1, nsa_cmp_k_w2, nsa_cmp_v_w1, nsa_cmp_v_w2, ret_gn_g, ret_gn_b, w_out, norm_ffn_g, router_w, router_b, moe_w_gate_up, moe_b_gate_up, moe_w_down, moe_b_down, final_norm_g):
    B, S, _ = x.shape
    assert B == 1 and c.shape[0] == 1
    x2 = x.reshape(S, D_MODEL)
    mod = _ada_mod(c, ada_w, ada_b).reshape(DEPTH, 6, 1, D_MODEL)
    tabs = _rotary_tables(S)
    cover = _cover_matrix(S)
    final_g = final_norm_g.reshape(1, D_MODEL)
    for l in range(DEPTH):
        shift_a, scale_a, gate_a, shift_f, scale_f, gate_f = [mod[l, i] for i in range(6)]
        proj = _in_proj(x2, norm_mix_g[l].reshape(1, -1), scale_a, shift_a, _repack_w_in(w_in[l]), tabs)
        y_conv = _conv_group(proj, conv_dw_w[l], conv_dw_b[l], conv_ln_g[l], conv_ln_b[l])
        kc = _compress(_strips(proj, CB_KC), nsa_pe_k[l], nsa_cmp_k_w1[l], nsa_cmp_k_w2[l])
        vc = _compress(_strips(proj, CB_VC), nsa_pe_v[l], nsa_cmp_v_w1[l], nsa_cmp_v_w2[l])
        oc, selbias = _cmp_attention(proj, kc, vc, cover)
        ksa, vsb, kwb, vwb = _kv_prep(proj)
        y_nsa = _nsa_attention(proj, selbias, oc, ksa, vsb, kwb, vwb)
        y_ret = _retention(proj, ret_gn_g[l], ret_gn_b[l])
        x2 = _out_proj(x2, y_conv, y_nsa, y_ret, w_out[l].astype(BF16), gate_a)
        h2, route, counts = _router(x2, norm_ffn_g[l].reshape(1, -1), scale_f, shift_f, router_w[l], router_b[l])
        dest, slot_tok, n_used, item_e, item_row0, item_nsub = _moe_plan(route, counts)
        ys = _experts(h2, slot_tok, item_e, item_row0, item_nsub, moe_w_gate_up, moe_b_gate_up[l], moe_w_down,
                      moe_b_down[l], l)
        x2 = _combine(x2, route, gate_f, final_g, ys, dest, final_norm=(l == DEPTH - 1))
    return x2.reshape(B, S, D_MODEL)
```
